```python
import math
import jax, jax.numpy as jnp
from jax import lax
import numpy as np

D_MODEL = 1024
BATCH = 8
SEQ = 2048
DEPTH = 2

GRID_W = 64
CTX_LEN = 256
N_EVEN = (DEPTH + 1) // 2
N_ODD = DEPTH // 2
NORM_EPS = 1e-6
ROPE_BASE = 10000.0

RET_HEADS = 8
RET_HD = 64
RET_W = RET_HEADS * RET_HD
RET_CHUNK = 128
RWKV_HEADS = 8
RWKV_HD = 64
RWKV_W = RWKV_HEADS * RWKV_HD
DECAY_LORA = 64
A_LORA = 64
RWKV_GN_EPS = 64e-5
HY_W = 512
HY_ORDER = 2
HY_BANDS = 16
HY_EMB = 1 + 2 * HY_BANDS
HY_FFN = 64
HY_SHORT = 3
HY_TARGET = 1e-2
HY_FAST_PCT = 0.3
HY_SLOW_PCT = 1.5
NA_HEADS = 8
NA_HD = 64
NA_W = NA_HEADS * NA_HD
NA_WIN_R = 8
NA_WIN_C = 16

E_RQ = 0
E_RK = RET_W
E_RV = 2 * RET_W
E_RZ = 3 * RET_W
E_WZ = 4 * RET_W
E_SHIFT = E_WZ + RWKV_W
SHIFT_W = 3 * RWKV_W + 2 * DECAY_LORA + 2 * A_LORA
EVEN_IN = E_SHIFT + SHIFT_W
S_K = RWKV_W
S_V = 2 * RWKV_W
S_WD = 3 * RWKV_W
S_AD = S_WD + 2 * DECAY_LORA
O_HZ = 3 * HY_W
O_NQ = 4 * HY_W
O_NK = O_NQ + NA_W
O_NV = O_NK + NA_W
O_NZ = O_NV + NA_W
ODD_IN = O_NZ + NA_W

kernel_name = 'hybrid_retention_rwkv7_hyena_natten_diffusion'

F32 = jnp.float32


def rmsnorm(x, w):
    xf = x.astype(F32)
    y = xf * lax.rsqrt(jnp.mean(xf * xf, axis=-1, keepdims=True) + NORM_EPS)
    return (y * w.astype(F32)).astype(x.dtype)


def split_heads(t, n_heads):
    b, l, _ = t.shape
    return t.reshape(b, l, n_heads, -1).transpose(0, 2, 1, 3)


def merge_heads(t):
    b, h, l, d = t.shape
    return t.transpose(0, 2, 1, 3).reshape(b, l, h * d)


def axial_rope(length, dim):
    t = jnp.arange(length)
    row = (t // GRID_W).astype(F32)
    col = (t % GRID_W).astype(F32)
    nf = dim // 4
    inv = ROPE_BASE ** (-jnp.arange(nf, dtype=F32) / nf)
    ang = jnp.concatenate([row[:, None] * inv, col[:, None] * inv], axis=-1)
    return jnp.cos(ang), jnp.sin(ang)


def apply_rope(x, cos, sin):
    half = x.shape[-1] // 2
    x1, x2 = x[..., :half], x[..., half:]
    return jnp.concatenate([x1 * cos - x2 * sin, x1 * sin + x2 * cos], axis=-1)


def head_rms(y):
    return y * lax.rsqrt(jnp.mean(y * y, axis=-1, keepdims=True) + NORM_EPS)


def retention_scan(q, k, v, log_g, s0):
    b, h, l, dk = q.shape
    dv = v.shape[-1]
    n = l // RET_CHUNK
    qc = q.reshape(b, h, n, RET_CHUNK, dk)
    kc = k.reshape(b, h, n, RET_CHUNK, dk)
    vc = v.reshape(b, h, n, RET_CHUNK, dv)
    pos = jnp.arange(RET_CHUNK, dtype=F32)
    diff = pos[:, None] - pos[None, :]
    lower = diff >= 0
    dmat = jnp.where(lower, jnp.exp(jnp.where(lower, diff, 0.0)[None] * log_g[:, None, None]), 0.0)
    scores = jnp.einsum('bhncd,bhnsd->bhncs', qc, kc) * dmat[None, :, None]
    y_intra = jnp.einsum('bhncs,bhnse->bhnce', scores, vc)
    k_decay = jnp.exp((RET_CHUNK - 1 - pos)[None] * log_g[:, None])
    kv = jnp.einsum('bhnsd,bhnse->nbhde', kc * k_decay[None, :, None, :, None], vc)
    g_chunk = jnp.exp(RET_CHUNK * log_g)[None, :, None, None]

    def step(s, kv_n):
        return g_chunk * s + kv_n, s

    s_fin, s_prev = lax.scan(step, s0, kv)
    q_decay = jnp.exp((pos + 1.0)[None] * log_g[:, None])
    y_cross = jnp.einsum('bhncd,nbhde->bhnce', qc * q_decay[None, :, None, :, None], s_prev)
    return (y_intra + y_cross).reshape(b, h, l, dv), s_fin


def token_shift_mix(u, mu):
    prev = jnp.pad(u[:, :-1], ((0, 0), (1, 0), (0, 0)))
    nxt = jnp.pad(u[:, 1:], ((0, 0), (0, 1), (0, 0)))
    return u + mu * (0.5 * (prev + nxt) - u)


def rwkv_prep(s, kk_w, ka, w0, w2, a0, a2):
    s = s.astype(F32)
    b, l, _ = s.shape
    hd = lambda t: t.reshape(b, l, RWKV_HEADS, RWKV_HD)
    r = s[..., :S_K]
    k = s[..., S_K:S_V]
    v = s[..., S_V:S_WD]
    wd = s[..., S_WD:S_AD].reshape(b, l, 2, DECAY_LORA)
    ad = s[..., S_AD:].reshape(b, l, 2, A_LORA)
    kk = hd(k * kk_w)
    kk = kk / jnp.maximum(jnp.sqrt(jnp.sum(kk * kk, axis=-1, keepdims=True)), 1e-12)
    dirs = []
    for d in range(2):
        w_log = -jax.nn.softplus(-(w0[d] + jnp.tanh(wd[:, :, d]) @ w2[d])) - 0.5
        a = jax.nn.sigmoid(a0[d] + ad[:, :, d] @ a2[d])
        dirs.append((hd(jnp.exp(-jnp.exp(w_log))), hd(k * (1.0 + (a - 1.0) * ka)), hd(a)))
    return hd(r), hd(v), kk, dirs


def rwkv7_scan(r, w, k, v, kk, a, s0):
    def step(s, inp):
        r_t, w_t, k_t, v_t, kk_t, a_t = inp
        sa = jnp.einsum('bhvk,bhk->bhv', s, -kk_t)
        s = s * w_t[:, :, None, :] + sa[..., None] * (kk_t * a_t)[:, :, None, :] + v_t[..., None] * k_t[:, :, None, :]
        return s, jnp.einsum('bhvk,bhk->bhv', s, r_t)

    xs = tuple(jnp.swapaxes(t.astype(F32), 0, 1) for t in (r, w, k, v, kk, a))
    s_fin, ys = lax.scan(step, s0, xs)
    return jnp.swapaxes(ys, 0, 1), s_fin


def even_layer(h_l, h_c, w_in, w_out, ret_decay, rw_mu, rw_w0, rw_w2, rw_a0, rw_a2, rw_kk, rw_ka, rw_rk,
               rw_ln_w, rw_ln_b, cos, sin, with_ctx):
    b = h_l.shape[0]
    u_l = h_l @ w_in
    u_c = h_c @ w_in

    def ret_qkv(u, rotate):
        q = split_heads(u[..., E_RQ:E_RK], RET_HEADS).astype(F32)
        k = split_heads(u[..., E_RK:E_RV], RET_HEADS).astype(F32) * RET_HD ** -0.5
        v = split_heads(u[..., E_RV:E_RZ], RET_HEADS).astype(F32)
        if rotate:
            q, k = apply_rope(q, cos, sin), apply_rope(k, cos, sin)
        return q, k, v

    log_g = -jnp.exp(ret_decay.astype(F32))
    lat = ret_qkv(u_l, True)
    con = ret_qkv(u_c, False)
    s0 = jnp.zeros((b, RET_HEADS, RET_HD, RET_HD), F32)
    flip_h = lambda t: t[:, :, ::-1]
    ret_c_f, st_f = retention_scan(*con, log_g[0], s0)
    ret_l_f, _ = retention_scan(*lat, log_g[0], st_f)
    ret_c_b, st_b = retention_scan(*[flip_h(t) for t in con], log_g[1], s0)
    ret_l_b, _ = retention_scan(*[flip_h(t) for t in lat], log_g[1], st_b)
    ret_l = ret_l_f + flip_h(ret_l_b)
    ret_c = ret_c_f + flip_h(ret_c_b)

    prep_l = rwkv_prep(token_shift_mix(u_l[..., E_SHIFT:], rw_mu), rw_kk, rw_ka, rw_w0, rw_w2, rw_a0, rw_a2)
    prep_c = rwkv_prep(token_shift_mix(u_c[..., E_SHIFT:], rw_mu), rw_kk, rw_ka, rw_w0, rw_w2, rw_a0, rw_a2)

    def dir_inputs(p, d):
        r, v, kk, dirs = p
        decay, k_d, a = dirs[d]
        return (r, decay, k_d, v, kk, a)

    flip_t = lambda t: t[:, ::-1]
    sw = jnp.zeros((b, RWKV_HEADS, RWKV_HD, RWKV_HD), F32)
    rw_c_f, sf = rwkv7_scan(*dir_inputs(prep_c, 0), sw)
    rw_l_f, _ = rwkv7_scan(*dir_inputs(prep_l, 0), sf)
    rw_c_b, sb = rwkv7_scan(*[flip_t(t) for t in dir_inputs(prep_c, 1)], sw)
    rw_l_b, _ = rwkv7_scan(*[flip_t(t) for t in dir_inputs(prep_l, 1)], sb)
    rw_l = rw_l_f + flip_t(rw_l_b)
    rw_c = rw_c_f + flip_t(rw_c_b)

    def combine(u, ret_y, rw_y, prep):
        r, v, _, dirs = prep
        bl, l = u.shape[0], u.shape[1]
        ret_o = merge_heads(head_rms(ret_y)) * jax.nn.silu(u[..., E_RZ:E_WZ])
        mean = jnp.mean(rw_y, axis=-1, keepdims=True)
        var = jnp.mean(jnp.square(rw_y - mean), axis=-1, keepdims=True)
        gn = ((rw_y - mean) * lax.rsqrt(var + RWKV_GN_EPS)).reshape(bl, l, RWKV_W) * rw_ln_w + rw_ln_b
        bonus = sum(jnp.sum(r * dirs[d][1] * rw_rk, axis=-1, keepdims=True) * v for d in range(2))
        rw_o = (gn + bonus.reshape(bl, l, RWKV_W)) * jax.nn.silu(u[..., E_WZ:E_SHIFT])
        return jnp.concatenate([ret_o, rw_o], axis=-1) @ w_out

    y_l = combine(u_l, ret_l, rw_l, prep_l)
    y_c = combine(u_c, ret_c, rw_c, prep_c) if with_ctx else None
    return y_l, y_c


def short_conv(u, w, bias):
    l = u.shape[1]
    p = jnp.pad(u, ((0, 0), (1, 1), (0, 0)))
    return p[:, :l] * w[0] + p[:, 1:l + 1] * w[1] + p[:, 2:] * w[2] + bias


def hyena_filters(length, w1, b1, f1, w2, b2, f2, w3):
    t = jnp.linspace(0.0, 1.0, length)[:, None]
    bands = jnp.linspace(1e-4, HY_BANDS - 1, HY_BANDS)
    ang = (2.0 * math.pi / length) * jnp.arange(length, dtype=F32)[:, None] * bands[None]
    z = jnp.concatenate([t, jnp.cos(ang), -jnp.sin(ang)], axis=-1)
    hid = jnp.sin(f1 * (z @ w1 + b1))
    hid = jnp.sin(f2 * (hid @ w2 + b2))
    h = (hid @ w3).reshape(length, HY_ORDER, 2, HY_W)
    deltas = jnp.abs(jnp.linspace(math.log(HY_TARGET) / HY_SLOW_PCT, math.log(HY_TARGET) / HY_FAST_PCT, HY_W))
    return (h * jnp.exp(-t * deltas)[:, None, None, :]).astype(F32)


def two_sided_fftconv(u, h_f, h_b):
    l = u.shape[1]
    kern = jnp.concatenate([h_f, jnp.zeros_like(h_f[:1]), h_b[1:][::-1]], axis=0)
    uf = jnp.fft.rfft(u, n=2 * l, axis=1)
    kf = jnp.fft.rfft(kern, axis=0)
    return jnp.fft.irfft(uf * kf[None], n=2 * l, axis=1)[:, :l]


def neighbourhood_attention(q, k, v, k_ctx, v_ctx, rpb):
    b, h, l, d = q.shape
    rows = l // GRID_W
    wr = min(NA_WIN_R, rows)
    r = jnp.arange(rows)
    c = jnp.arange(GRID_W)
    row_idx = jnp.clip(r - NA_WIN_R // 2, 0, rows - wr)[:, None] + jnp.arange(wr)[None]
    col_idx = jnp.clip(c - NA_WIN_C // 2, 0, GRID_W - NA_WIN_C)[:, None] + jnp.arange(NA_WIN_C)[None]
    qi = c[:, None]
    qg = q.reshape(b, h, rows, GRID_W, d)
    k_rows = k.reshape(b, h, rows, GRID_W, d)[:, :, row_idx]
    v_rows = v.reshape(b, h, rows, GRID_W, d)[:, :, row_idx]
    s_rows = jnp.einsum('bhrqd,bhrwkd->bhrwqk', qg, k_rows)
    s_win = s_rows[..., qi, col_idx]
    bias = rpb[:, (row_idx - r[:, None] + NA_WIN_R - 1)[:, :, None, None],
               (col_idx - c[:, None] + NA_WIN_C - 1)[None, None]]
    s_win = (s_win + bias[None]).transpose(0, 1, 2, 4, 3, 5).reshape(b, h, rows, GRID_W, wr * NA_WIN_C)
    s_ctx = jnp.einsum('bhrqd,bhcd->bhrqc', qg, k_ctx)
    p = jax.nn.softmax(jnp.concatenate([s_win, s_ctx], axis=-1).astype(F32), axis=-1)
    p_win = p[..., :wr * NA_WIN_C].reshape(b, h, rows, GRID_W, wr, NA_WIN_C).transpose(0, 1, 2, 4, 3, 5)
    p_ctx = p[..., wr * NA_WIN_C:]
    p_rows = jnp.zeros(s_rows.shape, p.dtype).at[..., qi, col_idx].set(p_win)
    out = jnp.einsum('bhrwqk,bhrwkd->bhrqd', p_rows, v_rows) + jnp.einsum('bhrqc,bhcd->bhrqd', p_ctx, v_ctx)
    return out.reshape(b, h, l, d)


def dense_attention(q, k, v):
    p = jax.nn.softmax(jnp.einsum('bhqd,bhkd->bhqk', q, k).astype(F32), axis=-1)
    return jnp.einsum('bhqk,bhkd->bhqd', p, v)


def odd_layer(h_l, h_c, w_in, w_out, conv_w, conv_b, f_w1, f_b1, f_f1, f_w2, f_b2, f_f2, f_w3, hy_bias, rpb,
              with_ctx):
    u_l = h_l @ w_in
    u_c = h_c @ w_in if with_ctx else None
    kv_c = u_c[..., O_NK:O_NZ] if with_ctx else h_c @ w_in[:, O_NK:O_NZ]
    k_c = split_heads(kv_c[..., :NA_W], NA_HEADS)
    v_c = split_heads(kv_c[..., NA_W:], NA_HEADS)

    def hyena(u):
        p = short_conv(u[..., :O_HZ], conv_w, conv_b).astype(F32)
        v, x1, x2 = p[..., :HY_W], p[..., HY_W:2 * HY_W], p[..., 2 * HY_W:]
        filt = hyena_filters(u.shape[1], f_w1, f_b1, f_f1, f_w2, f_b2, f_f2, f_w3)
        z = v
        for o, g in enumerate((x1, x2)):
            z = g * (two_sided_fftconv(z, filt[:, o, 0], filt[:, o, 1]) + z * hy_bias[o])
        return z * jax.nn.silu(u[..., O_HZ:O_NQ])

    q_l = split_heads(u_l[..., O_NQ:O_NK], NA_HEADS) * NA_HD ** -0.5
    k_l = split_heads(u_l[..., O_NK:O_NV], NA_HEADS)
    v_l = split_heads(u_l[..., O_NV:O_NZ], NA_HEADS)
    na_l = merge_heads(neighbourhood_attention(q_l, k_l, v_l, k_c, v_c, rpb)) * jax.nn.silu(u_l[..., O_NZ:])
    y_l = jnp.concatenate([hyena(u_l), na_l], axis=-1) @ w_out
    y_c = None
    if with_ctx:
        q_c = split_heads(u_c[..., O_NQ:O_NK], NA_HEADS) * NA_HD ** -0.5
        na_c = merge_heads(dense_attention(q_c, k_c, v_c)) * jax.nn.silu(u_c[..., O_NZ:])
        y_c = jnp.concatenate([hyena(u_c), na_c], axis=-1) @ w_out
    return y_l, y_c


def setup_inputs(seed: int = 0) -> dict:
    key = jax.random.key(seed)
    ks = iter(jax.random.split(key, 40))
    nrm = lambda shape, s: jax.random.normal(next(ks), shape, F32) * s
    ret_base = jnp.asarray(np.log(-np.log(1.0 - 2.0 ** (-5.0 - np.arange(RET_HEADS)))), F32)
    return {
        'x': nrm((BATCH, SEQ, D_MODEL), 1.0),
        'c': nrm((BATCH, D_MODEL), 1.0),
        'ctx': nrm((BATCH, CTX_LEN, D_MODEL), 1.0),
        'c_ctx': nrm((D_MODEL,), 1.0),
        'ada_w': nrm((DEPTH, D_MODEL, 3 * D_MODEL), 0.5 * D_MODEL ** -0.5),
        'ada_b': nrm((DEPTH, 3 * D_MODEL), 0.02),
        'norm_w': 1.0 + nrm((DEPTH, D_MODEL), 0.02),
        'final_norm_w': 1.0 + nrm((D_MODEL,), 0.02),
        'even_w_in': nrm((N_EVEN, D_MODEL, EVEN_IN), D_MODEL ** -0.5),
        'even_w_out': nrm((N_EVEN, RET_W + RWKV_W, D_MODEL), (RET_W + RWKV_W) ** -0.5),
        'ret_decay': ret_base + nrm((N_EVEN, 2, RET_HEADS), 0.05),
        'rw_mu': jax.random.uniform(next(ks), (N_EVEN, SHIFT_W), F32),
        'rw_w0': jnp.linspace(-6.0, 1.0, RWKV_W) + nrm((N_EVEN, 2, RWKV_W), 0.1),
        'rw_w2': nrm((N_EVEN, 2, DECAY_LORA, RWKV_W), 0.1),
        'rw_a0': nrm((N_EVEN, 2, RWKV_W), 0.1),
        'rw_a2': nrm((N_EVEN, 2, A_LORA, RWKV_W), 0.1),
        'rw_kk': 0.85 + nrm((N_EVEN, RWKV_W), 0.02),
        'rw_ka': 1.0 + nrm((N_EVEN, RWKV_W), 0.02),
        'rw_rk': nrm((N_EVEN, RWKV_HEADS, RWKV_HD), 0.1),
        'rw_ln_w': 1.0 + nrm((N_EVEN, RWKV_W), 0.02),
        'rw_ln_b': nrm((N_EVEN, RWKV_W), 0.02),
        'odd_w_in': nrm((N_ODD, D_MODEL, ODD_IN), D_MODEL ** -0.5),
        'odd_w_out': nrm((N_ODD, HY_W + NA_W, D_MODEL), (HY_W + NA_W) ** -0.5),
        'hy_conv_w': nrm((N_ODD, HY_SHORT, 3 * HY_W), 0.5),
        'hy_conv_b': nrm((N_ODD, 3 * HY_W), 0.02),
        'hy_w1': nrm((N_ODD, HY_EMB, HY_FFN), HY_EMB ** -0.5),
        'hy_b1': nrm((N_ODD, HY_FFN), 0.1),
        'hy_f1': 1.0 + nrm((N_ODD, HY_FFN), 0.05),
        'hy_w2': nrm((N_ODD, HY_FFN, HY_FFN), HY_FFN ** -0.5),
        'hy_b2': nrm((N_ODD, HY_FFN), 0.1),
        'hy_f2': 1.0 + nrm((N_ODD, HY_FFN), 0.05),
        'hy_w3': nrm((N_ODD, HY_FFN, HY_ORDER * 2 * HY_W), 0.01),
        'hy_bias': nrm((N_ODD, HY_ORDER, HY_W), 0.5),
        'na_rpb': nrm((N_ODD, NA_HEADS, 2 * NA_WIN_R - 1, 2 * NA_WIN_C - 1), 0.1),
    }


def reference(x, c, ctx, c_ctx, ada_w, ada_b, norm_w, final_norm_w, even_w_in, even_w_out, ret_decay, rw_mu,
              rw_w0, rw_w2, rw_a0, rw_a2, rw_kk, rw_ka, rw_rk, rw_ln_w, rw_ln_b, odd_w_in, odd_w_out, hy_conv_w,
              hy_conv_b, hy_w1, hy_b1, hy_f1, hy_w2, hy_b2, hy_f2, hy_w3, hy_bias, na_rpb):
    cos, sin = axial_rope(x.shape[1], RET_HD)
    sc = jax.nn.silu(c)
    scc = jax.nn.silu(c_ctx)
    h = x
    hc = ctx
    for i in range(DEPTH):
        with_ctx = i < DEPTH - 1
        shift, scale, gate = jnp.split(sc @ ada_w[i] + ada_b[i], 3, axis=-1)
        shift_c, scale_c, gate_c = jnp.split(scc @ ada_w[i] + ada_b[i], 3, axis=-1)
        n_l = rmsnorm(h, norm_w[i]) * (1.0 + scale[:, None]) + shift[:, None]
        n_c = rmsnorm(hc, norm_w[i]) * (1.0 + scale_c) + shift_c
        j = i // 2
        if i % 2 == 0:
            y_l, y_c = even_layer(n_l, n_c, even_w_in[j], even_w_out[j], ret_decay[j], rw_mu[j], rw_w0[j], rw_w2[j],
                                  rw_a0[j], rw_a2[j], rw_kk[j], rw_ka[j], rw_rk[j], rw_ln_w[j], rw_ln_b[j],
                                  cos, sin, with_ctx)
        else:
            y_l, y_c = odd_layer(n_l, n_c, odd_w_in[j], odd_w_out[j], hy_conv_w[j], hy_conv_b[j], hy_w1[j],
                                 hy_b1[j], hy_f1[j], hy_w2[j], hy_b2[j], hy_f2[j], hy_w3[j], hy_bias[j], na_rpb[j],
                                 with_ctx)
        h = h + (gate[:, None] * y_l).astype(h.dtype)
        if with_ctx:
            hc = hc + (gate_c * y_c).astype(hc.dtype)
    return rmsnorm(h, final_norm_w)
```

```python
import functools
import math

import jax
import jax.numpy as jnp
from jax import lax
from jax.experimental import pallas as pl
from jax.experimental.pallas import tpu as pltpu

F32 = jnp.float32
BF16 = jnp.bfloat16

GRID_W = 64
NORM_EPS = 1e-6
ROPE_BASE = 10000.0
HEADS = 8
HD = 64
W = HEADS * HD
LORA = 64
RWKV_GN_EPS = 64e-5
HY_BANDS = 16
HY_TARGET = 1e-2
HY_FAST_PCT = 0.3
HY_SLOW_PCT = 1.5
NA_WIN_R = 8
NA_WIN_C = 16
LANES = 128
RET_CHUNK = 128
RW_CHUNK = 64
ROW_TILE = 256
VMEM_LIMIT = 56 * 1024 * 1024


def _cparams(*sem):
    return pltpu.CompilerParams(dimension_semantics=sem, vmem_limit_bytes=VMEM_LIMIT)


def _bdot(a, b):
    return jnp.dot(a.astype(BF16), b.astype(BF16), preferred_element_type=F32)


def _bdot_nt(a, b):
    return lax.dot_general(a.astype(BF16), b.astype(BF16), (((1,), (1,)), ((), ())), preferred_element_type=F32)


def _bdot_tn(a, b):
    return lax.dot_general(a.astype(BF16), b.astype(BF16), (((0,), (0,)), ((), ())), preferred_element_type=F32)


def _split3(x):
    hi = x.astype(BF16)
    r1 = x - hi.astype(F32)
    mid = r1.astype(BF16)
    lo = (r1 - mid.astype(F32)).astype(BF16)
    return hi, mid, lo


def _dot_exact_lhs(m, x):
    mb = m.astype(BF16)
    hi, mid, lo = _split3(x)
    d = lambda p: jnp.dot(mb, p, preferred_element_type=F32)
    return d(hi) + d(mid) + d(lo)


def _dot_exact_rhs(x, m):
    mb = m.astype(BF16)
    hi, mid, lo = _split3(x)
    d = lambda p: jnp.dot(p, mb, preferred_element_type=F32)
    return d(hi) + d(mid) + d(lo)


def _dot_hi(a, b):
    ah = a.astype(BF16)
    al = (a - ah.astype(F32)).astype(BF16)
    bh = b.astype(BF16)
    bl = (b - bh.astype(F32)).astype(BF16)
    d = lambda p, q: jnp.dot(p, q, preferred_element_type=F32)
    return d(ah, bh) + d(ah, bl) + d(al, bh)


def _silu(x):
    return x * (1.0 / (1.0 + jnp.exp(-x)))


def _sigmoid(x):
    return 1.0 / (1.0 + jnp.exp(-x))


def _softplus(x):
    return jnp.maximum(x, 0.0) + jnp.log(1.0 + jnp.exp(-jnp.abs(x)))


def _ada_kernel(c_ref, w_ref, b_ref, o_ref):
    o_ref[...] = _dot_hi(_silu(c_ref[...]), w_ref[...]) + b_ref[...]


def _ada_mod(cond, ada_w, ada_b):
    depth, d, d3 = ada_w.shape
    rows = cond.shape[0]
    return pl.pallas_call(
        _ada_kernel,
        grid=(depth, d3 // d),
        in_specs=[
            pl.BlockSpec((rows, d), lambda i, j: (0, 0)),
            pl.BlockSpec((None, d, d), lambda i, j: (i, 0, j)),
            pl.BlockSpec((None, 1, d), lambda i, j: (i, 0, j)),
        ],
        out_specs=pl.BlockSpec((None, rows, d), lambda i, j: (i, 0, j)),
        out_shape=jax.ShapeDtypeStruct((depth, rows, d3), F32),
        compiler_params=_cparams("parallel", "parallel"),
        name="ada_mod",
    )(cond, ada_w, ada_b.reshape(depth, 1, d3))


def _norm_proj_kernel(h_ref, nw_ref, mod_ref, w_ref, o_ref):
    x = h_ref[...]
    y = x * lax.rsqrt(jnp.mean(x * x, axis=-1, keepdims=True) + NORM_EPS) * nw_ref[...]
    n = y * (1.0 + mod_ref[1:2, :]) + mod_ref[0:1, :]
    o_ref[...] = _bdot(n, w_ref[...]).astype(o_ref.dtype)


def _norm_proj(h, norm_w, mod, w, n_lat, out_dtype=F32):
    b, t, d = h.shape
    n = w.shape[1]
    tm = ROW_TILE
    lat_tiles = n_lat // tm
    return pl.pallas_call(
        _norm_proj_kernel,
        grid=(b, t // tm),
        in_specs=[
            pl.BlockSpec((None, tm, d), lambda bi, i: (bi, i, 0)),
            pl.BlockSpec((1, d), lambda bi, i: (0, 0)),
            pl.BlockSpec((None, None, 3, d), lambda bi, i: (bi, (i >= lat_tiles).astype(jnp.int32), 0, 0)),
            pl.BlockSpec((d, n), lambda bi, i: (0, 0)),
        ],
        out_specs=pl.BlockSpec((None, tm, n), lambda bi, i: (bi, i, 0)),
        out_shape=jax.ShapeDtypeStruct((b, t, n), out_dtype),
        compiler_params=_cparams("parallel", "parallel"),
        name="norm_proj",
    )(h, norm_w.reshape(1, d), mod, w)


def _lane_ids(shape):
    return lax.broadcasted_iota(jnp.int32, shape, len(shape) - 1)


def _row_ids(shape):
    return lax.broadcasted_iota(jnp.int32, shape, len(shape) - 2)


def _head_sum(x, lane_lo):
    sa = jnp.sum(jnp.where(lane_lo, x, 0.0), axis=-1, keepdims=True)
    sb = jnp.sum(jnp.where(lane_lo, 0.0, x), axis=-1, keepdims=True)
    return jnp.where(lane_lo, sa, sb)


def _with_neighbours(ref, c0, c, n_rows, lat_rows):
    row = lax.broadcasted_iota(jnp.int32, (c, 1), 0)
    x = ref[pl.ds(c0, c), :]
    p0 = pl.multiple_of(jnp.maximum(c0 - 8, 0), 8)
    n0 = pl.multiple_of(jnp.minimum(c0 + c, n_rows - 8), 8)
    prev_row = ref[pl.ds(p0, 8), :][7:8, :]
    next_row = ref[pl.ds(n0, 8), :][0:1, :]
    has_prev = jnp.logical_and(c0 != 0, c0 != lat_rows)
    has_next = jnp.logical_and(c0 + c != lat_rows, c0 + c != n_rows)
    prev_row = jnp.where(has_prev, prev_row, 0.0)
    next_row = jnp.where(has_next, next_row, 0.0)
    prev = jnp.where(row == 0, prev_row, pltpu.roll(x, 1, 0))
    nxt = jnp.where(row == c - 1, next_row, pltpu.roll(x, c - 1, 0))
    return x, prev, nxt


def _scan_chunk_id(i, n_lat, n_ctx, reverse):
    if reverse:
        return jnp.where(i < n_ctx, n_lat + n_ctx - 1 - i, n_lat - 1 - (i - n_ctx))
    return jnp.where(i < n_ctx, n_lat + i, i - n_ctx)


def _ret_kernel(q_ref, k_ref, v_ref, z_ref, cos_ref, sin_ref, rd_ref, o_ref, st_ref, *, n_lat, n_ctx):
    c = RET_CHUNK
    lane = _lane_ids((1, LANES))
    lane_lo = lane < HD
    rope_lo = (lane % HD) < (HD // 2)
    pos = lax.broadcasted_iota(jnp.int32, (c, 1), 0).astype(F32)
    ti = lax.broadcasted_iota(jnp.int32, (c, c), 0)
    si = lax.broadcasted_iota(jnp.int32, (c, c), 1)
    blockdiag = (_row_ids((LANES, LANES)) < HD) == (_lane_ids((LANES, LANES)) < HD)

    def rope(x, rows):
        partner = jnp.where(rope_lo, pltpu.roll(x, LANES - HD // 2, 1), pltpu.roll(x, HD // 2, 1))
        return x * cos_ref[rows, :] + partner * sin_ref[rows, :]

    for d in range(2):
        lg_a = -jnp.exp(rd_ref[d, 0])
        lg_b = -jnp.exp(rd_ref[d, 1])
        lgv = jnp.where(lane_lo, lg_a, lg_b)
        if d == 0:
            diff = (ti - si).astype(F32)
            q_dec = jnp.exp((pos + 1.0) * lgv)
            k_dec = jnp.exp((c - 1.0 - pos) * lgv)
        else:
            diff = (si - ti).astype(F32)
            q_dec = jnp.exp((c - pos) * lgv)
            k_dec = jnp.exp(pos * lgv)
        keep = diff >= 0
        dmat_a = jnp.where(keep, jnp.exp(jnp.where(keep, diff, 0.0) * lg_a), 0.0)
        dmat_b = jnp.where(keep, jnp.exp(jnp.where(keep, diff, 0.0) * lg_b), 0.0)
        g_chunk = jnp.exp(c * lgv)
        st_ref[...] = jnp.zeros_like(st_ref)

        def step(i, carry, d=d, q_dec=q_dec, k_dec=k_dec, dmat_a=dmat_a, dmat_b=dmat_b, g_chunk=g_chunk):
            cid = _scan_chunk_id(i, n_lat, n_ctx, d == 1)
            rows = pl.ds(pl.multiple_of(cid * c, c), c)
            q = rope(q_ref[rows, :], rows)
            k = rope(k_ref[rows, :] * HD ** -0.5, rows)
            v = v_ref[rows, :]
            s_a = _bdot_nt(jnp.where(lane_lo, q, 0.0), k) * dmat_a
            s_b = _bdot_nt(jnp.where(lane_lo, 0.0, q), k) * dmat_b
            y = jnp.where(lane_lo, _bdot(s_a, v), _bdot(s_b, v)) + _bdot(q * q_dec, st_ref[...])
            st_ref[...] = st_ref[...] * g_chunk + jnp.where(blockdiag, _bdot_tn(k * k_dec, v), 0.0)
            if d == 0:
                o_ref[rows, :] = y.astype(o_ref.dtype)
            else:
                y = y + o_ref[rows, :].astype(F32)
                ms = _head_sum(y * y, lane_lo) * (1.0 / HD)
                o_ref[rows, :] = (y * lax.rsqrt(ms + NORM_EPS) * _silu(z_ref[rows, :])).astype(o_ref.dtype)
            return carry

        lax.fori_loop(0, n_lat + n_ctx, step, 0)


def _retention(u, cos_t, sin_t, ret_decay, n_lat_rows, col_q, col_k, col_v, col_z):
    b, t, _ = u.shape
    c = RET_CHUNK
    pairs = W // LANES
    rd = jnp.broadcast_to(ret_decay.astype(F32)[:, :, None, None], (2, HEADS, 1, LANES))
    blk = lambda col: pl.BlockSpec((None, t, LANES), lambda bi, p, col=col: (bi, 0, col // LANES + p))
    return pl.pallas_call(
        functools.partial(_ret_kernel, n_lat=n_lat_rows // c, n_ctx=(t - n_lat_rows) // c),
        grid=(b, pairs),
        in_specs=[
            blk(col_q), blk(col_k), blk(col_v), blk(col_z),
            pl.BlockSpec((t, LANES), lambda bi, p: (0, 0)),
            pl.BlockSpec((t, LANES), lambda bi, p: (0, 0)),
            pl.BlockSpec((2, 2, 1, LANES), lambda bi, p: (0, p, 0, 0)),
        ],
        out_specs=pl.BlockSpec((None, t, LANES), lambda bi, p: (bi, 0, p)),
        out_shape=jax.ShapeDtypeStruct((b, t, W), F32),
        scratch_shapes=[pltpu.VMEM((LANES, LANES), F32)],
        compiler_params=_cparams("parallel", "parallel"),
        name="retention",
    )(u, u, u, u, cos_t, sin_t, rd)


def _rope_tables(n_lat_rows, t):
    pos = jnp.arange(n_lat_rows)
    row = (pos // GRID_W).astype(F32)
    col = (pos % GRID_W).astype(F32)
    nf = HD // 4
    inv = ROPE_BASE ** (-jnp.arange(nf, dtype=F32) / nf)
    ang = jnp.concatenate([row[:, None] * inv, col[:, None] * inv], axis=-1)
    cos, sin = jnp.cos(ang), jnp.sin(ang)
    cos_h = jnp.concatenate([cos, cos], axis=-1)
    sin_h = jnp.concatenate([-sin, sin], axis=-1)
    reps = LANES // HD
    cos_t = jnp.concatenate([jnp.tile(cos_h, (1, reps)), jnp.ones((t - n_lat_rows, LANES), F32)], axis=0)
    sin_t = jnp.concatenate([jnp.tile(sin_h, (1, reps)), jnp.zeros((t - n_lat_rows, LANES), F32)], axis=0)
    return cos_t, sin_t


def _unit_tri_inverse(n):
    c = n.shape[0]
    eye = (lax.broadcasted_iota(jnp.int32, (c, c), 0) == lax.broadcasted_iota(jnp.int32, (c, c), 1)).astype(F32)
    t = eye + n
    npow = n
    k = 1
    while 2 * k < c:
        npow = _dot_hi(npow, npow)
        t = t + _dot_hi(t, npow)
        k *= 2
    return t


def _rwkv_kernel(r_ref, k_ref, v_ref, lo_ref, z_ref, mur_ref, muk_ref, muv_ref, mulo_ref, kkw_ref, ka_ref,
                 w0_ref, a0_ref, w2_ref, a2_ref, rk_ref, lnw_ref, lnb_ref, o_ref, st_ref, *, n_lat, n_ctx):
    c = RW_CHUNK
    t_rows = (n_lat + n_ctx) * c
    lat_rows = n_lat * c
    lane = _lane_ids((1, LANES))
    lane_lo = lane < HD
    row = lax.broadcasted_iota(jnp.int32, (c, 1), 0)
    ti = lax.broadcasted_iota(jnp.int32, (c, c), 0)
    si = lax.broadcasted_iota(jnp.int32, (c, c), 1)

    def shifted_mix(ref, mu, c0):
        x, prev, nxt = _with_neighbours(ref, c0, c, t_rows, lat_rows)
        return x + mu * (0.5 * (prev + nxt) - x)

    def a_gate(lo, d):
        return _sigmoid(a0_ref[d] + _bdot(lo[:, LANES:], a2_ref[d]))

    for d in range(2):
        if d == 0:
            strict = si < ti
            incl = si <= ti
        else:
            strict = si > ti
            incl = si >= ti
        tri = incl.astype(F32)
        st_ref[...] = jnp.zeros_like(st_ref)

        def step(i, carry, d=d, strict=strict, incl=incl, tri=tri):
            cid = _scan_chunk_id(i, n_lat, n_ctx, d == 1)
            c0 = pl.multiple_of(cid * c, c)
            rows = pl.ds(c0, c)
            r = shifted_mix(r_ref, mur_ref[...], c0)
            k = shifted_mix(k_ref, muk_ref[...], c0)
            v = shifted_mix(v_ref, muv_ref[...], c0)
            lo = shifted_mix(lo_ref, mulo_ref[...], c0)
            kk = k * kkw_ref[...]
            kk = kk / jnp.maximum(jnp.sqrt(_head_sum(kk * kk, lane_lo)), 1e-12)
            w_log = -_softplus(-(w0_ref[d] + _bdot(jnp.tanh(lo[:, :LANES]), w2_ref[d]))) - 0.5
            logw = -jnp.exp(w_log)
            a = a_gate(lo, d)
            k_d = k * (1.0 + (a - 1.0) * ka_ref[...])

            cum = _dot_exact_lhs(tri, logw)
            p = jnp.exp(cum)
            inv_p = jnp.exp(-cum)
            at = -kk * jnp.exp(cum - logw)
            bt = kk * a * inv_p
            kt = k_d * inv_p
            rt = r * p
            p_last = p[c - 1:c, :] if d == 0 else p[0:1, :]

            s0 = st_ref[...]
            x0 = _bdot_nt(at, s0)
            halves = []
            for lo_half in (True, False):
                sel = lane_lo if lo_half else jnp.logical_not(lane_lo)
                at_h = jnp.where(sel, at, 0.0)
                rt_h = jnp.where(sel, rt, 0.0)
                aab = jnp.where(strict, _bdot_nt(at_h, bt), 0.0)
                aak = jnp.where(strict, _bdot_nt(at_h, kt), 0.0)
                arb = jnp.where(incl, _bdot_nt(rt_h, bt), 0.0)
                ark = jnp.where(incl, _bdot_nt(rt_h, kt), 0.0)
                tinv = _unit_tri_inverse(aab)
                u_h = _dot_hi(tinv, x0 + _bdot(aak, v))
                y_h = _bdot(arb, u_h) + _bdot(ark, v)
                halves.append((u_h, y_h))
            u = jnp.where(lane_lo, halves[0][0], halves[1][0])
            y = _bdot_nt(rt, s0) + jnp.where(lane_lo, halves[0][1], halves[1][1])
            blockdiag = (_row_ids((LANES, LANES)) < HD) == (_lane_ids((LANES, LANES)) < HD)
            st_ref[...] = (s0 + jnp.where(blockdiag, _bdot_tn(u, bt) + _bdot_tn(v, kt), 0.0)) * p_last

            if d == 0:
                o_ref[rows, :] = y
            else:
                y = y + o_ref[rows, :]
                mean = _head_sum(y, lane_lo) * (1.0 / HD)
                yc = y - mean
                var = _head_sum(yc * yc, lane_lo) * (1.0 / HD)
                gn = yc * lax.rsqrt(var + RWKV_GN_EPS) * lnw_ref[...] + lnb_ref[...]
                k_0 = k * (1.0 + (a_gate(lo, 0) - 1.0) * ka_ref[...])
                bonus = _head_sum(r * (k_0 + k_d) * rk_ref[...], lane_lo) * v
                o_ref[rows, :] = (gn + bonus) * _silu(z_ref[rows, :])
            return carry

        lax.fori_loop(0, n_lat + n_ctx, step, 0)


def _rwkv(u, n_lat_rows, col_shift, col_z, rw_mu, rw_w0, rw_w2, rw_a0, rw_a2, rw_kk, rw_ka, rw_rk, rw_ln_w,
          rw_ln_b):
    b, t, _ = u.shape
    c = RW_CHUNK
    pairs = W // LANES
    col_lo = col_shift + 3 * W
    def pad_dirs(w2):
        z = jnp.zeros_like(w2[0])
        return jnp.stack([jnp.concatenate([w2[0], z], axis=0), jnp.concatenate([z, w2[1]], axis=0)])

    row = lambda a: a.reshape(1, -1).astype(F32)
    blk = lambda col: pl.BlockSpec((None, t, LANES), lambda bi, p, col=col: (bi, 0, col // LANES + p))
    vec = lambda off=0: pl.BlockSpec((1, LANES), lambda bi, p, off=off: (0, off // LANES + p))
    dirvec = pl.BlockSpec((2, 1, LANES), lambda bi, p: (0, 0, p))
    dirmat = pl.BlockSpec((2, 2 * LORA, LANES), lambda bi, p: (0, 0, p))
    mu = row(rw_mu)
    return pl.pallas_call(
        functools.partial(_rwkv_kernel, n_lat=n_lat_rows // c, n_ctx=(t - n_lat_rows) // c),
        grid=(b, pairs),
        in_specs=[
            blk(col_shift), blk(col_shift + W), blk(col_shift + 2 * W),
            pl.BlockSpec((None, t, 2 * LANES), lambda bi, p: (bi, 0, col_lo // (2 * LANES))),
            blk(col_z),
            vec(0), vec(W), vec(2 * W),
            pl.BlockSpec((1, 2 * LANES), lambda bi, p: (0, (3 * W) // (2 * LANES))),
            vec(), vec(), dirvec, dirvec, dirmat, dirmat, vec(), vec(), vec(),
        ],
        out_specs=pl.BlockSpec((None, t, LANES), lambda bi, p: (bi, 0, p)),
        out_shape=jax.ShapeDtypeStruct((b, t, W), F32),
        scratch_shapes=[pltpu.VMEM((LANES, LANES), F32)],
        compiler_params=_cparams("parallel", "parallel"),
        name="rwkv7",
    )(u, u, u, u, u, mu, mu, mu, mu, row(rw_kk), row(rw_ka), rw_w0.reshape(2, 1, W), rw_a0.reshape(2, 1, W),
      pad_dirs(rw_w2), pad_dirs(rw_a2), row(rw_rk), row(rw_ln_w), row(rw_ln_b))


def _out_proj_kernel(a_ref, b_ref, w_ref, h_ref, mod_ref, *rest, final):
    wa = a_ref.shape[-1]
    y = _bdot(a_ref[...], w_ref[:wa, :]) + _bdot(b_ref[...], w_ref[wa:, :])
    hn = h_ref[...] + mod_ref[2:3, :] * y
    if final:
        fnw_ref, o_ref = rest
        o_ref[...] = hn * lax.rsqrt(jnp.mean(hn * hn, axis=-1, keepdims=True) + NORM_EPS) * fnw_ref[...]
    else:
        (o_ref,) = rest
        o_ref[...] = hn


def _out_proj(a, b, w, h, mod, n_lat, final_norm_w=None):
    bsz, rows, wa = a.shape
    d = h.shape[-1]
    tm = ROW_TILE
    lat_tiles = n_lat // tm
    final = final_norm_w is not None
    in_specs = [
        pl.BlockSpec((None, tm, wa), lambda bi, i: (bi, i, 0)),
        pl.BlockSpec((None, tm, b.shape[-1]), lambda bi, i: (bi, i, 0)),
        pl.BlockSpec(w.shape, lambda bi, i: (0, 0)),
        pl.BlockSpec((None, tm, d), lambda bi, i: (bi, i, 0)),
        pl.BlockSpec((None, None, 3, d), lambda bi, i: (bi, (i >= lat_tiles).astype(jnp.int32), 0, 0)),
    ]
    args = [a, b, w, h, mod]
    if final:
        in_specs.append(pl.BlockSpec((1, d), lambda bi, i: (0, 0)))
        args.append(final_norm_w.reshape(1, d))
    return pl.pallas_call(
        functools.partial(_out_proj_kernel, final=final),
        grid=(bsz, rows // tm),
        in_specs=in_specs,
        out_specs=pl.BlockSpec((None, tm, d), lambda bi, i: (bi, i, 0)),
        out_shape=jax.ShapeDtypeStruct((bsz, rows, d), F32),
        compiler_params=_cparams("parallel", "parallel"),
        name="out_proj_final" if final else "out_proj",
    )(*args)


HY_FREQ_BLOCK = 256
HY_CH_TILE = 256


def _dft_matrices(l):
    fb = HY_FREQ_BLOCK
    n = 2 * l
    k = jnp.arange(l, dtype=jnp.int32)[:, None]
    split = 64
    a = jnp.arange(l // split, dtype=jnp.int32)[None, :] * split
    bb = jnp.arange(split, dtype=jnp.int32)[None, :]
    ang = lambda m: (jnp.pi / n) * (((2 * k + 1) * m) % (2 * n)).astype(F32)
    ca, sa = jnp.cos(ang(a))[:, :, None], jnp.sin(ang(a))[:, :, None]
    cb, sb = jnp.cos(ang(bb))[:, None, :], jnp.sin(ang(bb))[:, None, :]
    cos = (ca * cb - sa * sb).reshape(l, l)
    sin = (sa * cb + ca * sb).reshape(l, l)
    f = jnp.concatenate([cos.reshape(l // fb, fb, l), -sin.reshape(l // fb, fb, l)], axis=1)
    return f.astype(BF16), jnp.swapaxes(f, 1, 2).astype(BF16)


def _hy_filter_kernel(z_ref, w1_ref, b1_ref, f1_ref, w2_ref, b2_ref, f2_ref, w3_ref, dec_ref, o_ref):
    hid = jnp.sin(f1_ref[...] * (_dot_hi(z_ref[...], w1_ref[...]) + b1_ref[...]))
    hid = jnp.sin(f2_ref[...] * (_dot_hi(hid, w2_ref[...]) + b2_ref[...]))
    h = _dot_hi(hid, w3_ref[...]) * dec_ref[...]
    j = pl.program_id(0)
    row = lax.broadcasted_iota(jnp.int32, h.shape, 0)
    o_ref[...] = jnp.where(jnp.logical_and(j % 2 == 1, row == 0), 0.0, h).astype(o_ref.dtype)


def _hy_filters(l, w1, b1, f1, w2, b2, f2, w3):
    t = jnp.linspace(0.0, 1.0, l)[:, None]
    bands = jnp.linspace(1e-4, HY_BANDS - 1, HY_BANDS)
    ang = (2.0 * math.pi / l) * jnp.arange(l, dtype=F32)[:, None] * bands[None]
    z = jnp.concatenate([t, jnp.cos(ang), -jnp.sin(ang)], axis=-1)
    emb = z.shape[1]
    emb_pad = 64
    z = jnp.pad(z, ((0, 0), (0, emb_pad - emb)))
    w1p = jnp.pad(w1, ((0, emb_pad - emb), (0, 0)))
    deltas = jnp.abs(jnp.linspace(math.log(HY_TARGET) / HY_SLOW_PCT, math.log(HY_TARGET) / HY_FAST_PCT, W))
    dec = jnp.exp(-t * deltas)
    ffn = w1.shape[1]
    nblk = w3.shape[1] // W
    full = lambda shape: pl.BlockSpec(shape, lambda j: (0,) * len(shape))
    return pl.pallas_call(
        _hy_filter_kernel,
        grid=(nblk,),
        in_specs=[full((l, emb_pad)), full((emb_pad, ffn)), full((1, ffn)), full((1, ffn)), full((ffn, ffn)),
                  full((1, ffn)), full((1, ffn)), pl.BlockSpec((ffn, W), lambda j: (0, j)), full((l, W))],
        out_specs=pl.BlockSpec((l, W), lambda j: (0, j)),
        out_shape=jax.ShapeDtypeStruct((l, nblk * W), F32),
        compiler_params=_cparams("parallel"),
        name="hyena_filters",
    )(z, w1p, b1.reshape(1, -1), f1.reshape(1, -1), w2, b2.reshape(1, -1), f2.reshape(1, -1), w3, dec)


def _hy_spectrum_kernel(f_ref, h_ref, o_ref):
    o_ref[...] = _bdot(f_ref[...], h_ref[...])


def _hy_spectra(f_blocks, filt):
    nkb, fb2, l = f_blocks.shape
    cols = filt.shape[1]
    return pl.pallas_call(
        _hy_spectrum_kernel,
        grid=(nkb, cols // W),
        in_specs=[pl.BlockSpec((None, fb2, l), lambda kb, j: (kb, 0, 0)), pl.BlockSpec((l, W), lambda kb, j: (0, j))],
        out_specs=pl.BlockSpec((None, fb2, W), lambda kb, j: (kb, 0, j)),
        out_shape=jax.ShapeDtypeStruct((nkb, fb2, cols), F32),
        compiler_params=_cparams("parallel", "parallel"),
        name="hyena_spectra",
    )(f_blocks, filt)


def _hyena_kernel(v_ref, x1_ref, x2_ref, g_ref, cw_ref, cb_ref, bias_ref, f_ref, ft_ref, sf_ref, sb_ref,
                  o_ref, z_ref, zb_ref, acc_ref, *, l):
    o = pl.program_id(2)
    kb = pl.program_id(3)
    nkb = pl.num_programs(3)
    fb = HY_FREQ_BLOCK
    rc = ROW_TILE

    def conv_rows(ref, sec, c0):
        x, prev, nxt = _with_neighbours(ref, c0, rc, l, l)
        w = cw_ref[sec]
        return prev * w[0:1, :] + x * w[1:2, :] + nxt * w[2:3, :] + cb_ref[sec]

    @pl.when(jnp.logical_and(o == 0, kb == 0))
    def _():
        def body(i, carry):
            c0 = pl.multiple_of(i * rc, rc)
            zc = conv_rows(v_ref, 0, c0)
            z_ref[pl.ds(c0, rc), :] = zc
            zb_ref[pl.ds(c0, rc), :] = zc.astype(BF16)
            return carry

        lax.fori_loop(0, l // rc, body, 0)

    @pl.when(kb == 0)
    def _():
        acc_ref[...] = jnp.zeros_like(acc_ref)

    x = jnp.dot(f_ref[...], zb_ref[...], preferred_element_type=F32)
    xr, xi = x[:fb], x[fb:]
    sf, sb = sf_ref[...], sb_ref[...]
    gr = sf[:fb] + sb[:fb]
    gi = sf[fb:] - sb[fb:]
    y = jnp.concatenate([xr * gr - xi * gi, xr * gi + xi * gr], axis=0)
    acc_ref[...] += jnp.dot(ft_ref[...], y.astype(BF16), preferred_element_type=F32)

    @pl.when(kb == nkb - 1)
    def _():
        def body(i, carry):
            c0 = pl.multiple_of(i * rc, rc)
            rows = pl.ds(c0, rc)
            conv = acc_ref[rows, :] * (1.0 / l) + z_ref[rows, :] * bias_ref[o]

            @pl.when(o == 0)
            def _():
                zn = conv_rows(x1_ref, 1, c0) * conv
                z_ref[rows, :] = zn
                zb_ref[rows, :] = zn.astype(BF16)

            @pl.when(o == 1)
            def _():
                zn = conv_rows(x2_ref, 2, c0) * conv
                o_ref[rows, :] = zn * _silu(g_ref[rows, :])

            return carry

        lax.fori_loop(0, l // rc, body, 0)


def _hyena(u, l, conv_w, conv_b, hy_bias, f_blocks, ft_blocks, spectra):
    b = u.shape[0]
    ct = HY_CH_TILE
    nct = W // ct
    nkb, fb2, _ = f_blocks.shape
    sec = lambda s: pl.BlockSpec((None, l, ct), lambda bi, c, o, kb, s=s: (bi, 0, s * nct + c))
    cw = jnp.transpose(conv_w.reshape(3, 3, W), (1, 0, 2))
    spec = lambda direction: pl.BlockSpec(
        (None, fb2, ct), lambda bi, c, o, kb, direction=direction: (kb, 0, (2 * o + direction) * nct + c))
    return pl.pallas_call(
        functools.partial(_hyena_kernel, l=l),
        grid=(b, nct, 2, nkb),
        in_specs=[
            sec(0), sec(1), sec(2), sec(3),
            pl.BlockSpec((3, 3, ct), lambda bi, c, o, kb: (0, 0, c)),
            pl.BlockSpec((3, 1, ct), lambda bi, c, o, kb: (0, 0, c)),
            pl.BlockSpec((2, 1, ct), lambda bi, c, o, kb: (0, 0, c)),
            pl.BlockSpec((None, fb2, l), lambda bi, c, o, kb: (kb, 0, 0)),
            pl.BlockSpec((None, l, fb2), lambda bi, c, o, kb: (kb, 0, 0)),
            spec(0), spec(1),
        ],
        out_specs=pl.BlockSpec((None, l, ct), lambda bi, c, o, kb: (bi, 0, c)),
        out_shape=jax.ShapeDtypeStruct((b, l, W), F32),
        scratch_shapes=[pltpu.VMEM((l, ct), F32), pltpu.VMEM((l, ct), BF16), pltpu.VMEM((l, ct), F32)],
        compiler_params=_cparams("parallel", "parallel", "arbitrary", "arbitrary"),
        name="hyena",
    )(u, u, u, u, cw, conv_b.reshape(3, 1, W), hy_bias.reshape(2, 1, W), f_blocks, ft_blocks, spectra, spectra)


NA_Q_ROWS = 4
NA_K_ROWS = 12
NEG_BIG = -1e30


def _na_bias_tables(rpb, rows):
    nblk = rows // NA_Q_ROWS
    qi = jnp.arange(NA_Q_ROWS * GRID_W)
    ki = jnp.arange(NA_K_ROWS * GRID_W)
    tables = []
    for i in (0, 1, nblk - 1):
        ks = min(max(NA_Q_ROWS * (i - 1), 0), rows - NA_K_ROWS)
        rq = NA_Q_ROWS * i + qi // GRID_W
        cq = qi % GRID_W
        kr = ks + ki // GRID_W
        kc = ki % GRID_W
        r0 = jnp.clip(rq - NA_WIN_R // 2, 0, rows - NA_WIN_R)
        c0 = jnp.clip(cq - NA_WIN_C // 2, 0, GRID_W - NA_WIN_C)
        ok = ((kr[None, :] >= r0[:, None]) & (kr[None, :] < r0[:, None] + NA_WIN_R)
              & (kc[None, :] >= c0[:, None]) & (kc[None, :] < c0[:, None] + NA_WIN_C))
        dr = jnp.clip(kr[None, :] - rq[:, None] + NA_WIN_R - 1, 0, 2 * NA_WIN_R - 2)
        dc = jnp.clip(kc[None, :] - cq[:, None] + NA_WIN_C - 1, 0, 2 * NA_WIN_C - 2)
        tables.append(jnp.where(ok[None], rpb[:, dr, dc], NEG_BIG))
    return jnp.stack(tables, axis=1).astype(F32)


def _na_kernel(q_ref, k_ref, v_ref, z_ref, bias_ref, o_ref, *, l):
    rows = l // GRID_W
    nblk = rows // NA_Q_ROWS
    qn = NA_Q_ROWS * GRID_W
    kn = NA_K_ROWS * GRID_W
    n_ctx = k_ref.shape[0] - l
    lane_lo = _lane_ids((1, LANES)) < HD
    k_ctx = k_ref[pl.ds(l, n_ctx), :]
    v_ctx = v_ref[pl.ds(l, n_ctx), :]

    def body(i, carry):
        q0 = pl.multiple_of(i * qn, qn)
        ks = jnp.clip(NA_Q_ROWS * (i - 1), 0, rows - NA_K_ROWS)
        k0 = pl.multiple_of(ks * GRID_W, GRID_W)
        kind = jnp.where(i == 0, 0, jnp.where(i == nblk - 1, 2, 1))
        q = q_ref[pl.ds(q0, qn), :] * HD ** -0.5
        k_win = k_ref[pl.ds(k0, kn), :]
        v_win = v_ref[pl.ds(k0, kn), :]
        outs = []
        for hh in range(2):
            q_h = jnp.where(lane_lo if hh == 0 else jnp.logical_not(lane_lo), q, 0.0)
            s_win = _bdot_nt(q_h, k_win) + bias_ref[hh, kind]
            s_ctx = _bdot_nt(q_h, k_ctx)
            m = jnp.maximum(jnp.max(s_win, axis=-1, keepdims=True), jnp.max(s_ctx, axis=-1, keepdims=True))
            p_win = jnp.exp(s_win - m)
            p_ctx = jnp.exp(s_ctx - m)
            den = jnp.sum(p_win, axis=-1, keepdims=True) + jnp.sum(p_ctx, axis=-1, keepdims=True)
            outs.append((_bdot(p_win, v_win) + _bdot(p_ctx, v_ctx)) / den)
        rws = pl.ds(q0, qn)
        o_ref[rws, :] = jnp.where(lane_lo, outs[0], outs[1]) * _silu(z_ref[rws, :])
        return carry

    lax.fori_loop(0, nblk, body, 0)


def _natten(u, l, col_q, col_k, col_v, col_z, bias):
    b, t, _ = u.shape
    pairs = W // LANES
    lat = lambda col: pl.BlockSpec((None, l, LANES), lambda bi, p, col=col: (bi, 0, col // LANES + p))
    full = lambda col: pl.BlockSpec((None, t, LANES), lambda bi, p, col=col: (bi, 0, col // LANES + p))
    return pl.pallas_call(
        functools.partial(_na_kernel, l=l),
        grid=(b, pairs),
        in_specs=[lat(col_q), full(col_k), full(col_v), lat(col_z),
                  pl.BlockSpec((2,) + bias.shape[1:], lambda bi, p: (p, 0, 0, 0))],
        out_specs=pl.BlockSpec((None, l, LANES), lambda bi, p: (bi, 0, p)),
        out_shape=jax.ShapeDtypeStruct((b, l, W), F32),
        compiler_params=_cparams("parallel", "parallel"),
        name="natten",
    )(u, u, u, u, bias)


def kernel(x, c, ctx, c_ctx, ada_w, ada_b, norm_w, final_norm_w, even_w_in, even_w_out, ret_decay, rw_mu, rw_w0, rw_w2, rw_a0, rw_a2, rw_kk, rw_ka, rw_rk, rw_ln_w, rw_ln_b, odd_w_in, odd_w_out, hy_conv_w, hy_conv_b, hy_w1, hy_b1, hy_f1, hy_w2, hy_b2, hy_f2, hy_w3, hy_bias, na_rpb):
    b, l, d = x.shape
    n_ctx = ctx.shape[1]
    t = l + n_ctx
    depth = ada_w.shape[0]
    assert depth == 2 and l % ROW_TILE == 0 and n_ctx % ROW_TILE == 0 and (l // GRID_W) >= NA_K_ROWS

    cond_rows = -(-(b + 1) // 8) * 8
    cond = jnp.concatenate([c, c_ctx[None, :], jnp.zeros((cond_rows - b - 1, d), F32)], axis=0)
    mod_all = _ada_mod(cond, ada_w, ada_b)
    mods = [jnp.stack([mod_all[i, :b].reshape(b, 3, d),
                       jnp.broadcast_to(mod_all[i, b].reshape(1, 3, d), (b, 3, d))], axis=1) for i in range(depth)]

    h = jnp.concatenate([x, ctx], axis=1)

    col_rz, col_wz, col_shift = 3 * W, 4 * W, 5 * W
    u0 = _norm_proj(h, norm_w[0], mods[0], even_w_in[0].astype(BF16), l)
    cos_t, sin_t = _rope_tables(l, t)
    ret_o = _retention(u0, cos_t, sin_t, ret_decay[0], l, 0, W, 2 * W, col_rz)
    rw_o = _rwkv(u0, l, col_shift, col_wz, rw_mu[0], rw_w0[0], rw_w2[0], rw_a0[0], rw_a2[0], rw_kk[0], rw_ka[0],
                 rw_rk[0], rw_ln_w[0], rw_ln_b[0])
    h = _out_proj(ret_o, rw_o, even_w_out[0].astype(BF16), h, mods[0], l)

    u1 = _norm_proj(h, norm_w[1], mods[1], odd_w_in[0].astype(BF16), l)
    f_blocks, ft_blocks = _dft_matrices(l)
    filt = _hy_filters(l, hy_w1[0], hy_b1[0], hy_f1[0], hy_w2[0], hy_b2[0], hy_f2[0], hy_w3[0])
    spectra = _hy_spectra(f_blocks, filt)
    hy_o = _hyena(u1, l, hy_conv_w[0], hy_conv_b[0], hy_bias[0], f_blocks, ft_blocks, spectra)
    na_o = _natten(u1, l, 4 * W, 5 * W, 6 * W, 7 * W, _na_bias_tables(na_rpb[0], l // GRID_W))
    return _out_proj(hy_o, na_o, odd_w_out[0].astype(BF16), h, mods[1], l, final_norm_w)
```

```python
import functools
import math

import jax
import jax.numpy as jnp
from jax import lax
from jax.experimental import pallas as pl
from jax.experimental.pallas import tpu as pltpu

F32 = jnp.float32
BF16 = jnp.bfloat16

GRID_W = 64
NORM_EPS = 1e-6
ROPE_BASE = 10000.0
HEADS = 8
HD = 64
W = HEADS * HD
LORA = 64
RWKV_GN_EPS = 64e-5
HY_BANDS = 16
HY_TARGET = 1e-2
HY_FAST_PCT = 0.3
HY_SLOW_PCT = 1.5
NA_WIN_R = 8
NA_WIN_C = 16
LANES = 128
PAIRS = W // LANES
RET_CHUNK = 128
RW_CHUNK = 64
RW_SUB = 32
ROW_TILE = 256
HALO = 16
VMEM_LIMIT = 56 * 1024 * 1024


def _cparams(*sem):
    return pltpu.CompilerParams(dimension_semantics=sem, vmem_limit_bytes=VMEM_LIMIT)


def _bdot(a, b):
    return jnp.dot(a.astype(BF16), b.astype(BF16), preferred_element_type=F32)


def _bdot_nt(a, b):
    return lax.dot_general(a.astype(BF16), b.astype(BF16), (((1,), (1,)), ((), ())), preferred_element_type=F32)


def _bdot_tn(a, b):
    return lax.dot_general(a.astype(BF16), b.astype(BF16), (((0,), (0,)), ((), ())), preferred_element_type=F32)


def _split3(x):
    hi = x.astype(BF16)
    r1 = x - hi.astype(F32)
    mid = r1.astype(BF16)
    lo = (r1 - mid.astype(F32)).astype(BF16)
    return hi, mid, lo


def _dot_exact_lhs(m, x):
    mb = m.astype(BF16)
    hi, mid, lo = _split3(x)
    d = lambda p: jnp.dot(mb, p, preferred_element_type=F32)
    return d(hi) + d(mid) + d(lo)


def _dot_hi(a, b):
    ah = a.astype(BF16)
    al = (a - ah.astype(F32)).astype(BF16)
    bh = b.astype(BF16)
    bl = (b - bh.astype(F32)).astype(BF16)
    d = lambda p, q: jnp.dot(p, q, preferred_element_type=F32)
    return d(ah, bh) + d(ah, bl) + d(al, bh)


def _silu(x):
    return x * (1.0 / (1.0 + jnp.exp(-x)))


def _sigmoid(x):
    return 1.0 / (1.0 + jnp.exp(-x))


def _softplus(x):
    return jnp.maximum(x, 0.0) + jnp.log(1.0 + jnp.exp(-jnp.abs(x)))


def _lane_ids(shape):
    return lax.broadcasted_iota(jnp.int32, shape, len(shape) - 1)


def _row_ids(shape):
    return lax.broadcasted_iota(jnp.int32, shape, len(shape) - 2)


def _head_sum(x, lane_lo):
    sa = jnp.sum(jnp.where(lane_lo, x, 0.0), axis=-1, keepdims=True)
    sb = jnp.sum(jnp.where(lane_lo, 0.0, x), axis=-1, keepdims=True)
    return jnp.where(lane_lo, sa, sb)


def _head_sum_mxu(x, ones_bd):
    hi = x.astype(BF16)
    lo = (x - hi.astype(F32)).astype(BF16)
    tiles = []
    for p in range(x.shape[-1] // LANES):
        sl = slice(p * LANES, (p + 1) * LANES)
        tiles.append(jnp.dot(hi[:, sl], ones_bd, preferred_element_type=F32)
                     + jnp.dot(lo[:, sl], ones_bd, preferred_element_type=F32))
    return tiles[0] if len(tiles) == 1 else jnp.concatenate(tiles, axis=-1)


def _with_neighbours(ref, cols, c0, c, n_rows, lat_rows):
    row = lax.broadcasted_iota(jnp.int32, (c, 1), 0)
    x = ref[pl.ds(c0, c), cols].astype(F32)
    p0 = pl.multiple_of(jnp.maximum(c0 - HALO, 0), HALO)
    n0 = pl.multiple_of(jnp.minimum(c0 + c, n_rows - HALO), HALO)
    prev_row = ref[pl.ds(p0, HALO), cols].astype(F32)[HALO - 1:HALO, :]
    next_row = ref[pl.ds(n0, HALO), cols].astype(F32)[0:1, :]
    has_prev = jnp.logical_and(c0 != 0, c0 != lat_rows)
    has_next = jnp.logical_and(c0 + c != lat_rows, c0 + c != n_rows)
    prev_row = jnp.where(has_prev, prev_row, 0.0)
    next_row = jnp.where(has_next, next_row, 0.0)
    prev = jnp.where(row == 0, prev_row, pltpu.roll(x, 1, 0))
    nxt = jnp.where(row == c - 1, next_row, pltpu.roll(x, c - 1, 0))
    return x, prev, nxt


def _scan_chunk_id(i, n_lat, n_ctx, reverse):
    if reverse:
        return jnp.where(i < n_ctx, n_lat + n_ctx - 1 - i, n_lat - 1 - (i - n_ctx))
    return jnp.where(i < n_ctx, n_lat + i, i - n_ctx)


def _ada_kernel(c_ref, w_ref, b_ref, o_ref):
    o_ref[...] = _dot_hi(_silu(c_ref[...]), w_ref[...]) + b_ref[...]


def _ada_mod(cond, ada_w, ada_b):
    depth, d, d3 = ada_w.shape
    rows = cond.shape[0]
    return pl.pallas_call(
        _ada_kernel,
        grid=(depth, d3 // d),
        in_specs=[
            pl.BlockSpec((rows, d), lambda i, j: (0, 0)),
            pl.BlockSpec((None, d, d), lambda i, j: (i, 0, j)),
            pl.BlockSpec((None, 1, d), lambda i, j: (i, 0, j)),
        ],
        out_specs=pl.BlockSpec((None, rows, d), lambda i, j: (i, 0, j)),
        out_shape=jax.ShapeDtypeStruct((depth, rows, d3), F32),
        compiler_params=_cparams("parallel", "parallel"),
        name="ada_mod",
    )(cond, ada_w, ada_b.reshape(depth, 1, d3))


def _norm_proj_kernel(h_ref, nw_ref, mod_ref, w_ref, o_ref):
    x = h_ref[...]
    y = x * lax.rsqrt(jnp.mean(x * x, axis=-1, keepdims=True) + NORM_EPS) * nw_ref[...]
    n = y * (1.0 + mod_ref[1:2, :]) + mod_ref[0:1, :]
    o_ref[...] = _bdot(n, w_ref[...]).astype(o_ref.dtype)


def _norm_proj(h, norm_w, mod, w, n_lat):
    b, t, d = h.shape
    n = w.shape[1]
    tm = ROW_TILE
    lat_tiles = n_lat // tm
    return pl.pallas_call(
        _norm_proj_kernel,
        grid=(b, t // tm),
        in_specs=[
            pl.BlockSpec((None, tm, d), lambda bi, i: (bi, i, 0)),
            pl.BlockSpec((1, d), lambda bi, i: (0, 0)),
            pl.BlockSpec((None, None, 3, d), lambda bi, i: (bi, (i >= lat_tiles).astype(jnp.int32), 0, 0)),
            pl.BlockSpec((d, n), lambda bi, i: (0, 0)),
        ],
        out_specs=pl.BlockSpec((None, tm, n), lambda bi, i: (bi, i, 0)),
        out_shape=jax.ShapeDtypeStruct((b, t, n), BF16),
        compiler_params=_cparams("parallel", "parallel"),
        name="norm_proj",
    )(h, norm_w.reshape(1, d), mod, w)


def _ret_kernel(q_ref, k_ref, v_ref, z_ref, cos_ref, sin_ref, rd_ref, o_ref, yf_ref, st_ref, *, n_lat, n_ctx):
    c = RET_CHUNK
    lane = _lane_ids((1, LANES))
    lane_lo = lane < HD
    rope_lo = (lane % HD) < (HD // 2)
    pos = lax.broadcasted_iota(jnp.int32, (c, 1), 0).astype(F32)
    ti = lax.broadcasted_iota(jnp.int32, (c, c), 0)
    si = lax.broadcasted_iota(jnp.int32, (c, c), 1)
    blockdiag = (_row_ids((LANES, LANES)) < HD) == (_lane_ids((LANES, LANES)) < HD)

    def rope(x, rows):
        partner = jnp.where(rope_lo, pltpu.roll(x, LANES - HD // 2, 1), pltpu.roll(x, HD // 2, 1))
        return x * cos_ref[rows, :] + partner * sin_ref[rows, :]

    for d in range(2):
        lg_a = -jnp.exp(rd_ref[d, 0])
        lg_b = -jnp.exp(rd_ref[d, 1])
        lgv = jnp.where(lane_lo, lg_a, lg_b)
        if d == 0:
            diff = (ti - si).astype(F32)
            q_dec = jnp.exp((pos + 1.0) * lgv)
            k_dec = jnp.exp((c - 1.0 - pos) * lgv)
        else:
            diff = (si - ti).astype(F32)
            q_dec = jnp.exp((c - pos) * lgv)
            k_dec = jnp.exp(pos * lgv)
        keep = diff >= 0
        dmat_a = jnp.where(keep, jnp.exp(jnp.where(keep, diff, 0.0) * lg_a), 0.0)
        dmat_b = jnp.where(keep, jnp.exp(jnp.where(keep, diff, 0.0) * lg_b), 0.0)
        g_chunk = jnp.exp(c * lgv)
        st_ref[...] = jnp.zeros_like(st_ref)

        def step(i, carry, d=d, q_dec=q_dec, k_dec=k_dec, dmat_a=dmat_a, dmat_b=dmat_b, g_chunk=g_chunk):
            cid = _scan_chunk_id(i, n_lat, n_ctx, d == 1)
            rows = pl.ds(pl.multiple_of(cid * c, c), c)
            q = rope(q_ref[rows, :].astype(F32), rows)
            k = rope(k_ref[rows, :].astype(F32) * HD ** -0.5, rows)
            v = v_ref[rows, :]
            s_a = _bdot_nt(jnp.where(lane_lo, q, 0.0), k) * dmat_a
            s_b = _bdot_nt(jnp.where(lane_lo, 0.0, q), k) * dmat_b
            y = jnp.where(lane_lo, _bdot(s_a, v), _bdot(s_b, v)) + _bdot(q * q_dec, st_ref[...])
            st_ref[...] = st_ref[...] * g_chunk + jnp.where(blockdiag, _bdot_tn(k * k_dec, v), 0.0)
            if d == 0:
                yf_ref[rows, :] = y
            else:
                y = y + yf_ref[rows, :]
                ms = _head_sum(y * y, lane_lo) * (1.0 / HD)
                gate = _silu(z_ref[rows, :].astype(F32))
                o_ref[rows, :] = (y * lax.rsqrt(ms + NORM_EPS) * gate).astype(o_ref.dtype)
            return carry

        lax.fori_loop(0, n_lat + n_ctx, step, 0)


def _retention(u, cos_t, sin_t, ret_decay, n_lat_rows, col_q, col_k, col_v, col_z):
    b, t, _ = u.shape
    c = RET_CHUNK
    rd = jnp.broadcast_to(ret_decay.astype(F32)[:, :, None, None], (2, HEADS, 1, LANES))
    blk = lambda col: pl.BlockSpec((None, t, LANES), lambda bi, p, col=col: (bi, 0, col // LANES + p))
    return pl.pallas_call(
        functools.partial(_ret_kernel, n_lat=n_lat_rows // c, n_ctx=(t - n_lat_rows) // c),
        grid=(b, PAIRS),
        in_specs=[
            blk(col_q), blk(col_k), blk(col_v), blk(col_z),
            pl.BlockSpec((t, LANES), lambda bi, p: (0, 0)),
            pl.BlockSpec((t, LANES), lambda bi, p: (0, 0)),
            pl.BlockSpec((2, 2, 1, LANES), lambda bi, p: (0, p, 0, 0)),
        ],
        out_specs=pl.BlockSpec((None, t, LANES), lambda bi, p: (bi, 0, p)),
        out_shape=jax.ShapeDtypeStruct((b, t, W), BF16),
        scratch_shapes=[pltpu.VMEM((t, LANES), F32), pltpu.VMEM((LANES, LANES), F32)],
        compiler_params=_cparams("parallel", "parallel"),
        name="retention",
    )(u, u, u, u, cos_t, sin_t, rd)


def _rope_tables(n_lat_rows, t):
    pos = jnp.arange(n_lat_rows)
    row = (pos // GRID_W).astype(F32)
    col = (pos % GRID_W).astype(F32)
    nf = HD // 4
    inv = ROPE_BASE ** (-jnp.arange(nf, dtype=F32) / nf)
    ang = jnp.concatenate([row[:, None] * inv, col[:, None] * inv], axis=-1)
    cos, sin = jnp.cos(ang), jnp.sin(ang)
    cos_h = jnp.concatenate([cos, cos], axis=-1)
    sin_h = jnp.concatenate([-sin, sin], axis=-1)
    reps = LANES // HD
    cos_t = jnp.concatenate([jnp.tile(cos_h, (1, reps)), jnp.ones((t - n_lat_rows, LANES), F32)], axis=0)
    sin_t = jnp.concatenate([jnp.tile(sin_h, (1, reps)), jnp.zeros((t - n_lat_rows, LANES), F32)], axis=0)
    return cos_t, sin_t


def _rwkv_kernel(r_ref, k_ref, v_ref, lo_ref, z_ref, mu_ref, mulo_ref, kkw_ref, ka_ref, w0_ref, a0_ref, w2_ref,
                 a2_ref, rk_ref, lnw_ref, lnb_ref, o_ref, y_acc, b_acc, st_ref, *, n_lat, n_ctx):
    c = RW_CHUNK
    c2 = 2 * c
    t_rows = (n_lat + n_ctx) * c
    lat_rows = n_lat * c
    lane_lo = _lane_ids((1, LANES)) < HD
    ri = _row_ids((c2, c2))
    ci = _lane_ids((c2, c2))
    same_head = (ri < c) == (ci < c)
    rt_, ct_ = ri % c, ci % c
    same_sub = (rt_ // RW_SUB) == (ct_ // RW_SUB)
    eye = (ri == ci).astype(F32)
    ones_bd = same_head.astype(BF16)
    t64 = lax.broadcasted_iota(jnp.int32, (c, c), 0)
    s64 = lax.broadcasted_iota(jnp.int32, (c, c), 1)
    masks = []
    for d in range(2):
        before = (ct_ < rt_) if d == 0 else (ct_ > rt_)
        strict = jnp.logical_and(same_head, before)
        incl = jnp.logical_and(same_head, jnp.logical_or(before, ct_ == rt_))
        tri = ((s64 <= t64) if d == 0 else (s64 >= t64)).astype(BF16)
        masks.append((strict, incl, jnp.logical_and(strict, same_sub), tri))

    y_acc[...] = jnp.zeros_like(y_acc)
    b_acc[...] = jnp.zeros_like(b_acc)
    st_ref[...] = jnp.zeros_like(st_ref)

    def stack2(x):
        return jnp.concatenate([jnp.where(lane_lo, x, 0.0), jnp.where(lane_lo, 0.0, x)], axis=0)

    def dup(x):
        return jnp.concatenate([x, x], axis=0)

    def unstack(x):
        return jnp.where(lane_lo, x[:c], x[c:])

    def mixed(ref, cols, mu, c0):
        x, prev, nxt = _with_neighbours(ref, cols, c0, c, t_rows, lat_rows)
        return x + mu * (0.5 * (prev + nxt) - x)

    def prepare(d, cid):
        tri = masks[d][3]
        c0 = pl.multiple_of(cid * c, c)
        rows = pl.ds(c0, c)
        every = slice(None)
        r = mixed(r_ref, every, mu_ref[0], c0)
        k = mixed(k_ref, every, mu_ref[1], c0)
        v = mixed(v_ref, every, mu_ref[2], c0)
        lo = mixed(lo_ref, every, mulo_ref[...], c0)
        kk = k * kkw_ref[...]
        kk = kk / jnp.maximum(jnp.sqrt(_head_sum_mxu(kk * kk, ones_bd)), 1e-12)
        w_log = -_softplus(-(w0_ref[d] + _bdot(jnp.tanh(lo[:, :LANES]), w2_ref[d]))) - 0.5
        logw = -jnp.exp(w_log)
        a = _sigmoid(a0_ref[d] + _bdot(lo[:, LANES:], a2_ref[d]))
        k_d = k * (1.0 + (a - 1.0) * ka_ref[...])
        b_acc[rows, :] += _head_sum_mxu(r * k_d * rk_ref[...], ones_bd) * v

        cum = _dot_exact_lhs(tri, logw)
        p = jnp.exp(cum)
        inv_p = jnp.exp(-cum)
        at_w = -kk * jnp.exp(cum - logw)
        bt_w = kk * a * inv_p
        kt_w = k_d * inv_p
        rt_w = r * p
        p_last_w = p[c - 1:c, :] if d == 0 else p[0:1, :]
        return dict(rows=rows, at=at_w, bt=bt_w, kt=kt_w, rt=rt_w, v=v, p_last=p_last_w)

    def step(i, carry):
        prep = [prepare(0, _scan_chunk_id(i, n_lat, n_ctx, False)), prepare(1, _scan_chunk_id(i, n_lat, n_ctx, True))]
        inst = [(d, pr) for d in range(2) for pr in range(PAIRS)]
        each = lambda fn: [fn(j) for j in range(len(inst))]
        sl = lambda j: slice(inst[j][1] * LANES, (inst[j][1] + 1) * LANES)
        op = lambda j, name: prep[inst[j][0]][name][:, sl(j)]
        msk = lambda j: masks[inst[j][0]]

        big = each(lambda j: _bdot_nt(jnp.concatenate([stack2(op(j, "at")), stack2(op(j, "rt"))], axis=0),
                                      jnp.concatenate([dup(op(j, "bt")), dup(op(j, "kt"))], axis=0)))
        s0 = each(lambda j: st_ref[inst[j][0], inst[j][1]])
        proj = each(lambda j: _bdot_nt(jnp.concatenate([op(j, "at"), op(j, "rt")], axis=0), s0[j]))
        v2 = each(lambda j: dup(op(j, "v")))
        n_all = each(lambda j: jnp.where(msk(j)[0], big[j][:c2, :c2], 0.0))
        aak_v = each(lambda j: _bdot(jnp.where(msk(j)[0], big[j][:c2, c2:], 0.0), v2[j]))
        n_sub = each(lambda j: jnp.where(msk(j)[2], n_all[j], 0.0))
        t_sub = each(lambda j: eye + n_sub[j])
        npow = n_sub
        span = 1
        while 2 * span < RW_SUB:
            npow = each(lambda j: _bdot(npow[j], npow[j]))
            t_sub = each(lambda j: t_sub[j] + _bdot(t_sub[j], npow[j]))
            span *= 2
        corr = each(lambda j: _bdot(t_sub[j], n_all[j] - n_sub[j]))
        t_inv = each(lambda j: t_sub[j] + _bdot(corr[j], t_sub[j]))
        u_st = each(lambda j: _bdot(t_inv[j], dup(proj[j][:c]) + aak_v[j]))
        y_st = each(lambda j: _bdot(
            jnp.concatenate([jnp.where(msk(j)[1], big[j][c2:, :c2], 0.0), jnp.where(msk(j)[1], big[j][c2:, c2:], 0.0)],
                            axis=1),
            jnp.concatenate([u_st[j], v2[j]], axis=0)))
        upd = each(lambda j: _bdot_tn(jnp.concatenate([unstack(u_st[j]), op(j, "v")], axis=0),
                                      jnp.concatenate([op(j, "bt"), op(j, "kt")], axis=0)))
        for j, (d, pr) in enumerate(inst):
            st_ref[d, pr] = (s0[j] + jnp.where(same_head, upd[j], 0.0)) * prep[d]["p_last"][:, sl(j)]
            y_acc[prep[d]["rows"], sl(j)] += proj[j][c:] + unstack(y_st[j])
        return carry

    lax.fori_loop(0, n_lat + n_ctx, step, 0)

    def finish(i, carry):
        rows = pl.ds(pl.multiple_of(i * ROW_TILE, ROW_TILE), ROW_TILE)
        y = y_acc[rows, :]
        mean = _head_sum_mxu(y, ones_bd) * (1.0 / HD)
        yc = y - mean
        var = _head_sum_mxu(yc * yc, ones_bd) * (1.0 / HD)
        gn = yc * lax.rsqrt(var + RWKV_GN_EPS) * lnw_ref[...] + lnb_ref[...]
        o_ref[rows, :] = ((gn + b_acc[rows, :]) * _silu(z_ref[rows, :].astype(F32))).astype(o_ref.dtype)
        return carry

    lax.fori_loop(0, t_rows // ROW_TILE, finish, 0)


def _rwkv(u, n_lat_rows, col_shift, col_z, rw_mu, rw_w0, rw_w2, rw_a0, rw_a2, rw_kk, rw_ka, rw_rk, rw_ln_w,
          rw_ln_b):
    b, t, _ = u.shape
    c = RW_CHUNK
    assert c == HD
    col_lo = col_shift + 3 * W
    def pad_dirs(w2):
        z = jnp.zeros_like(w2[0])
        return jnp.stack([jnp.concatenate([w2[0], z], axis=0), jnp.concatenate([z, w2[1]], axis=0)]).astype(BF16)

    row = lambda a: a.reshape(1, W).astype(F32)
    blk = lambda col, width=W: pl.BlockSpec((None, t, width), lambda bi, col=col, width=width: (bi, 0, col // width))
    full = lambda shape: pl.BlockSpec(shape, lambda bi: (0,) * len(shape))
    return pl.pallas_call(
        functools.partial(_rwkv_kernel, n_lat=n_lat_rows // c, n_ctx=(t - n_lat_rows) // c),
        grid=(b,),
        in_specs=[
            blk(col_shift), blk(col_shift + W), blk(col_shift + 2 * W), blk(col_lo, 2 * LANES), blk(col_z),
            full((3, 1, W)), full((1, 2 * LANES)), full((1, W)), full((1, W)), full((2, 1, W)), full((2, 1, W)),
            full((2, 2 * LORA, W)), full((2, 2 * LORA, W)), full((1, W)), full((1, W)), full((1, W)),
        ],
        out_specs=pl.BlockSpec((None, t, W), lambda bi: (bi, 0, 0)),
        out_shape=jax.ShapeDtypeStruct((b, t, W), BF16),
        scratch_shapes=[pltpu.VMEM((t, W), F32), pltpu.VMEM((t, W), F32),
                        pltpu.VMEM((2, PAIRS, LANES, LANES), F32)],
        compiler_params=_cparams("parallel"),
        name="rwkv7",
    )(u, u, u, u, u, rw_mu[:3 * W].reshape(3, 1, W), rw_mu[3 * W:].reshape(1, 2 * LANES), row(rw_kk), row(rw_ka),
      rw_w0.reshape(2, 1, W), rw_a0.reshape(2, 1, W), pad_dirs(rw_w2), pad_dirs(rw_a2), row(rw_rk), row(rw_ln_w),
      row(rw_ln_b))


def _out_proj_kernel(a_ref, b_ref, w_ref, h_ref, mod_ref, *rest, final):
    wa = a_ref.shape[-1]
    y = _bdot(a_ref[...], w_ref[:wa, :]) + _bdot(b_ref[...], w_ref[wa:, :])
    hn = h_ref[...] + mod_ref[2:3, :] * y
    if final:
        fnw_ref, o_ref = rest
        o_ref[...] = hn * lax.rsqrt(jnp.mean(hn * hn, axis=-1, keepdims=True) + NORM_EPS) * fnw_ref[...]
    else:
        (o_ref,) = rest
        o_ref[...] = hn


def _out_proj(a, b, w, h, mod, n_lat, final_norm_w=None):
    bsz, rows, wa = a.shape
    d = h.shape[-1]
    tm = ROW_TILE
    lat_tiles = n_lat // tm
    final = final_norm_w is not None
    in_specs = [
        pl.BlockSpec((None, tm, wa), lambda bi, i: (bi, i, 0)),
        pl.BlockSpec((None, tm, b.shape[-1]), lambda bi, i: (bi, i, 0)),
        pl.BlockSpec(w.shape, lambda bi, i: (0, 0)),
        pl.BlockSpec((None, tm, d), lambda bi, i: (bi, i, 0)),
        pl.BlockSpec((None, None, 3, d), lambda bi, i: (bi, (i >= lat_tiles).astype(jnp.int32), 0, 0)),
    ]
    args = [a, b, w, h, mod]
    if final:
        in_specs.append(pl.BlockSpec((1, d), lambda bi, i: (0, 0)))
        args.append(final_norm_w.reshape(1, d))
    return pl.pallas_call(
        functools.partial(_out_proj_kernel, final=final),
        grid=(bsz, rows // tm),
        in_specs=in_specs,
        out_specs=pl.BlockSpec((None, tm, d), lambda bi, i: (bi, i, 0)),
        out_shape=jax.ShapeDtypeStruct((bsz, rows, d), F32),
        compiler_params=_cparams("parallel", "parallel"),
        name="out_proj_final" if final else "out_proj",
    )(*args)


HY_FREQ_BLOCK = 256


def _dft_matrices(l):
    fb = HY_FREQ_BLOCK
    n = 2 * l
    k = jnp.arange(l, dtype=jnp.int32)[:, None]
    split = 64
    a = jnp.arange(l // split, dtype=jnp.int32)[None, :] * split
    bb = jnp.arange(split, dtype=jnp.int32)[None, :]
    ang = lambda m: (jnp.pi / n) * (((2 * k + 1) * m) % (2 * n)).astype(F32)
    ca, sa = jnp.cos(ang(a))[:, :, None], jnp.sin(ang(a))[:, :, None]
    cb, sb = jnp.cos(ang(bb))[:, None, :], jnp.sin(ang(bb))[:, None, :]
    cos = (ca * cb - sa * sb).reshape(l, l)
    sin = (sa * cb + ca * sb).reshape(l, l)
    f = jnp.concatenate([cos.reshape(l // fb, fb, l), -sin.reshape(l // fb, fb, l)], axis=1)
    return f.astype(BF16), jnp.swapaxes(f, 1, 2).astype(BF16)


def _hy_filter_kernel(z_ref, w1_ref, b1_ref, f1_ref, w2_ref, b2_ref, f2_ref, w3_ref, dec_ref, o_ref):
    hid = jnp.sin(f1_ref[...] * (_dot_hi(z_ref[...], w1_ref[...]) + b1_ref[...]))
    hid = jnp.sin(f2_ref[...] * (_dot_hi(hid, w2_ref[...]) + b2_ref[...]))
    h = _dot_hi(hid, w3_ref[...]) * dec_ref[...]
    j = pl.program_id(0)
    row = lax.broadcasted_iota(jnp.int32, h.shape, 0)
    o_ref[...] = jnp.where(jnp.logical_and(j % 2 == 1, row == 0), 0.0, h).astype(o_ref.dtype)


def _hy_filters(l, w1, b1, f1, w2, b2, f2, w3):
    t = jnp.linspace(0.0, 1.0, l)[:, None]
    bands = jnp.linspace(1e-4, HY_BANDS - 1, HY_BANDS)
    ang = (2.0 * math.pi / l) * jnp.arange(l, dtype=F32)[:, None] * bands[None]
    z = jnp.concatenate([t, jnp.cos(ang), -jnp.sin(ang)], axis=-1)
    emb = z.shape[1]
    emb_pad = 64
    z = jnp.pad(z, ((0, 0), (0, emb_pad - emb)))
    w1p = jnp.pad(w1, ((0, emb_pad - emb), (0, 0)))
    deltas = jnp.abs(jnp.linspace(math.log(HY_TARGET) / HY_SLOW_PCT, math.log(HY_TARGET) / HY_FAST_PCT, W))
    dec = jnp.exp(-t * deltas)
    ffn = w1.shape[1]
    nblk = w3.shape[1] // W
    full = lambda shape: pl.BlockSpec(shape, lambda j: (0,) * len(shape))
    return pl.pallas_call(
        _hy_filter_kernel,
        grid=(nblk,),
        in_specs=[full((l, emb_pad)), full((emb_pad, ffn)), full((1, ffn)), full((1, ffn)), full((ffn, ffn)),
                  full((1, ffn)), full((1, ffn)), pl.BlockSpec((ffn, W), lambda j: (0, j)), full((l, W))],
        out_specs=pl.BlockSpec((l, W), lambda j: (0, j)),
        out_shape=jax.ShapeDtypeStruct((l, nblk * W), F32),
        compiler_params=_cparams("parallel"),
        name="hyena_filters",
    )(z, w1p, b1.reshape(1, -1), f1.reshape(1, -1), w2, b2.reshape(1, -1), f2.reshape(1, -1), w3, dec)


def _hy_spectrum_kernel(f_ref, h_ref, o_ref):
    o_ref[...] = _bdot(f_ref[...], h_ref[...])


def _hy_spectra(f_blocks, filt):
    nkb, fb2, l = f_blocks.shape
    cols = filt.shape[1]
    return pl.pallas_call(
        _hy_spectrum_kernel,
        grid=(nkb, cols // W),
        in_specs=[pl.BlockSpec((None, fb2, l), lambda kb, j: (kb, 0, 0)), pl.BlockSpec((l, W), lambda kb, j: (0, j))],
        out_specs=pl.BlockSpec((None, fb2, W), lambda kb, j: (kb, 0, j)),
        out_shape=jax.ShapeDtypeStruct((nkb, fb2, cols), F32),
        compiler_params=_cparams("parallel", "parallel"),
        name="hyena_spectra",
    )(f_blocks, filt)


def _hyena_kernel(v_ref, x1_ref, x2_ref, g_ref, cw_ref, cb_ref, bias_ref, f_ref, ft_ref, sf_ref, sb_ref,
                  o_ref, z_ref, zb_ref, acc_ref, *, l):
    o = pl.program_id(1)
    kb = pl.program_id(2)
    nkb = pl.num_programs(2)
    fb = HY_FREQ_BLOCK
    rc = ROW_TILE

    def conv_rows(ref, sec, c0):
        x, prev, nxt = _with_neighbours(ref, slice(None), c0, rc, l, l)
        w = cw_ref[sec]
        return prev * w[0:1, :] + x * w[1:2, :] + nxt * w[2:3, :] + cb_ref[sec]

    @pl.when(jnp.logical_and(o == 0, kb == 0))
    def _():
        def body(i, carry):
            c0 = pl.multiple_of(i * rc, rc)
            zc = conv_rows(v_ref, 0, c0)
            z_ref[pl.ds(c0, rc), :] = zc
            zb_ref[pl.ds(c0, rc), :] = zc.astype(BF16)
            return carry

        lax.fori_loop(0, l // rc, body, 0)

    @pl.when(kb == 0)
    def _():
        acc_ref[...] = jnp.zeros_like(acc_ref)

    x = jnp.dot(f_ref[...], zb_ref[...], preferred_element_type=F32)
    xr, xi = x[:fb], x[fb:]
    sf, sb = sf_ref[...], sb_ref[...]
    gr = sf[:fb] + sb[:fb]
    gi = sf[fb:] - sb[fb:]
    y = jnp.concatenate([xr * gr - xi * gi, xr * gi + xi * gr], axis=0)
    acc_ref[...] += jnp.dot(ft_ref[...], y.astype(BF16), preferred_element_type=F32)

    @pl.when(kb == nkb - 1)
    def _():
        def body(i, carry):
            c0 = pl.multiple_of(i * rc, rc)
            rows = pl.ds(c0, rc)
            conv = acc_ref[rows, :] * (1.0 / l) + z_ref[rows, :] * bias_ref[o]

            @pl.when(o == 0)
            def _():
                zn = conv_rows(x1_ref, 1, c0) * conv
                z_ref[rows, :] = zn
                zb_ref[rows, :] = zn.astype(BF16)

            @pl.when(o == 1)
            def _():
                zn = conv_rows(x2_ref, 2, c0) * conv
                o_ref[rows, :] = (zn * _silu(g_ref[rows, :].astype(F32))).astype(o_ref.dtype)

            return carry

        lax.fori_loop(0, l // rc, body, 0)


def _hyena(u, l, conv_w, conv_b, hy_bias, f_blocks, ft_blocks, spectra):
    b = u.shape[0]
    nkb, fb2, _ = f_blocks.shape
    sec = lambda s: pl.BlockSpec((None, l, W), lambda bi, o, kb, s=s: (bi, 0, s))
    cw = jnp.transpose(conv_w.reshape(3, 3, W), (1, 0, 2))
    spec = lambda direction: pl.BlockSpec(
        (None, fb2, W), lambda bi, o, kb, direction=direction: (kb, 0, 2 * o + direction))
    return pl.pallas_call(
        functools.partial(_hyena_kernel, l=l),
        grid=(b, 2, nkb),
        in_specs=[
            sec(0), sec(1), sec(2), sec(3),
            pl.BlockSpec((3, 3, W), lambda bi, o, kb: (0, 0, 0)),
            pl.BlockSpec((3, 1, W), lambda bi, o, kb: (0, 0, 0)),
            pl.BlockSpec((2, 1, W), lambda bi, o, kb: (0, 0, 0)),
            pl.BlockSpec((None, fb2, l), lambda bi, o, kb: (kb, 0, 0)),
            pl.BlockSpec((None, l, fb2), lambda bi, o, kb: (kb, 0, 0)),
            spec(0), spec(1),
        ],
        out_specs=pl.BlockSpec((None, l, W), lambda bi, o, kb: (bi, 0, 0)),
        out_shape=jax.ShapeDtypeStruct((b, l, W), BF16),
        scratch_shapes=[pltpu.VMEM((l, W), F32), pltpu.VMEM((l, W), BF16), pltpu.VMEM((l, W), F32)],
        compiler_params=_cparams("parallel", "arbitrary", "arbitrary"),
        name="hyena",
    )(u, u, u, u, cw, conv_b.reshape(3, 1, W), hy_bias.reshape(2, 1, W), f_blocks, ft_blocks, spectra, spectra)


NA_Q_ROWS = 4
NA_K_ROWS = 12
NEG_BIG = -1e30


def _na_bias_tables(rpb, rows):
    nblk = rows // NA_Q_ROWS
    heads = rpb.shape[0]
    cq = jnp.arange(GRID_W)[:, None]
    kc = jnp.arange(GRID_W)[None, :]
    c0 = jnp.clip(cq - NA_WIN_C // 2, 0, GRID_W - NA_WIN_C)
    col_ok = (kc >= c0) & (kc < c0 + NA_WIN_C)
    onehot = (jnp.arange(2 * NA_WIN_C - 1)[:, None, None] == (kc - cq + NA_WIN_C - 1)[None]).astype(F32)
    tiles = jnp.einsum("hrj,jck->hrck", rpb.astype(F32), onehot, precision=lax.Precision.HIGHEST)
    tiles = jnp.where(col_ok[None, None], tiles, NEG_BIG)
    masked = jnp.full((heads, GRID_W, GRID_W), NEG_BIG, F32)
    tables = []
    for i in (0, 1, nblk - 1):
        ks = min(max(NA_Q_ROWS * (i - 1), 0), rows - NA_K_ROWS)
        q_rows = []
        for qr in range(NA_Q_ROWS):
            rq = NA_Q_ROWS * i + qr
            r0 = min(max(rq - NA_WIN_R // 2, 0), rows - NA_WIN_R)
            k_tiles = []
            for w in range(NA_K_ROWS):
                kr = ks + w
                k_tiles.append(tiles[:, kr - rq + NA_WIN_R - 1] if r0 <= kr < r0 + NA_WIN_R else masked)
            q_rows.append(jnp.concatenate(k_tiles, axis=-1))
        tables.append(jnp.concatenate(q_rows, axis=-2))
    return jnp.stack(tables, axis=1)


def _na_kernel(q_ref, k_ref, v_ref, z_ref, bias_ref, o_ref, *, l):
    rows = l // GRID_W
    nblk = rows // NA_Q_ROWS
    qn = NA_Q_ROWS * GRID_W
    kn = NA_K_ROWS * GRID_W
    n_ctx = k_ref.shape[0] - l
    lane_lo = _lane_ids((1, LANES)) < HD
    k_ctx = k_ref[pl.ds(l, n_ctx), :]
    v_ctx = v_ref[pl.ds(l, n_ctx), :]

    def body(i, carry):
        q0 = pl.multiple_of(i * qn, qn)
        ks = jnp.clip(NA_Q_ROWS * (i - 1), 0, rows - NA_K_ROWS)
        k0 = pl.multiple_of(ks * GRID_W, GRID_W)
        kind = jnp.where(i == 0, 0, jnp.where(i == nblk - 1, 2, 1))
        q = q_ref[pl.ds(q0, qn), :].astype(F32) * HD ** -0.5
        k_win = k_ref[pl.ds(k0, kn), :]
        v_win = v_ref[pl.ds(k0, kn), :]
        outs = []
        for hh in range(2):
            q_h = jnp.where(lane_lo if hh == 0 else jnp.logical_not(lane_lo), q, 0.0)
            s_win = _bdot_nt(q_h, k_win) + bias_ref[hh, kind]
            s_ctx = _bdot_nt(q_h, k_ctx)
            m = jnp.maximum(jnp.max(s_win, axis=-1, keepdims=True), jnp.max(s_ctx, axis=-1, keepdims=True))
            p_win = jnp.exp(s_win - m)
            p_ctx = jnp.exp(s_ctx - m)
            den = jnp.sum(p_win, axis=-1, keepdims=True) + jnp.sum(p_ctx, axis=-1, keepdims=True)
            outs.append((_bdot(p_win, v_win) + _bdot(p_ctx, v_ctx)) / den)
        rws = pl.ds(q0, qn)
        gate = _silu(z_ref[rws, :].astype(F32))
        o_ref[rws, :] = (jnp.where(lane_lo, outs[0], outs[1]) * gate).astype(o_ref.dtype)
        return carry

    lax.fori_loop(0, nblk, body, 0)


def _natten(u, l, col_q, col_k, col_v, col_z, bias):
    b, t, _ = u.shape
    lat = lambda col: pl.BlockSpec((None, l, LANES), lambda bi, p, col=col: (bi, 0, col // LANES + p))
    full = lambda col: pl.BlockSpec((None, t, LANES), lambda bi, p, col=col: (bi, 0, col // LANES + p))
    return pl.pallas_call(
        functools.partial(_na_kernel, l=l),
        grid=(b, PAIRS),
        in_specs=[lat(col_q), full(col_k), full(col_v), lat(col_z),
                  pl.BlockSpec((2,) + bias.shape[1:], lambda bi, p: (p, 0, 0, 0))],
        out_specs=pl.BlockSpec((None, l, LANES), lambda bi, p: (bi, 0, p)),
        out_shape=jax.ShapeDtypeStruct((b, l, W), BF16),
        compiler_params=_cparams("parallel", "parallel"),
        name="natten",
    )(u, u, u, u, bias)


def kernel(x, c, ctx, c_ctx, ada_w, ada_b, norm_w, final_norm_w, even_w_in, even_w_out, ret_decay, rw_mu, rw_w0, rw_w2, rw_a0, rw_a2, rw_kk, rw_ka, rw_rk, rw_ln_w, rw_ln_b, odd_w_in, odd_w_out, hy_conv_w, hy_conv_b, hy_w1, hy_b1, hy_f1, hy_w2, hy_b2, hy_f2, hy_w3, hy_bias, na_rpb):
    b, l, d = x.shape
    n_ctx = ctx.shape[1]
    t = l + n_ctx
    depth = ada_w.shape[0]
    assert depth == 2 and l % ROW_TILE == 0 and n_ctx % ROW_TILE == 0 and (l // GRID_W) >= NA_K_ROWS

    cond_rows = -(-(b + 1) // 8) * 8
    cond = jnp.concatenate([c, c_ctx[None, :], jnp.zeros((cond_rows - b - 1, d), F32)], axis=0)
    mod_all = _ada_mod(cond, ada_w, ada_b)
    mods = [jnp.stack([mod_all[i, :b].reshape(b, 3, d),
                       jnp.broadcast_to(mod_all[i, b].reshape(1, 3, d), (b, 3, d))], axis=1) for i in range(depth)]

    h = jnp.concatenate([x, ctx], axis=1)

    col_rz, col_wz, col_shift = 3 * W, 4 * W, 5 * W
    u0 = _norm_proj(h, norm_w[0], mods[0], even_w_in[0].astype(BF16), l)
    cos_t, sin_t = _rope_tables(l, t)
    ret_o = _retention(u0, cos_t, sin_t, ret_decay[0], l, 0, W, 2 * W, col_rz)
    rw_o = _rwkv(u0, l, col_shift, col_wz, rw_mu[0], rw_w0[0], rw_w2[0], rw_a0[0], rw_a2[0], rw_kk[0], rw_ka[0],
                 rw_rk[0], rw_ln_w[0], rw_ln_b[0])
    h = _out_proj(ret_o, rw_o, even_w_out[0].astype(BF16), h, mods[0], l)

    u1 = _norm_proj(h, norm_w[1], mods[1], odd_w_in[0].astype(BF16), l)
    f_blocks, ft_blocks = _dft_matrices(l)
    filt = _hy_filters(l, hy_w1[0], hy_b1[0], hy_f1[0], hy_w2[0], hy_b2[0], hy_f2[0], hy_w3[0])
    spectra = _hy_spectra(f_blocks, filt)
    hy_o = _hyena(u1, l, hy_conv_w[0], hy_conv_b[0], hy_bias[0], f_blocks, ft_blocks, spectra)
    na_o = _natten(u1, l, 4 * W, 5 * W, 6 * W, 7 * W, _na_bias_tables(na_rpb[0], l // GRID_W))
    return _out_proj(hy_o, na_o, odd_w_out[0].astype(BF16), h, mods[1], l, final_norm_w)
```

```python
import functools
import math

import jax
import jax.numpy as jnp
from jax import lax
from jax.experimental import pallas as pl
from jax.experimental.pallas import tpu as pltpu

F32 = jnp.float32
BF16 = jnp.bfloat16

GRID_W = 64
NORM_EPS = 1e-6
ROPE_BASE = 10000.0
HEADS = 8
HD = 64
W = HEADS * HD
LORA = 64
RWKV_GN_EPS = 64e-5
HY_BANDS = 16
HY_TARGET = 1e-2
HY_FAST_PCT = 0.3
HY_SLOW_PCT = 1.5
NA_WIN_R = 8
NA_WIN_C = 16
LANES = 128
PAIRS = W // LANES
RET_CHUNK = 128
RW_CHUNK = 64
RW_SUB = 32
ROW_TILE = 256
HALO = 16
VMEM_LIMIT = 56 * 1024 * 1024


def _cparams(*sem):
    return pltpu.CompilerParams(dimension_semantics=sem, vmem_limit_bytes=VMEM_LIMIT)


def _bdot(a, b):
    return jnp.dot(a.astype(BF16), b.astype(BF16), preferred_element_type=F32)


def _bdot_nt(a, b):
    return lax.dot_general(a.astype(BF16), b.astype(BF16), (((1,), (1,)), ((), ())), preferred_element_type=F32)


def _bdot_tn(a, b):
    return lax.dot_general(a.astype(BF16), b.astype(BF16), (((0,), (0,)), ((), ())), preferred_element_type=F32)


def _split3(x):
    hi = x.astype(BF16)
    r1 = x - hi.astype(F32)
    mid = r1.astype(BF16)
    lo = (r1 - mid.astype(F32)).astype(BF16)
    return hi, mid, lo


def _dot_exact_lhs(m, x):
    mb = m.astype(BF16)
    hi, mid, lo = _split3(x)
    d = lambda p: jnp.dot(mb, p, preferred_element_type=F32)
    return d(hi) + d(mid) + d(lo)


def _dot_hi(a, b):
    ah = a.astype(BF16)
    al = (a - ah.astype(F32)).astype(BF16)
    bh = b.astype(BF16)
    bl = (b - bh.astype(F32)).astype(BF16)
    d = lambda p, q: jnp.dot(p, q, preferred_element_type=F32)
    return d(ah, bh) + d(ah, bl) + d(al, bh)


def _silu(x):
    return x * (1.0 / (1.0 + jnp.exp(-x)))


def _sigmoid(x):
    return 1.0 / (1.0 + jnp.exp(-x))


def _softplus(x):
    return jnp.maximum(x, 0.0) + jnp.log(1.0 + jnp.exp(-jnp.abs(x)))


def _lane_ids(shape):
    return lax.broadcasted_iota(jnp.int32, shape, len(shape) - 1)


def _row_ids(shape):
    return lax.broadcasted_iota(jnp.int32, shape, len(shape) - 2)


def _head_sum_mxu(x, ones_bd):
    hi = x.astype(BF16)
    lo = (x - hi.astype(F32)).astype(BF16)
    tiles = []
    for p in range(x.shape[-1] // LANES):
        sl = slice(p * LANES, (p + 1) * LANES)
        tiles.append(jnp.dot(hi[:, sl], ones_bd, preferred_element_type=F32)
                     + jnp.dot(lo[:, sl], ones_bd, preferred_element_type=F32))
    return tiles[0] if len(tiles) == 1 else jnp.concatenate(tiles, axis=-1)


def _with_neighbours(ref, cols, c0, c, n_rows, lat_rows):
    row = lax.broadcasted_iota(jnp.int32, (c, 1), 0)
    x = ref[pl.ds(c0, c), cols].astype(F32)
    p0 = pl.multiple_of(jnp.maximum(c0 - HALO, 0), HALO)
    n0 = pl.multiple_of(jnp.minimum(c0 + c, n_rows - HALO), HALO)
    prev_row = ref[pl.ds(p0, HALO), cols].astype(F32)[HALO - 1:HALO, :]
    next_row = ref[pl.ds(n0, HALO), cols].astype(F32)[0:1, :]
    has_prev = jnp.logical_and(c0 != 0, c0 != lat_rows)
    has_next = jnp.logical_and(c0 + c != lat_rows, c0 + c != n_rows)
    prev_row = jnp.where(has_prev, prev_row, 0.0)
    next_row = jnp.where(has_next, next_row, 0.0)
    prev = jnp.where(row == 0, prev_row, pltpu.roll(x, 1, 0))
    nxt = jnp.where(row == c - 1, next_row, pltpu.roll(x, c - 1, 0))
    return x, prev, nxt


def _scan_chunk_id(i, n_lat, n_ctx, reverse):
    if reverse:
        return jnp.where(i < n_ctx, n_lat + n_ctx - 1 - i, n_lat - 1 - (i - n_ctx))
    return jnp.where(i < n_ctx, n_lat + i, i - n_ctx)


def _ada_kernel(c_ref, w_ref, b_ref, o_ref):
    o_ref[...] = _dot_hi(_silu(c_ref[...]), w_ref[...]) + b_ref[...]


def _ada_mod(cond, ada_w, ada_b):
    depth, d, d3 = ada_w.shape
    rows = cond.shape[0]
    return pl.pallas_call(
        _ada_kernel,
        grid=(depth, d3 // d),
        in_specs=[
            pl.BlockSpec((rows, d), lambda i, j: (0, 0)),
            pl.BlockSpec((None, d, d), lambda i, j: (i, 0, j)),
            pl.BlockSpec((None, 1, d), lambda i, j: (i, 0, j)),
        ],
        out_specs=pl.BlockSpec((None, rows, d), lambda i, j: (i, 0, j)),
        out_shape=jax.ShapeDtypeStruct((depth, rows, d3), F32),
        compiler_params=_cparams("parallel", "parallel"),
        name="ada_mod",
    )(cond, ada_w, ada_b.reshape(depth, 1, d3))


def _norm_proj_kernel(h_ref, nw_ref, mod_ref, w_ref, o_ref):
    x = h_ref[...]
    y = x * lax.rsqrt(jnp.mean(x * x, axis=-1, keepdims=True) + NORM_EPS) * nw_ref[...]
    n = y * (1.0 + mod_ref[1:2, :]) + mod_ref[0:1, :]
    o_ref[...] = _bdot(n, w_ref[...]).astype(o_ref.dtype)


def _norm_proj(h, norm_w, mod, w, n_lat):
    b, t, d = h.shape
    n = w.shape[1]
    tm = ROW_TILE
    lat_tiles = n_lat // tm
    return pl.pallas_call(
        _norm_proj_kernel,
        grid=(b, t // tm),
        in_specs=[
            pl.BlockSpec((None, tm, d), lambda bi, i: (bi, i, 0)),
            pl.BlockSpec((1, d), lambda bi, i: (0, 0)),
            pl.BlockSpec((None, None, 3, d), lambda bi, i: (bi, (i >= lat_tiles).astype(jnp.int32), 0, 0)),
            pl.BlockSpec((d, n), lambda bi, i: (0, 0)),
        ],
        out_specs=pl.BlockSpec((None, tm, n), lambda bi, i: (bi, i, 0)),
        out_shape=jax.ShapeDtypeStruct((b, t, n), BF16),
        compiler_params=_cparams("parallel", "parallel"),
        name="norm_proj",
    )(h, norm_w.reshape(1, d), mod, w)


def _ret_kernel(q_ref, k_ref, v_ref, z_ref, cos_ref, sin_ref, rd_ref, o_ref, y_acc, st_ref, dm_ref, qd_ref, kd_ref,
                *, n_lat, n_ctx):
    c = RET_CHUNK
    lane = _lane_ids((1, LANES))
    lane_lo = lane < HD
    rope_lo = (lane % HD) < (HD // 2)
    pos = lax.broadcasted_iota(jnp.int32, (c, 1), 0).astype(F32)
    ti = lax.broadcasted_iota(jnp.int32, (c, c), 0)
    si = lax.broadcasted_iota(jnp.int32, (c, c), 1)
    same_head = (_row_ids((LANES, LANES)) < HD) == (_lane_ids((LANES, LANES)) < HD)
    ones_bd = same_head.astype(BF16)
    inst = [(d, pr) for d in range(2) for pr in range(PAIRS)]
    sl = lambda pr: slice(pr * LANES, (pr + 1) * LANES)

    g_chunk = []
    for d, pr in inst:
        lg_a = -jnp.exp(rd_ref[d, 2 * pr])
        lg_b = -jnp.exp(rd_ref[d, 2 * pr + 1])
        lgv = jnp.where(lane_lo, lg_a, lg_b)
        diff = ((ti - si) if d == 0 else (si - ti)).astype(F32)
        keep = diff >= 0
        dmat = lambda lg: jnp.where(keep, jnp.exp(jnp.where(keep, diff, 0.0) * lg), 0.0)
        dm_ref[d, pr] = jnp.concatenate([dmat(lg_a), dmat(lg_b)], axis=0)
        qd_ref[d, :, sl(pr)] = jnp.exp(((pos + 1.0) if d == 0 else (c - pos)) * lgv)
        kd_ref[d, :, sl(pr)] = jnp.exp(((c - 1.0 - pos) if d == 0 else pos) * lgv)
        g_chunk.append(jnp.exp(c * lgv))
    y_acc[...] = jnp.zeros_like(y_acc)
    st_ref[...] = jnp.zeros_like(st_ref)

    def rope(x, rows):
        partner = jnp.where(rope_lo, pltpu.roll(x, LANES - HD // 2, 1), pltpu.roll(x, HD // 2, 1))
        return x * cos_ref[rows, :] + partner * sin_ref[rows, :]

    def stack2(x):
        return jnp.concatenate([jnp.where(lane_lo, x, 0.0), jnp.where(lane_lo, 0.0, x)], axis=0)

    def step(i, carry):
        rows = [pl.ds(pl.multiple_of(_scan_chunk_id(i, n_lat, n_ctx, d == 1) * c, c), c) for d in range(2)]
        each = lambda fn: [fn(j, *inst[j]) for j in range(len(inst))]
        q = each(lambda j, d, pr: rope(q_ref[rows[d], sl(pr)].astype(F32), rows[d]))
        k = each(lambda j, d, pr: rope(k_ref[rows[d], sl(pr)].astype(F32) * HD ** -0.5, rows[d]))
        v = each(lambda j, d, pr: v_ref[rows[d], sl(pr)])
        s0 = each(lambda j, d, pr: st_ref[d, pr])
        scores = each(lambda j, d, pr: _bdot_nt(stack2(q[j]), k[j]) * dm_ref[d, pr])
        cross = each(lambda j, d, pr: _bdot(q[j] * qd_ref[d, :, sl(pr)], s0[j]))
        upd = each(lambda j, d, pr: _bdot_tn(k[j] * kd_ref[d, :, sl(pr)], v[j]))
        intra = each(lambda j, d, pr: _bdot(scores[j], v[j]))
        for j, (d, pr) in enumerate(inst):
            st_ref[d, pr] = s0[j] * g_chunk[j] + jnp.where(same_head, upd[j], 0.0)
            y_acc[rows[d], sl(pr)] += jnp.where(lane_lo, intra[j][:c], intra[j][c:]) + cross[j]
        return carry

    lax.fori_loop(0, n_lat + n_ctx, step, 0)

    def finish(i, carry):
        rows = pl.ds(pl.multiple_of(i * ROW_TILE, ROW_TILE), ROW_TILE)
        y = y_acc[rows, :]
        ms = _head_sum_mxu(y * y, ones_bd) * (1.0 / HD)
        o_ref[rows, :] = (y * lax.rsqrt(ms + NORM_EPS) * _silu(z_ref[rows, :].astype(F32))).astype(o_ref.dtype)
        return carry

    lax.fori_loop(0, (n_lat + n_ctx) * c // ROW_TILE, finish, 0)


def _retention(u, cos_t, sin_t, ret_decay, n_lat_rows, col_q, col_k, col_v, col_z):
    b, t, _ = u.shape
    c = RET_CHUNK
    rd = jnp.broadcast_to(ret_decay.astype(F32)[:, :, None, None], (2, HEADS, 1, LANES))
    blk = lambda col: pl.BlockSpec((None, t, W), lambda bi, col=col: (bi, 0, col // W))
    return pl.pallas_call(
        functools.partial(_ret_kernel, n_lat=n_lat_rows // c, n_ctx=(t - n_lat_rows) // c),
        grid=(b,),
        in_specs=[
            blk(col_q), blk(col_k), blk(col_v), blk(col_z),
            pl.BlockSpec((t, LANES), lambda bi: (0, 0)),
            pl.BlockSpec((t, LANES), lambda bi: (0, 0)),
            pl.BlockSpec((2, HEADS, 1, LANES), lambda bi: (0, 0, 0, 0)),
        ],
        out_specs=pl.BlockSpec((None, t, W), lambda bi: (bi, 0, 0)),
        out_shape=jax.ShapeDtypeStruct((b, t, W), BF16),
        scratch_shapes=[pltpu.VMEM((t, W), F32), pltpu.VMEM((2, PAIRS, LANES, LANES), F32),
                        pltpu.VMEM((2, PAIRS, 2 * c, c), F32), pltpu.VMEM((2, c, W), F32),
                        pltpu.VMEM((2, c, W), F32)],
        compiler_params=_cparams("parallel"),
        name="retention",
    )(u, u, u, u, cos_t, sin_t, rd)


def _rope_tables(n_lat_rows, t):
    pos = jnp.arange(n_lat_rows)
    row = (pos // GRID_W).astype(F32)
    col = (pos % GRID_W).astype(F32)
    nf = HD // 4
    inv = ROPE_BASE ** (-jnp.arange(nf, dtype=F32) / nf)
    ang = jnp.concatenate([row[:, None] * inv, col[:, None] * inv], axis=-1)
    cos, sin = jnp.cos(ang), jnp.sin(ang)
    cos_h = jnp.concatenate([cos, cos], axis=-1)
    sin_h = jnp.concatenate([-sin, sin], axis=-1)
    reps = LANES // HD
    cos_t = jnp.concatenate([jnp.tile(cos_h, (1, reps)), jnp.ones((t - n_lat_rows, LANES), F32)], axis=0)
    sin_t = jnp.concatenate([jnp.tile(sin_h, (1, reps)), jnp.zeros((t - n_lat_rows, LANES), F32)], axis=0)
    return cos_t, sin_t


def _rwkv_kernel(r_ref, k_ref, v_ref, lo_ref, z_ref, mu_ref, mulo_ref, kkw_ref, ka_ref, w0_ref, a0_ref, w2_ref,
                 a2_ref, rk_ref, lnw_ref, lnb_ref, o_ref, y_acc, b_acc, st_ref, prep_ref, plast_ref, *, n_lat, n_ctx):
    c = RW_CHUNK
    c2 = 2 * c
    t_rows = (n_lat + n_ctx) * c
    lat_rows = n_lat * c
    lane_lo = _lane_ids((1, LANES)) < HD
    ri = _row_ids((c2, c2))
    ci = _lane_ids((c2, c2))
    same_head = (ri < c) == (ci < c)
    rt_, ct_ = ri % c, ci % c
    same_sub = (rt_ // RW_SUB) == (ct_ // RW_SUB)
    eye = (ri == ci).astype(F32)
    ones_bd = same_head.astype(BF16)
    t64 = lax.broadcasted_iota(jnp.int32, (c, c), 0)
    s64 = lax.broadcasted_iota(jnp.int32, (c, c), 1)
    masks = []
    for d in range(2):
        before = (ct_ < rt_) if d == 0 else (ct_ > rt_)
        strict = jnp.logical_and(same_head, before)
        incl = jnp.logical_and(same_head, jnp.logical_or(before, ct_ == rt_))
        tri = ((s64 <= t64) if d == 0 else (s64 >= t64)).astype(BF16)
        masks.append((strict, incl, jnp.logical_and(strict, same_sub), tri))

    y_acc[...] = jnp.zeros_like(y_acc)
    b_acc[...] = jnp.zeros_like(b_acc)
    st_ref[...] = jnp.zeros_like(st_ref)

    def stack2(x):
        return jnp.concatenate([jnp.where(lane_lo, x, 0.0), jnp.where(lane_lo, 0.0, x)], axis=0)

    def dup(x):
        return jnp.concatenate([x, x], axis=0)

    def unstack(x):
        return jnp.where(lane_lo, x[:c], x[c:])

    def mixed(ref, cols, mu, c0):
        x, prev, nxt = _with_neighbours(ref, cols, c0, c, t_rows, lat_rows)
        return x + mu * (0.5 * (prev + nxt) - x)

    n_steps = n_lat + n_ctx
    fields = ("at", "bt", "kt", "rt", "v", "bonus")

    def step_rows(i, d):
        return pl.ds(pl.multiple_of(_scan_chunk_id(i, n_lat, n_ctx, d == 1) * c, c), c)

    def prepare(i, d):
        tri = masks[d][3]
        c0 = pl.multiple_of(_scan_chunk_id(i, n_lat, n_ctx, d == 1) * c, c)
        every = slice(None)
        r = mixed(r_ref, every, mu_ref[0], c0)
        k = mixed(k_ref, every, mu_ref[1], c0)
        v = mixed(v_ref, every, mu_ref[2], c0)
        lo = mixed(lo_ref, every, mulo_ref[...], c0)
        kk = k * kkw_ref[...]
        yield
        kk_norm = jnp.sqrt(_head_sum_mxu(kk * kk, ones_bd))
        w_pre = _bdot(jnp.tanh(lo[:, :LANES]), w2_ref[d])
        a_pre = _bdot(lo[:, LANES:], a2_ref[d])
        yield
        kk = kk / jnp.maximum(kk_norm, 1e-12)
        logw = -jnp.exp(-_softplus(-(w0_ref[d] + w_pre)) - 0.5)
        a = _sigmoid(a0_ref[d] + a_pre)
        k_d = k * (1.0 + (a - 1.0) * ka_ref[...])
        cum = _dot_exact_lhs(tri, logw)
        bonus_sum = _head_sum_mxu(r * k_d * rk_ref[...], ones_bd)
        yield
        p = jnp.exp(cum)
        inv_p = jnp.exp(-cum)
        staged = dict(at=-kk * jnp.exp(cum - logw), bt=kk * a * inv_p, kt=k_d * inv_p, rt=r * p, v=v,
                      bonus=bonus_sum * v)
        yield "stores next"
        for f, name in enumerate(fields):
            prep_ref[d, f] = staged[name]
        plast_ref[d] = p[c - 1:c, :] if d == 0 else p[0:1, :]

    for d in range(2):
        for _ in prepare(0, d):
            pass

    def step(i, carry):
        rows = [step_rows(i, d) for d in range(2)]
        inst = [(d, pr) for d in range(2) for pr in range(PAIRS)]
        each = lambda fn: [fn(j) for j in range(len(inst))]
        sl = lambda j: slice(inst[j][1] * LANES, (inst[j][1] + 1) * LANES)
        op = lambda j, name: prep_ref[inst[j][0], fields.index(name), :, sl(j)]
        msk = lambda j: masks[inst[j][0]]
        p_last = [plast_ref[d] for d in range(2)]
        for d in range(2):
            b_acc[rows[d], :] += prep_ref[d, fields.index("bonus")]
        nxt = jnp.minimum(i + 1, n_steps - 1)
        staging = [[prepare(nxt, d), None] for d in range(2)]

        def advance_staging():
            for s in staging:
                if s[1] is None:
                    s[1] = next(s[0])

        big = each(lambda j: _bdot_nt(jnp.concatenate([stack2(op(j, "at")), stack2(op(j, "rt"))], axis=0),
                                      jnp.concatenate([dup(op(j, "bt")), dup(op(j, "kt"))], axis=0)))
        s0 = each(lambda j: st_ref[inst[j][0], inst[j][1]])
        proj = each(lambda j: _bdot_nt(jnp.concatenate([op(j, "at"), op(j, "rt")], axis=0), s0[j]))
        v2 = each(lambda j: dup(op(j, "v")))
        n_all = each(lambda j: jnp.where(msk(j)[0], big[j][:c2, :c2], 0.0))
        aak_v = each(lambda j: _bdot(jnp.where(msk(j)[0], big[j][:c2, c2:], 0.0), v2[j]))
        advance_staging()
        n_sub = each(lambda j: jnp.where(msk(j)[2], n_all[j], 0.0))
        t_sub = each(lambda j: eye + n_sub[j])
        npow = n_sub
        span = 1
        while 2 * span < RW_SUB:
            npow = each(lambda j: _bdot(npow[j], npow[j]))
            t_sub = each(lambda j: t_sub[j] + _bdot(t_sub[j], npow[j]))
            span *= 2
            advance_staging()
        corr = each(lambda j: _bdot(t_sub[j], n_all[j] - n_sub[j]))
        t_inv = each(lambda j: t_sub[j] + _bdot(corr[j], t_sub[j]))
        u_st = each(lambda j: _bdot(t_inv[j], dup(proj[j][:c]) + aak_v[j]))
        y_st = each(lambda j: _bdot(
            jnp.concatenate([jnp.where(msk(j)[1], big[j][c2:, :c2], 0.0), jnp.where(msk(j)[1], big[j][c2:, c2:], 0.0)],
                            axis=1),
            jnp.concatenate([u_st[j], v2[j]], axis=0)))
        upd = each(lambda j: _bdot_tn(jnp.concatenate([unstack(u_st[j]), op(j, "v")], axis=0),
                                      jnp.concatenate([op(j, "bt"), op(j, "kt")], axis=0)))
        for j, (d, pr) in enumerate(inst):
            st_ref[d, pr] = (s0[j] + jnp.where(same_head, upd[j], 0.0)) * p_last[d][:, sl(j)]
            y_acc[rows[d], sl(j)] += proj[j][c:] + unstack(y_st[j])
        for g, _ in staging:
            for _ in g:
                pass
        return carry

    lax.fori_loop(0, n_steps, step, 0)

    def finish(i, carry):
        rows = pl.ds(pl.multiple_of(i * ROW_TILE, ROW_TILE), ROW_TILE)
        y = y_acc[rows, :]
        mean = _head_sum_mxu(y, ones_bd) * (1.0 / HD)
        yc = y - mean
        var = _head_sum_mxu(yc * yc, ones_bd) * (1.0 / HD)
        gn = yc * lax.rsqrt(var + RWKV_GN_EPS) * lnw_ref[...] + lnb_ref[...]
        o_ref[rows, :] = ((gn + b_acc[rows, :]) * _silu(z_ref[rows, :].astype(F32))).astype(o_ref.dtype)
        return carry

    lax.fori_loop(0, t_rows // ROW_TILE, finish, 0)


def _rwkv(u, n_lat_rows, col_shift, col_z, rw_mu, rw_w0, rw_w2, rw_a0, rw_a2, rw_kk, rw_ka, rw_rk, rw_ln_w,
          rw_ln_b):
    b, t, _ = u.shape
    c = RW_CHUNK
    assert c == HD
    col_lo = col_shift + 3 * W
    def pad_dirs(w2):
        z = jnp.zeros_like(w2[0])
        return jnp.stack([jnp.concatenate([w2[0], z], axis=0), jnp.concatenate([z, w2[1]], axis=0)]).astype(BF16)

    row = lambda a: a.reshape(1, W).astype(F32)
    blk = lambda col, width=W: pl.BlockSpec((None, t, width), lambda bi, col=col, width=width: (bi, 0, col // width))
    full = lambda shape: pl.BlockSpec(shape, lambda bi: (0,) * len(shape))
    return pl.pallas_call(
        functools.partial(_rwkv_kernel, n_lat=n_lat_rows // c, n_ctx=(t - n_lat_rows) // c),
        grid=(b,),
        in_specs=[
            blk(col_shift), blk(col_shift + W), blk(col_shift + 2 * W), blk(col_lo, 2 * LANES), blk(col_z),
            full((3, 1, W)), full((1, 2 * LANES)), full((1, W)), full((1, W)), full((2, 1, W)), full((2, 1, W)),
            full((2, 2 * LORA, W)), full((2, 2 * LORA, W)), full((1, W)), full((1, W)), full((1, W)),
        ],
        out_specs=pl.BlockSpec((None, t, W), lambda bi: (bi, 0, 0)),
        out_shape=jax.ShapeDtypeStruct((b, t, W), BF16),
        scratch_shapes=[pltpu.VMEM((t, W), F32), pltpu.VMEM((t, W), F32),
                        pltpu.VMEM((2, PAIRS, LANES, LANES), F32), pltpu.VMEM((2, 6, c, W), F32),
                        pltpu.VMEM((2, 1, W), F32)],
        compiler_params=_cparams("parallel"),
        name="rwkv7",
    )(u, u, u, u, u, rw_mu[:3 * W].reshape(3, 1, W), rw_mu[3 * W:].reshape(1, 2 * LANES), row(rw_kk), row(rw_ka),
      rw_w0.reshape(2, 1, W), rw_a0.reshape(2, 1, W), pad_dirs(rw_w2), pad_dirs(rw_a2), row(rw_rk), row(rw_ln_w),
      row(rw_ln_b))


def _out_proj_kernel(a_ref, b_ref, w_ref, h_ref, mod_ref, *rest, final):
    wa = a_ref.shape[-1]
    y = _bdot(a_ref[...], w_ref[:wa, :]) + _bdot(b_ref[...], w_ref[wa:, :])
    hn = h_ref[...] + mod_ref[2:3, :] * y
    if final:
        fnw_ref, o_ref = rest
        o_ref[...] = hn * lax.rsqrt(jnp.mean(hn * hn, axis=-1, keepdims=True) + NORM_EPS) * fnw_ref[...]
    else:
        (o_ref,) = rest
        o_ref[...] = hn


def _out_proj(a, b, w, h, mod, n_lat, final_norm_w=None):
    bsz, rows, wa = a.shape
    d = h.shape[-1]
    tm = ROW_TILE
    lat_tiles = n_lat // tm
    final = final_norm_w is not None
    in_specs = [
        pl.BlockSpec((None, tm, wa), lambda bi, i: (bi, i, 0)),
        pl.BlockSpec((None, tm, b.shape[-1]), lambda bi, i: (bi, i, 0)),
        pl.BlockSpec(w.shape, lambda bi, i: (0, 0)),
        pl.BlockSpec((None, tm, d), lambda bi, i: (bi, i, 0)),
        pl.BlockSpec((None, None, 3, d), lambda bi, i: (bi, (i >= lat_tiles).astype(jnp.int32), 0, 0)),
    ]
    args = [a, b, w, h, mod]
    if final:
        in_specs.append(pl.BlockSpec((1, d), lambda bi, i: (0, 0)))
        args.append(final_norm_w.reshape(1, d))
    return pl.pallas_call(
        functools.partial(_out_proj_kernel, final=final),
        grid=(bsz, rows // tm),
        in_specs=in_specs,
        out_specs=pl.BlockSpec((None, tm, d), lambda bi, i: (bi, i, 0)),
        out_shape=jax.ShapeDtypeStruct((bsz, rows, d), F32),
        compiler_params=_cparams("parallel", "parallel"),
        name="out_proj_final" if final else "out_proj",
    )(*args)


HY_FREQ_BLOCK = 256


def _dft_matrices(l):
    fb = HY_FREQ_BLOCK
    n = 2 * l
    k = jnp.arange(l, dtype=jnp.int32)[:, None]
    split = 64
    a = jnp.arange(l // split, dtype=jnp.int32)[None, :] * split
    bb = jnp.arange(split, dtype=jnp.int32)[None, :]
    ang = lambda m: (jnp.pi / n) * (((2 * k + 1) * m) % (2 * n)).astype(F32)
    ca, sa = jnp.cos(ang(a))[:, :, None], jnp.sin(ang(a))[:, :, None]
    cb, sb = jnp.cos(ang(bb))[:, None, :], jnp.sin(ang(bb))[:, None, :]
    cos = (ca * cb - sa * sb).reshape(l, l)
    sin = (sa * cb + ca * sb).reshape(l, l)
    f = jnp.concatenate([cos.reshape(l // fb, fb, l), -sin.reshape(l // fb, fb, l)], axis=1)
    return f.astype(BF16), jnp.swapaxes(f, 1, 2).astype(BF16)


def _hy_filter_kernel(z_ref, w1_ref, b1_ref, f1_ref, w2_ref, b2_ref, f2_ref, w3_ref, dec_ref, o_ref):
    hid = jnp.sin(f1_ref[...] * (_dot_hi(z_ref[...], w1_ref[...]) + b1_ref[...]))
    hid = jnp.sin(f2_ref[...] * (_dot_hi(hid, w2_ref[...]) + b2_ref[...]))
    h = _dot_hi(hid, w3_ref[...]) * dec_ref[...]
    j = pl.program_id(0)
    row = lax.broadcasted_iota(jnp.int32, h.shape, 0)
    o_ref[...] = jnp.where(jnp.logical_and(j % 2 == 1, row == 0), 0.0, h).astype(o_ref.dtype)


def _hy_filters(l, w1, b1, f1, w2, b2, f2, w3):
    t = jnp.linspace(0.0, 1.0, l)[:, None]
    bands = jnp.linspace(1e-4, HY_BANDS - 1, HY_BANDS)
    ang = (2.0 * math.pi / l) * jnp.arange(l, dtype=F32)[:, None] * bands[None]
    z = jnp.concatenate([t, jnp.cos(ang), -jnp.sin(ang)], axis=-1)
    emb = z.shape[1]
    emb_pad = 64
    z = jnp.pad(z, ((0, 0), (0, emb_pad - emb)))
    w1p = jnp.pad(w1, ((0, emb_pad - emb), (0, 0)))
    deltas = jnp.abs(jnp.linspace(math.log(HY_TARGET) / HY_SLOW_PCT, math.log(HY_TARGET) / HY_FAST_PCT, W))
    dec = jnp.exp(-t * deltas)
    ffn = w1.shape[1]
    nblk = w3.shape[1] // W
    full = lambda shape: pl.BlockSpec(shape, lambda j: (0,) * len(shape))
    return pl.pallas_call(
        _hy_filter_kernel,
        grid=(nblk,),
        in_specs=[full((l, emb_pad)), full((emb_pad, ffn)), full((1, ffn)), full((1, ffn)), full((ffn, ffn)),
                  full((1, ffn)), full((1, ffn)), pl.BlockSpec((ffn, W), lambda j: (0, j)), full((l, W))],
        out_specs=pl.BlockSpec((l, W), lambda j: (0, j)),
        out_shape=jax.ShapeDtypeStruct((l, nblk * W), F32),
        compiler_params=_cparams("parallel"),
        name="hyena_filters",
    )(z, w1p, b1.reshape(1, -1), f1.reshape(1, -1), w2, b2.reshape(1, -1), f2.reshape(1, -1), w3, dec)


def _hy_spectrum_kernel(f_ref, h_ref, o_ref):
    o_ref[...] = _bdot(f_ref[...], h_ref[...])


def _hy_spectra(f_blocks, filt):
    nkb, fb2, l = f_blocks.shape
    cols = filt.shape[1]
    return pl.pallas_call(
        _hy_spectrum_kernel,
        grid=(nkb, cols // W),
        in_specs=[pl.BlockSpec((None, fb2, l), lambda kb, j: (kb, 0, 0)), pl.BlockSpec((l, W), lambda kb, j: (0, j))],
        out_specs=pl.BlockSpec((None, fb2, W), lambda kb, j: (kb, 0, j)),
        out_shape=jax.ShapeDtypeStruct((nkb, fb2, cols), F32),
        compiler_params=_cparams("parallel", "parallel"),
        name="hyena_spectra",
    )(f_blocks, filt)


def _hyena_kernel(v_ref, x1_ref, x2_ref, g_ref, cw_ref, cb_ref, bias_ref, f_ref, ft_ref, sf_ref, sb_ref,
                  o_ref, z_ref, zb_ref, acc_ref, *, l):
    o = pl.program_id(1)
    kb = pl.program_id(2)
    nkb = pl.num_programs(2)
    fb = HY_FREQ_BLOCK
    rc = ROW_TILE

    def conv_rows(ref, sec, c0):
        x, prev, nxt = _with_neighbours(ref, slice(None), c0, rc, l, l)
        w = cw_ref[sec]
        return prev * w[0:1, :] + x * w[1:2, :] + nxt * w[2:3, :] + cb_ref[sec]

    @pl.when(jnp.logical_and(o == 0, kb == 0))
    def _():
        def body(i, carry):
            c0 = pl.multiple_of(i * rc, rc)
            zc = conv_rows(v_ref, 0, c0)
            z_ref[pl.ds(c0, rc), :] = zc
            zb_ref[pl.ds(c0, rc), :] = zc.astype(BF16)
            return carry

        lax.fori_loop(0, l // rc, body, 0)

    @pl.when(kb == 0)
    def _():
        acc_ref[...] = jnp.zeros_like(acc_ref)

    x = jnp.dot(f_ref[...], zb_ref[...], preferred_element_type=F32)
    xr, xi = x[:fb], x[fb:]
    sf, sb = sf_ref[...], sb_ref[...]
    gr = sf[:fb] + sb[:fb]
    gi = sf[fb:] - sb[fb:]
    y = jnp.concatenate([xr * gr - xi * gi, xr * gi + xi * gr], axis=0)
    acc_ref[...] += jnp.dot(ft_ref[...], y.astype(BF16), preferred_element_type=F32)

    @pl.when(kb == nkb - 1)
    def _():
        def body(i, carry):
            c0 = pl.multiple_of(i * rc, rc)
            rows = pl.ds(c0, rc)
            conv = acc_ref[rows, :] * (1.0 / l) + z_ref[rows, :] * bias_ref[o]

            @pl.when(o == 0)
            def _():
                zn = conv_rows(x1_ref, 1, c0) * conv
                z_ref[rows, :] = zn
                zb_ref[rows, :] = zn.astype(BF16)

            @pl.when(o == 1)
            def _():
                zn = conv_rows(x2_ref, 2, c0) * conv
                o_ref[rows, :] = (zn * _silu(g_ref[rows, :].astype(F32))).astype(o_ref.dtype)

            return carry

        lax.fori_loop(0, l // rc, body, 0)


def _hyena(u, l, conv_w, conv_b, hy_bias, f_blocks, ft_blocks, spectra):
    b = u.shape[0]
    nkb, fb2, _ = f_blocks.shape
    sec = lambda s: pl.BlockSpec((None, l, W), lambda bi, o, kb, s=s: (bi, 0, s))
    cw = jnp.transpose(conv_w.reshape(3, 3, W), (1, 0, 2))
    spec = lambda direction: pl.BlockSpec(
        (None, fb2, W), lambda bi, o, kb, direction=direction: (kb, 0, 2 * o + direction))
    return pl.pallas_call(
        functools.partial(_hyena_kernel, l=l),
        grid=(b, 2, nkb),
        in_specs=[
            sec(0), sec(1), sec(2), sec(3),
            pl.BlockSpec((3, 3, W), lambda bi, o, kb: (0, 0, 0)),
            pl.BlockSpec((3, 1, W), lambda bi, o, kb: (0, 0, 0)),
            pl.BlockSpec((2, 1, W), lambda bi, o, kb: (0, 0, 0)),
            pl.BlockSpec((None, fb2, l), lambda bi, o, kb: (kb, 0, 0)),
            pl.BlockSpec((None, l, fb2), lambda bi, o, kb: (kb, 0, 0)),
            spec(0), spec(1),
        ],
        out_specs=pl.BlockSpec((None, l, W), lambda bi, o, kb: (bi, 0, 0)),
        out_shape=jax.ShapeDtypeStruct((b, l, W), BF16),
        scratch_shapes=[pltpu.VMEM((l, W), F32), pltpu.VMEM((l, W), BF16), pltpu.VMEM((l, W), F32)],
        compiler_params=_cparams("parallel", "arbitrary", "arbitrary"),
        name="hyena",
    )(u, u, u, u, cw, conv_b.reshape(3, 1, W), hy_bias.reshape(2, 1, W), f_blocks, ft_blocks, spectra, spectra)


NA_Q_ROWS = 4
NA_K_ROWS = 12
NA_BLOCKS_PER_STEP = 2
NEG_BIG = -1e30


def _na_bias_tables(rpb, rows):
    nblk = rows // NA_Q_ROWS
    heads = rpb.shape[0]
    cq = jnp.arange(GRID_W)[:, None]
    kc = jnp.arange(GRID_W)[None, :]
    c0 = jnp.clip(cq - NA_WIN_C // 2, 0, GRID_W - NA_WIN_C)
    col_ok = (kc >= c0) & (kc < c0 + NA_WIN_C)
    onehot = (jnp.arange(2 * NA_WIN_C - 1)[:, None, None] == (kc - cq + NA_WIN_C - 1)[None]).astype(F32)
    tiles = jnp.einsum("hrj,jck->hrck", rpb.astype(F32), onehot, precision=lax.Precision.HIGHEST)
    tiles = jnp.where(col_ok[None, None], tiles, NEG_BIG)
    masked = jnp.full((heads, GRID_W, GRID_W), NEG_BIG, F32)
    tables = []
    for i in (0, 1, nblk - 1):
        ks = min(max(NA_Q_ROWS * (i - 1), 0), rows - NA_K_ROWS)
        q_rows = []
        for qr in range(NA_Q_ROWS):
            rq = NA_Q_ROWS * i + qr
            r0 = min(max(rq - NA_WIN_R // 2, 0), rows - NA_WIN_R)
            k_tiles = []
            for w in range(NA_K_ROWS):
                kr = ks + w
                k_tiles.append(tiles[:, kr - rq + NA_WIN_R - 1] if r0 <= kr < r0 + NA_WIN_R else masked)
            q_rows.append(jnp.concatenate(k_tiles, axis=-1))
        tables.append(jnp.concatenate(q_rows, axis=-2))
    tab = jnp.stack(tables, axis=1)
    qn, kn = tab.shape[-2:]
    tab = tab.reshape(heads // 2, 2, 3, qn, kn).transpose(0, 2, 1, 3, 4)
    return tab.reshape(heads // 2, 3, 2 * qn, kn)


def _na_kernel(q_ref, k_ref, v_ref, z_ref, bias_ref, o_ref, *, l):
    rows = l // GRID_W
    nblk = rows // NA_Q_ROWS
    qn = NA_Q_ROWS * GRID_W
    kn = NA_K_ROWS * GRID_W
    n_ctx = k_ref.shape[0] - l
    lane_lo = _lane_ids((1, LANES)) < HD
    k_ctx = k_ref[pl.ds(l, n_ctx), :]
    v_ctx = v_ref[pl.ds(l, n_ctx), :]

    def body(it, carry):
        blocks = [it * NA_BLOCKS_PER_STEP + j for j in range(NA_BLOCKS_PER_STEP)]
        each = lambda fn: [fn(j, blocks[j]) for j in range(NA_BLOCKS_PER_STEP)]
        qrows = each(lambda j, i: pl.ds(pl.multiple_of(i * qn, qn), qn))
        krows = each(lambda j, i: pl.ds(
            pl.multiple_of(jnp.clip(NA_Q_ROWS * (i - 1), 0, rows - NA_K_ROWS) * GRID_W, GRID_W), kn))
        kind = each(lambda j, i: jnp.where(i == 0, 0, jnp.where(i == nblk - 1, 2, 1)))

        def stacked_q(j, i):
            q = q_ref[qrows[j], :].astype(F32) * HD ** -0.5
            return jnp.concatenate([jnp.where(lane_lo, q, 0.0), jnp.where(lane_lo, 0.0, q)], axis=0).astype(BF16)

        q2 = each(stacked_q)
        s_win = each(lambda j, i: _bdot_nt(q2[j], k_ref[krows[j], :]) + bias_ref[kind[j]])
        s_ctx = each(lambda j, i: _bdot_nt(q2[j], k_ctx))
        m = each(lambda j, i: jnp.maximum(jnp.max(s_win[j], axis=-1, keepdims=True),
                                          jnp.max(s_ctx[j], axis=-1, keepdims=True)))
        p_win = each(lambda j, i: jnp.exp(s_win[j] - m[j]))
        p_ctx = each(lambda j, i: jnp.exp(s_ctx[j] - m[j]))
        den = each(lambda j, i: jnp.sum(p_win[j], axis=-1, keepdims=True) + jnp.sum(p_ctx[j], axis=-1, keepdims=True))
        out = each(lambda j, i: (_bdot(p_win[j], v_ref[krows[j], :]) + _bdot(p_ctx[j], v_ctx)) / den[j])
        for j in range(NA_BLOCKS_PER_STEP):
            gate = _silu(z_ref[qrows[j], :].astype(F32))
            o_ref[qrows[j], :] = (jnp.where(lane_lo, out[j][:qn], out[j][qn:]) * gate).astype(o_ref.dtype)
        return carry

    lax.fori_loop(0, nblk // NA_BLOCKS_PER_STEP, body, 0)


def _natten(u, l, col_q, col_k, col_v, col_z, bias):
    b, t, _ = u.shape
    lat = lambda col: pl.BlockSpec((None, l, LANES), lambda bi, p, col=col: (bi, 0, col // LANES + p))
    full = lambda col: pl.BlockSpec((None, t, LANES), lambda bi, p, col=col: (bi, 0, col // LANES + p))
    return pl.pallas_call(
        functools.partial(_na_kernel, l=l),
        grid=(b, PAIRS),
        in_specs=[lat(col_q), full(col_k), full(col_v), lat(col_z),
                  pl.BlockSpec((None,) + bias.shape[1:], lambda bi, p: (p, 0, 0, 0))],
        out_specs=pl.BlockSpec((None, l, LANES), lambda bi, p: (bi, 0, p)),
        out_shape=jax.ShapeDtypeStruct((b, l, W), BF16),
        compiler_params=_cparams("parallel", "parallel"),
        name="natten",
    )(u, u, u, u, bias)


def kernel(x, c, ctx, c_ctx, ada_w, ada_b, norm_w, final_norm_w, even_w_in, even_w_out, ret_decay, rw_mu, rw_w0, rw_w2, rw_a0, rw_a2, rw_kk, rw_ka, rw_rk, rw_ln_w, rw_ln_b, odd_w_in, odd_w_out, hy_conv_w, hy_conv_b, hy_w1, hy_b1, hy_f1, hy_w2, hy_b2, hy_f2, hy_w3, hy_bias, na_rpb):
    b, l, d = x.shape
    n_ctx = ctx.shape[1]
    t = l + n_ctx
    depth = ada_w.shape[0]
    assert depth == 2 and l % ROW_TILE == 0 and n_ctx % ROW_TILE == 0 and (l // GRID_W) >= NA_K_ROWS
    assert (l // GRID_W) % (NA_Q_ROWS * NA_BLOCKS_PER_STEP) == 0

    cond_rows = -(-(b + 1) // 8) * 8
    cond = jnp.concatenate([c, c_ctx[None, :], jnp.zeros((cond_rows - b - 1, d), F32)], axis=0)
    mod_all = _ada_mod(cond, ada_w, ada_b)
    mods = [jnp.stack([mod_all[i, :b].reshape(b, 3, d),
                       jnp.broadcast_to(mod_all[i, b].reshape(1, 3, d), (b, 3, d))], axis=1) for i in range(depth)]

    h = jnp.concatenate([x, ctx], axis=1)

    col_rz, col_wz, col_shift = 3 * W, 4 * W, 5 * W
    u0 = _norm_proj(h, norm_w[0], mods[0], even_w_in[0].astype(BF16), l)
    cos_t, sin_t = _rope_tables(l, t)
    ret_o = _retention(u0, cos_t, sin_t, ret_decay[0], l, 0, W, 2 * W, col_rz)
    rw_o = _rwkv(u0, l, col_shift, col_wz, rw_mu[0], rw_w0[0], rw_w2[0], rw_a0[0], rw_a2[0], rw_kk[0], rw_ka[0],
                 rw_rk[0], rw_ln_w[0], rw_ln_b[0])
    h = _out_proj(ret_o, rw_o, even_w_out[0].astype(BF16), h, mods[0], l)

    u1 = _norm_proj(h, norm_w[1], mods[1], odd_w_in[0].astype(BF16), l)
    f_blocks, ft_blocks = _dft_matrices(l)
    filt = _hy_filters(l, hy_w1[0], hy_b1[0], hy_f1[0], hy_w2[0], hy_b2[0], hy_f2[0], hy_w3[0])
    spectra = _hy_spectra(f_blocks, filt)
    hy_o = _hyena(u1, l, hy_conv_w[0], hy_conv_b[0], hy_bias[0], f_blocks, ft_blocks, spectra)
    na_o = _natten(u1, l, 4 * W, 5 * W, 6 * W, 7 * W, _na_bias_tables(na_rpb[0], l // GRID_W))
    return _out_proj(hy_o, na_o, odd_w_out[0].astype(BF16), h, mods[1], l, final_norm_w)
```

```python
import functools
import math

import jax
import jax.numpy as jnp
import numpy as np
from jax import lax
from jax.experimental import pallas as pl
from jax.experimental.pallas import tpu as pltpu

F32 = jnp.float32
BF16 = jnp.bfloat16

GRID_W = 64
NORM_EPS = 1e-6
ROPE_BASE = 10000.0
HEADS = 8
HD = 64
W = HEADS * HD
LORA = 64
RWKV_GN_EPS = 64e-5
HY_BANDS = 16
HY_TARGET = 1e-2
HY_FAST_PCT = 0.3
HY_SLOW_PCT = 1.5
NA_WIN_R = 8
NA_WIN_C = 16
LANES = 128
PAIRS = W // LANES
RET_CHUNK = 128
RW_CHUNK = 64
RW_SUB = 32
ROW_TILE = 256
HALO = 16
VMEM_LIMIT = 56 * 1024 * 1024


def _cparams(*sem):
    return pltpu.CompilerParams(dimension_semantics=sem, vmem_limit_bytes=VMEM_LIMIT)


def _bdot(a, b):
    return jnp.dot(a.astype(BF16), b.astype(BF16), preferred_element_type=F32)


def _bdot_nt(a, b):
    return lax.dot_general(a.astype(BF16), b.astype(BF16), (((1,), (1,)), ((), ())), preferred_element_type=F32)


def _bdot_tn(a, b):
    return lax.dot_general(a.astype(BF16), b.astype(BF16), (((0,), (0,)), ((), ())), preferred_element_type=F32)


def _split3(x):
    hi = x.astype(BF16)
    r1 = x - hi.astype(F32)
    mid = r1.astype(BF16)
    lo = (r1 - mid.astype(F32)).astype(BF16)
    return hi, mid, lo


def _dot_exact_lhs(m, x):
    mb = m.astype(BF16)
    hi, mid, lo = _split3(x)
    d = lambda p: jnp.dot(mb, p, preferred_element_type=F32)
    return d(hi) + d(mid) + d(lo)


def _dot_hi(a, b):
    ah = a.astype(BF16)
    al = (a - ah.astype(F32)).astype(BF16)
    bh = b.astype(BF16)
    bl = (b - bh.astype(F32)).astype(BF16)
    d = lambda p, q: jnp.dot(p, q, preferred_element_type=F32)
    return d(ah, bh) + d(ah, bl) + d(al, bh)


def _silu(x):
    return x * (1.0 / (1.0 + jnp.exp(-x)))


def _sigmoid(x):
    return 1.0 / (1.0 + jnp.exp(-x))


def _softplus(x):
    return jnp.maximum(x, 0.0) + jnp.log(1.0 + jnp.exp(-jnp.abs(x)))


def _lane_ids(shape):
    return lax.broadcasted_iota(jnp.int32, shape, len(shape) - 1)


def _row_ids(shape):
    return lax.broadcasted_iota(jnp.int32, shape, len(shape) - 2)


def _head_sum_mxu(x, ones_bd):
    hi = x.astype(BF16)
    lo = (x - hi.astype(F32)).astype(BF16)
    tiles = []
    for p in range(x.shape[-1] // LANES):
        sl = slice(p * LANES, (p + 1) * LANES)
        tiles.append(jnp.dot(hi[:, sl], ones_bd, preferred_element_type=F32)
                     + jnp.dot(lo[:, sl], ones_bd, preferred_element_type=F32))
    return tiles[0] if len(tiles) == 1 else jnp.concatenate(tiles, axis=-1)


def _with_neighbours(ref, cols, c0, c, n_rows, lat_rows):
    row = lax.broadcasted_iota(jnp.int32, (c, 1), 0)
    x = ref[pl.ds(c0, c), cols].astype(F32)
    p0 = pl.multiple_of(jnp.maximum(c0 - HALO, 0), HALO)
    n0 = pl.multiple_of(jnp.minimum(c0 + c, n_rows - HALO), HALO)
    prev_row = ref[pl.ds(p0, HALO), cols].astype(F32)[HALO - 1:HALO, :]
    next_row = ref[pl.ds(n0, HALO), cols].astype(F32)[0:1, :]
    has_prev = jnp.logical_and(c0 != 0, c0 != lat_rows)
    has_next = jnp.logical_and(c0 + c != lat_rows, c0 + c != n_rows)
    prev_row = jnp.where(has_prev, prev_row, 0.0)
    next_row = jnp.where(has_next, next_row, 0.0)
    prev = jnp.where(row == 0, prev_row, pltpu.roll(x, 1, 0))
    nxt = jnp.where(row == c - 1, next_row, pltpu.roll(x, c - 1, 0))
    return x, prev, nxt


def _scan_chunk_id(i, n_lat, n_ctx, reverse):
    if reverse:
        return jnp.where(i < n_ctx, n_lat + n_ctx - 1 - i, n_lat - 1 - (i - n_ctx))
    return jnp.where(i < n_ctx, n_lat + i, i - n_ctx)


def _ada_kernel(c_ref, w_ref, b_ref, o_ref):
    o_ref[...] = _dot_hi(_silu(c_ref[...]), w_ref[...]) + b_ref[...]


def _ada_mod(cond, ada_w, ada_b):
    depth, d, d3 = ada_w.shape
    rows = cond.shape[0]
    return pl.pallas_call(
        _ada_kernel,
        grid=(depth, d3 // d),
        in_specs=[
            pl.BlockSpec((rows, d), lambda i, j: (0, 0)),
            pl.BlockSpec((None, d, d), lambda i, j: (i, 0, j)),
            pl.BlockSpec((None, 1, d), lambda i, j: (i, 0, j)),
        ],
        out_specs=pl.BlockSpec((None, rows, d), lambda i, j: (i, 0, j)),
        out_shape=jax.ShapeDtypeStruct((depth, rows, d3), F32),
        compiler_params=_cparams("parallel", "parallel"),
        name="ada_mod",
    )(cond, ada_w, ada_b.reshape(depth, 1, d3))


def _token_rows(h, tm, lat_tiles):
    if isinstance(h, tuple):
        d = h[0].shape[-1]
        return [pl.BlockSpec((None, tm, d), lambda bi, i: (bi, jnp.minimum(i, lat_tiles - 1), 0)),
                pl.BlockSpec((None, tm, d), lambda bi, i: (bi, jnp.maximum(i - lat_tiles, 0), 0))], list(h)
    return [pl.BlockSpec((None, tm, h.shape[-1]), lambda bi, i: (bi, i, 0))], [h]


def _read_token_rows(refs, lat_tiles):
    if len(refs) == 2:
        return jnp.where(pl.program_id(1) >= lat_tiles, refs[1][...], refs[0][...])
    return refs[0][...]


def _norm_proj_kernel(*refs, n_src, lat_tiles):
    nw_ref, mod_ref, w_ref, o_ref = refs[n_src:]
    x = _read_token_rows(refs[:n_src], lat_tiles)
    y = x * lax.rsqrt(jnp.mean(x * x, axis=-1, keepdims=True) + NORM_EPS) * nw_ref[...]
    n = y * (1.0 + mod_ref[1:2, :]) + mod_ref[0:1, :]
    o_ref[...] = _bdot(n, w_ref[...]).astype(o_ref.dtype)


def _norm_proj(h, norm_w, mod, w, n_lat, t):
    d, n = w.shape
    b = mod.shape[0]
    tm = ROW_TILE
    lat_tiles = n_lat // tm
    src_specs, srcs = _token_rows(h, tm, lat_tiles)
    return pl.pallas_call(
        functools.partial(_norm_proj_kernel, n_src=len(srcs), lat_tiles=lat_tiles),
        grid=(b, t // tm),
        in_specs=src_specs + [
            pl.BlockSpec((1, d), lambda bi, i: (0, 0)),
            pl.BlockSpec((None, None, 3, d), lambda bi, i: (bi, (i >= lat_tiles).astype(jnp.int32), 0, 0)),
            pl.BlockSpec((d, n), lambda bi, i: (0, 0)),
        ],
        out_specs=pl.BlockSpec((None, tm, n), lambda bi, i: (bi, i, 0)),
        out_shape=jax.ShapeDtypeStruct((b, t, n), BF16),
        compiler_params=_cparams("parallel", "parallel"),
        name="norm_proj",
    )(*srcs, norm_w.reshape(1, d), mod, w)


def _ret_kernel(q_ref, k_ref, v_ref, z_ref, cos_ref, sin_ref, rd_ref, o_ref, y_acc, st_ref, dm_ref, qd_ref, kd_ref,
                *, n_lat, n_ctx):
    c = RET_CHUNK
    lane = _lane_ids((1, LANES))
    lane_lo = lane < HD
    rope_lo = (lane % HD) < (HD // 2)
    pos = lax.broadcasted_iota(jnp.int32, (c, 1), 0).astype(F32)
    ti = lax.broadcasted_iota(jnp.int32, (c, c), 0)
    si = lax.broadcasted_iota(jnp.int32, (c, c), 1)
    same_head = (_row_ids((LANES, LANES)) < HD) == (_lane_ids((LANES, LANES)) < HD)
    ones_bd = same_head.astype(BF16)
    inst = [(d, pr) for d in range(2) for pr in range(PAIRS)]
    sl = lambda pr: slice(pr * LANES, (pr + 1) * LANES)

    g_chunk = []
    for d, pr in inst:
        lg_a = -jnp.exp(rd_ref[d, 2 * pr])
        lg_b = -jnp.exp(rd_ref[d, 2 * pr + 1])
        lgv = jnp.where(lane_lo, lg_a, lg_b)
        diff = ((ti - si) if d == 0 else (si - ti)).astype(F32)
        keep = diff >= 0
        dmat = lambda lg: jnp.where(keep, jnp.exp(jnp.where(keep, diff, 0.0) * lg), 0.0)
        dm_ref[d, pr] = jnp.concatenate([dmat(lg_a), dmat(lg_b)], axis=0)
        qd_ref[d, :, sl(pr)] = jnp.exp(((pos + 1.0) if d == 0 else (c - pos)) * lgv)
        kd_ref[d, :, sl(pr)] = jnp.exp(((c - 1.0 - pos) if d == 0 else pos) * lgv)
        g_chunk.append(jnp.exp(c * lgv))
    y_acc[...] = jnp.zeros_like(y_acc)
    st_ref[...] = jnp.zeros_like(st_ref)

    def rope(x, rows):
        partner = jnp.where(rope_lo, pltpu.roll(x, LANES - HD // 2, 1), pltpu.roll(x, HD // 2, 1))
        return x * cos_ref[rows, :] + partner * sin_ref[rows, :]

    def stack2(x):
        return jnp.concatenate([jnp.where(lane_lo, x, 0.0), jnp.where(lane_lo, 0.0, x)], axis=0)

    def step(i, carry):
        rows = [pl.ds(pl.multiple_of(_scan_chunk_id(i, n_lat, n_ctx, d == 1) * c, c), c) for d in range(2)]
        each = lambda fn: [fn(j, *inst[j]) for j in range(len(inst))]
        q = each(lambda j, d, pr: rope(q_ref[rows[d], sl(pr)].astype(F32), rows[d]))
        k = each(lambda j, d, pr: rope(k_ref[rows[d], sl(pr)].astype(F32) * HD ** -0.5, rows[d]))
        v = each(lambda j, d, pr: v_ref[rows[d], sl(pr)])
        s0 = each(lambda j, d, pr: st_ref[d, pr])
        scores = each(lambda j, d, pr: _bdot_nt(stack2(q[j]), k[j]) * dm_ref[d, pr])
        cross = each(lambda j, d, pr: _bdot(q[j] * qd_ref[d, :, sl(pr)], s0[j]))
        upd = each(lambda j, d, pr: _bdot_tn(k[j] * kd_ref[d, :, sl(pr)], v[j]))
        intra = each(lambda j, d, pr: _bdot(scores[j], v[j]))
        for j, (d, pr) in enumerate(inst):
            st_ref[d, pr] = s0[j] * g_chunk[j] + jnp.where(same_head, upd[j], 0.0)
            y_acc[rows[d], sl(pr)] += jnp.where(lane_lo, intra[j][:c], intra[j][c:]) + cross[j]
        return carry

    lax.fori_loop(0, n_lat + n_ctx, step, 0)

    def finish(i, carry):
        rows = pl.ds(pl.multiple_of(i * ROW_TILE, ROW_TILE), ROW_TILE)
        y = y_acc[rows, :]
        ms = _head_sum_mxu(y * y, ones_bd) * (1.0 / HD)
        o_ref[rows, :] = (y * lax.rsqrt(ms + NORM_EPS) * _silu(z_ref[rows, :].astype(F32))).astype(o_ref.dtype)
        return carry

    lax.fori_loop(0, (n_lat + n_ctx) * c // ROW_TILE, finish, 0)


def _retention(u, cos_t, sin_t, ret_decay, n_lat_rows, col_q, col_k, col_v, col_z):
    b, t, _ = u.shape
    c = RET_CHUNK
    rd = jnp.broadcast_to(ret_decay.astype(F32)[:, :, None, None], (2, HEADS, 1, LANES))
    blk = lambda col: pl.BlockSpec((None, t, W), lambda bi, col=col: (bi, 0, col // W))
    return pl.pallas_call(
        functools.partial(_ret_kernel, n_lat=n_lat_rows // c, n_ctx=(t - n_lat_rows) // c),
        grid=(b,),
        in_specs=[
            blk(col_q), blk(col_k), blk(col_v), blk(col_z),
            pl.BlockSpec((t, LANES), lambda bi: (0, 0)),
            pl.BlockSpec((t, LANES), lambda bi: (0, 0)),
            pl.BlockSpec((2, HEADS, 1, LANES), lambda bi: (0, 0, 0, 0)),
        ],
        out_specs=pl.BlockSpec((None, t, W), lambda bi: (bi, 0, 0)),
        out_shape=jax.ShapeDtypeStruct((b, t, W), BF16),
        scratch_shapes=[pltpu.VMEM((t, W), F32), pltpu.VMEM((2, PAIRS, LANES, LANES), F32),
                        pltpu.VMEM((2, PAIRS, 2 * c, c), F32), pltpu.VMEM((2, c, W), F32),
                        pltpu.VMEM((2, c, W), F32)],
        compiler_params=_cparams("parallel"),
        name="retention",
    )(u, u, u, u, cos_t, sin_t, rd)


def _rope_tables(n_lat_rows, t):
    pos = jnp.arange(n_lat_rows)
    row = (pos // GRID_W).astype(F32)
    col = (pos % GRID_W).astype(F32)
    nf = HD // 4
    inv = ROPE_BASE ** (-jnp.arange(nf, dtype=F32) / nf)
    ang = jnp.concatenate([row[:, None] * inv, col[:, None] * inv], axis=-1)
    cos, sin = jnp.cos(ang), jnp.sin(ang)
    cos_h = jnp.concatenate([cos, cos], axis=-1)
    sin_h = jnp.concatenate([-sin, sin], axis=-1)
    reps = LANES // HD
    cos_t = jnp.concatenate([jnp.tile(cos_h, (1, reps)), jnp.ones((t - n_lat_rows, LANES), F32)], axis=0)
    sin_t = jnp.concatenate([jnp.tile(sin_h, (1, reps)), jnp.zeros((t - n_lat_rows, LANES), F32)], axis=0)
    return cos_t, sin_t


def _rwkv_kernel(r_ref, k_ref, v_ref, lo_ref, z_ref, mu_ref, mulo_ref, kkw_ref, ka_ref, w0_ref, a0_ref, w2_ref,
                 a2_ref, rk_ref, lnw_ref, lnb_ref, o_ref, y_acc, b_acc, st_ref, prep_ref, plast_ref, *, n_lat, n_ctx):
    c = RW_CHUNK
    c2 = 2 * c
    t_rows = (n_lat + n_ctx) * c
    lat_rows = n_lat * c
    lane_lo = _lane_ids((1, LANES)) < HD
    ri = _row_ids((c2, c2))
    ci = _lane_ids((c2, c2))
    same_head = (ri < c) == (ci < c)
    rt_, ct_ = ri % c, ci % c
    same_sub = (rt_ // RW_SUB) == (ct_ // RW_SUB)
    eye = (ri == ci).astype(F32)
    ones_bd = same_head.astype(BF16)
    t64 = lax.broadcasted_iota(jnp.int32, (c, c), 0)
    s64 = lax.broadcasted_iota(jnp.int32, (c, c), 1)
    masks = []
    for d in range(2):
        before = (ct_ < rt_) if d == 0 else (ct_ > rt_)
        strict = jnp.logical_and(same_head, before)
        incl = jnp.logical_and(same_head, jnp.logical_or(before, ct_ == rt_))
        tri = ((s64 <= t64) if d == 0 else (s64 >= t64)).astype(BF16)
        masks.append((strict, incl, jnp.logical_and(strict, same_sub), tri))

    y_acc[...] = jnp.zeros_like(y_acc)
    b_acc[...] = jnp.zeros_like(b_acc)
    st_ref[...] = jnp.zeros_like(st_ref)

    def stack2(x):
        return jnp.concatenate([jnp.where(lane_lo, x, 0.0), jnp.where(lane_lo, 0.0, x)], axis=0)

    def dup(x):
        return jnp.concatenate([x, x], axis=0)

    def unstack(x):
        return jnp.where(lane_lo, x[:c], x[c:])

    sh_t = lax.broadcasted_iota(jnp.int32, (c, c + 2 * HALO), 0) + HALO
    sh_s = lax.broadcasted_iota(jnp.int32, (c, c + 2 * HALO), 1)
    shift_base = jnp.where(jnp.abs(sh_s - sh_t) == 1, 0.5, 0.0)

    n_steps = n_lat + n_ctx
    fields = ("at", "bt", "kt", "rt", "v", "bonus")

    def step_rows(i, d):
        return pl.ds(pl.multiple_of(_scan_chunk_id(i, n_lat, n_ctx, d == 1) * c, c), c)

    def prepare(i, d):
        tri = masks[d][3]
        c0 = pl.multiple_of(_scan_chunk_id(i, n_lat, n_ctx, d == 1) * c, c)
        p0 = pl.multiple_of(jnp.maximum(c0 - HALO, 0), HALO)
        n0 = pl.multiple_of(jnp.minimum(c0 + c, t_rows - HALO), HALO)
        no_prev = jnp.logical_or(c0 == 0, c0 == lat_rows)
        no_next = jnp.logical_or(c0 + c == lat_rows, c0 + c == t_rows)
        dead_prev = jnp.where(no_prev, HALO - 1, -1)
        dead_next = jnp.where(no_next, HALO + c, -1)
        dead = jnp.logical_or(sh_s == dead_prev, sh_s == dead_next)
        shift = jnp.where(dead, 0.0, shift_base).astype(BF16)

        def mixed(ref, mu):
            x = ref[pl.ds(c0, c), :]
            ext = jnp.concatenate([ref[pl.ds(p0, HALO), :], x, ref[pl.ds(n0, HALO), :]], axis=0)
            x = x.astype(F32)
            return x + mu * (jnp.dot(shift, ext, preferred_element_type=F32) - x)

        r = mixed(r_ref, mu_ref[0])
        k = mixed(k_ref, mu_ref[1])
        v = mixed(v_ref, mu_ref[2])
        lo = mixed(lo_ref, mulo_ref[...])
        kk = k * kkw_ref[...]
        yield
        kk_norm = jnp.sqrt(_head_sum_mxu(kk * kk, ones_bd))
        w_pre = _bdot(jnp.tanh(lo[:, :LANES]), w2_ref[d])
        a_pre = _bdot(lo[:, LANES:], a2_ref[d])
        yield
        kk = kk / jnp.maximum(kk_norm, 1e-12)
        logw = -math.exp(-0.5) * _sigmoid(w0_ref[d] + w_pre)
        a = _sigmoid(a0_ref[d] + a_pre)
        k_d = k * (1.0 + (a - 1.0) * ka_ref[...])
        lw_hi = logw.astype(BF16)
        lw_lo = (logw - lw_hi.astype(F32)).astype(BF16)
        cum = jnp.dot(tri, lw_hi, preferred_element_type=F32) + jnp.dot(tri, lw_lo, preferred_element_type=F32)
        bonus_sum = jnp.concatenate(
            [_bdot((r * k_d * rk_ref[...])[:, pr * LANES:(pr + 1) * LANES], ones_bd) for pr in range(PAIRS)], axis=1)
        yield
        p = jnp.exp(cum)
        inv_p = jnp.exp(-cum)
        staged = dict(at=-kk * jnp.exp(cum - logw), bt=kk * a * inv_p, kt=k_d * inv_p, rt=r * p, v=v,
                      bonus=bonus_sum * v)
        yield "stores next"
        for f, name in enumerate(fields):
            prep_ref[d, f] = staged[name]
        plast_ref[d] = p[c - 1:c, :] if d == 0 else p[0:1, :]

    for d in range(2):
        for _ in prepare(0, d):
            pass

    def step(i, carry):
        rows = [step_rows(i, d) for d in range(2)]
        inst = [(d, pr) for d in range(2) for pr in range(PAIRS)]
        each = lambda fn: [fn(j) for j in range(len(inst))]
        sl = lambda j: slice(inst[j][1] * LANES, (inst[j][1] + 1) * LANES)
        op = lambda j, name: prep_ref[inst[j][0], fields.index(name), :, sl(j)]
        msk = lambda j: masks[inst[j][0]]
        p_last = [plast_ref[d] for d in range(2)]
        for d in range(2):
            b_acc[rows[d], :] += prep_ref[d, fields.index("bonus")]
        nxt = jnp.minimum(i + 1, n_steps - 1)
        staging = [[prepare(nxt, d), None] for d in range(2)]

        def advance_staging():
            for s in staging:
                if s[1] is None:
                    s[1] = next(s[0])

        big = each(lambda j: _bdot_nt(jnp.concatenate([stack2(op(j, "at")), stack2(op(j, "rt"))], axis=0),
                                      jnp.concatenate([dup(op(j, "bt")), dup(op(j, "kt"))], axis=0)))
        s0 = each(lambda j: st_ref[inst[j][0], inst[j][1]])
        proj = each(lambda j: _bdot_nt(jnp.concatenate([op(j, "at"), op(j, "rt")], axis=0), s0[j]))
        v2 = each(lambda j: dup(op(j, "v")))
        n_all = each(lambda j: jnp.where(msk(j)[0], big[j][:c2, :c2], 0.0))
        aak_v = each(lambda j: _bdot(jnp.where(msk(j)[0], big[j][:c2, c2:], 0.0), v2[j]))
        advance_staging()
        n_tr = each(lambda j: n_all[j].T)
        nsub_tr = each(lambda j: jnp.where(masks[1 - inst[j][0]][2], n_tr[j], 0.0))
        t_tr = each(lambda j: eye + nsub_tr[j])
        p_tr = each(lambda j: _bdot(nsub_tr[j], nsub_tr[j]))
        span = 2
        while 2 * span < RW_SUB:
            both = each(lambda j: _bdot(p_tr[j], jnp.concatenate([t_tr[j], p_tr[j]], axis=1)))
            t_tr = each(lambda j: t_tr[j] + both[j][:, :c2])
            p_tr = each(lambda j: both[j][:, c2:])
            span *= 2
            advance_staging()
        t_tr = each(lambda j: t_tr[j] + _bdot(p_tr[j], t_tr[j]))
        corr = each(lambda j: _bdot(n_tr[j] - nsub_tr[j], t_tr[j]))
        tinv_tr = each(lambda j: t_tr[j] + _bdot(t_tr[j], corr[j]))
        u_st = each(lambda j: _bdot_tn(tinv_tr[j], dup(proj[j][:c]) + aak_v[j]))
        y_st = each(lambda j: _bdot(
            jnp.concatenate([jnp.where(msk(j)[1], big[j][c2:, :c2], 0.0), jnp.where(msk(j)[1], big[j][c2:, c2:], 0.0)],
                            axis=1),
            jnp.concatenate([u_st[j], v2[j]], axis=0)))
        upd = each(lambda j: _bdot_tn(jnp.concatenate([unstack(u_st[j]), op(j, "v")], axis=0),
                                      jnp.concatenate([op(j, "bt"), op(j, "kt")], axis=0)))
        for j, (d, pr) in enumerate(inst):
            st_ref[d, pr] = (s0[j] + jnp.where(same_head, upd[j], 0.0)) * p_last[d][:, sl(j)]
            y_acc[rows[d], sl(j)] += proj[j][c:] + unstack(y_st[j])
        for g, _ in staging:
            for _ in g:
                pass
        return carry

    lax.fori_loop(0, n_steps, step, 0)

    def finish(i, carry):
        rows = pl.ds(pl.multiple_of(i * ROW_TILE, ROW_TILE), ROW_TILE)
        y = y_acc[rows, :]
        mean = _head_sum_mxu(y, ones_bd) * (1.0 / HD)
        yc = y - mean
        var = _head_sum_mxu(yc * yc, ones_bd) * (1.0 / HD)
        gn = yc * lax.rsqrt(var + RWKV_GN_EPS) * lnw_ref[...] + lnb_ref[...]
        o_ref[rows, :] = ((gn + b_acc[rows, :]) * _silu(z_ref[rows, :].astype(F32))).astype(o_ref.dtype)
        return carry

    lax.fori_loop(0, t_rows // ROW_TILE, finish, 0)


def _rwkv(u, n_lat_rows, col_shift, col_z, rw_mu, rw_w0, rw_w2, rw_a0, rw_a2, rw_kk, rw_ka, rw_rk, rw_ln_w,
          rw_ln_b):
    b, t, _ = u.shape
    c = RW_CHUNK
    assert c == HD
    col_lo = col_shift + 3 * W
    def pad_dirs(w2):
        z = jnp.zeros_like(w2[0])
        return jnp.stack([jnp.concatenate([w2[0], z], axis=0), jnp.concatenate([z, w2[1]], axis=0)]).astype(BF16)

    row = lambda a: a.reshape(1, W).astype(F32)
    blk = lambda col, width=W: pl.BlockSpec((None, t, width), lambda bi, col=col, width=width: (bi, 0, col // width))
    full = lambda shape: pl.BlockSpec(shape, lambda bi: (0,) * len(shape))
    return pl.pallas_call(
        functools.partial(_rwkv_kernel, n_lat=n_lat_rows // c, n_ctx=(t - n_lat_rows) // c),
        grid=(b,),
        in_specs=[
            blk(col_shift), blk(col_shift + W), blk(col_shift + 2 * W), blk(col_lo, 2 * LANES), blk(col_z),
            full((3, 1, W)), full((1, 2 * LANES)), full((1, W)), full((1, W)), full((2, 1, W)), full((2, 1, W)),
            full((2, 2 * LORA, W)), full((2, 2 * LORA, W)), full((1, W)), full((1, W)), full((1, W)),
        ],
        out_specs=pl.BlockSpec((None, t, W), lambda bi: (bi, 0, 0)),
        out_shape=jax.ShapeDtypeStruct((b, t, W), BF16),
        scratch_shapes=[pltpu.VMEM((t, W), F32), pltpu.VMEM((t, W), F32),
                        pltpu.VMEM((2, PAIRS, LANES, LANES), F32), pltpu.VMEM((2, 6, c, W), F32),
                        pltpu.VMEM((2, 1, W), F32)],
        compiler_params=_cparams("parallel"),
        name="rwkv7",
    )(u, u, u, u, u, rw_mu[:3 * W].reshape(3, 1, W), rw_mu[3 * W:].reshape(1, 2 * LANES), row(rw_kk), row(rw_ka),
      rw_w0.reshape(2, 1, W), rw_a0.reshape(2, 1, W), pad_dirs(rw_w2), pad_dirs(rw_a2), row(rw_rk), row(rw_ln_w),
      row(rw_ln_b))


def _out_proj_kernel(*refs, n_src, lat_tiles, final):
    a_ref, b_ref, w_ref, mod_ref = refs[n_src:n_src + 4]
    wa = a_ref.shape[-1]
    y = _bdot(a_ref[...], w_ref[:wa, :]) + _bdot(b_ref[...], w_ref[wa:, :])
    hn = _read_token_rows(refs[:n_src], lat_tiles) + mod_ref[2:3, :] * y
    if final:
        fnw_ref, o_ref = refs[n_src + 4:]
        o_ref[...] = hn * lax.rsqrt(jnp.mean(hn * hn, axis=-1, keepdims=True) + NORM_EPS) * fnw_ref[...]
    else:
        (o_ref,) = refs[n_src + 4:]
        o_ref[...] = hn


def _out_proj(a, b, w, h, mod, n_lat, final_norm_w=None):
    bsz, rows, wa = a.shape
    d = w.shape[1]
    tm = ROW_TILE
    lat_tiles = n_lat // tm
    final = final_norm_w is not None
    src_specs, srcs = _token_rows(h, tm, lat_tiles)
    in_specs = src_specs + [
        pl.BlockSpec((None, tm, wa), lambda bi, i: (bi, i, 0)),
        pl.BlockSpec((None, tm, b.shape[-1]), lambda bi, i: (bi, i, 0)),
        pl.BlockSpec(w.shape, lambda bi, i: (0, 0)),
        pl.BlockSpec((None, None, 3, d), lambda bi, i: (bi, (i >= lat_tiles).astype(jnp.int32), 0, 0)),
    ]
    args = srcs + [a, b, w, mod]
    if final:
        in_specs.append(pl.BlockSpec((1, d), lambda bi, i: (0, 0)))
        args.append(final_norm_w.reshape(1, d))
    return pl.pallas_call(
        functools.partial(_out_proj_kernel, n_src=len(srcs), lat_tiles=lat_tiles, final=final),
        grid=(bsz, rows // tm),
        in_specs=in_specs,
        out_specs=pl.BlockSpec((None, tm, d), lambda bi, i: (bi, i, 0)),
        out_shape=jax.ShapeDtypeStruct((bsz, rows, d), F32),
        compiler_params=_cparams("parallel", "parallel"),
        name="out_proj_final" if final else "out_proj",
    )(*args)


HY_FREQ_BLOCK = 256


def _dft_matrices(l):
    n = 2 * l
    k = jnp.arange(l, dtype=jnp.int32)[:, None]
    split = 64
    a = jnp.arange(l // split, dtype=jnp.int32)[None, :] * split
    bb = jnp.arange(split, dtype=jnp.int32)[None, :]
    ang = lambda m: (jnp.pi / n) * (((2 * k + 1) * m) % (2 * n)).astype(F32)
    ca, sa, cb, sb = jnp.cos(ang(a)), jnp.sin(ang(a)), jnp.cos(ang(bb)), jnp.sin(ang(bb))
    cos = (ca[:, :, None] * cb[:, None, :] - sa[:, :, None] * sb[:, None, :]).reshape(l, l)
    sin = (sa[:, :, None] * cb[:, None, :] + ca[:, :, None] * sb[:, None, :]).reshape(l, l)
    cat, sat, cbt, sbt = ca.T, sa.T, cb.T, sb.T
    cos_t = (cat[:, None, :] * cbt[None, :, :] - sat[:, None, :] * sbt[None, :, :]).reshape(l, l)
    sin_t = (sat[:, None, :] * cbt[None, :, :] + cat[:, None, :] * sbt[None, :, :]).reshape(l, l)
    f = jnp.concatenate([cos.astype(BF16), (-sin).astype(BF16)], axis=0)
    ft = jnp.concatenate([cos_t.astype(BF16), (-sin_t).astype(BF16)], axis=1)
    return f, ft


def _hy_filter_kernel(z_ref, w1_ref, b1_ref, f1_ref, w2_ref, b2_ref, f2_ref, w3_ref, dec_ref, o_ref, hid_ref):
    j = pl.program_id(0)

    @pl.when(j == 0)
    def _():
        hid = jnp.sin(f1_ref[...] * (_dot_hi(z_ref[...], w1_ref[...]) + b1_ref[...]))
        hid_ref[...] = jnp.sin(f2_ref[...] * (_dot_hi(hid, w2_ref[...]) + b2_ref[...]))

    h = _dot_hi(hid_ref[...], w3_ref[...]) * dec_ref[...]
    row = lax.broadcasted_iota(jnp.int32, h.shape, 0)
    o_ref[...] = jnp.where(jnp.logical_and(j % 2 == 1, row == 0), 0.0, h).astype(o_ref.dtype)


def _hy_filters(l, w1, b1, f1, w2, b2, f2, w3):
    t = jnp.linspace(0.0, 1.0, l)[:, None]
    bands = jnp.linspace(1e-4, HY_BANDS - 1, HY_BANDS)
    ang = (2.0 * math.pi / l) * jnp.arange(l, dtype=F32)[:, None] * bands[None]
    z = jnp.concatenate([t, jnp.cos(ang), -jnp.sin(ang)], axis=-1)
    emb = z.shape[1]
    emb_pad = 64
    z = jnp.pad(z, ((0, 0), (0, emb_pad - emb)))
    w1p = jnp.pad(w1, ((0, emb_pad - emb), (0, 0)))
    deltas = jnp.abs(jnp.linspace(math.log(HY_TARGET) / HY_SLOW_PCT, math.log(HY_TARGET) / HY_FAST_PCT, W))
    dec = jnp.exp(-t * deltas)
    ffn = w1.shape[1]
    nblk = w3.shape[1] // W
    full = lambda shape: pl.BlockSpec(shape, lambda j: (0,) * len(shape))
    return pl.pallas_call(
        _hy_filter_kernel,
        grid=(nblk,),
        in_specs=[full((l, emb_pad)), full((emb_pad, ffn)), full((1, ffn)), full((1, ffn)), full((ffn, ffn)),
                  full((1, ffn)), full((1, ffn)), pl.BlockSpec((ffn, W), lambda j: (0, j)), full((l, W))],
        out_specs=pl.BlockSpec((l, W), lambda j: (0, j)),
        out_shape=jax.ShapeDtypeStruct((l, nblk * W), F32),
        scratch_shapes=[pltpu.VMEM((l, ffn), F32)],
        compiler_params=_cparams("arbitrary"),
        name="hyena_filters",
    )(z, w1p, b1.reshape(1, -1), f1.reshape(1, -1), w2, b2.reshape(1, -1), f2.reshape(1, -1), w3, dec)


def _hy_spectrum_kernel(f_ref, hf_ref, hb_ref, o_ref, *, real_blocks):
    sign = jnp.where(pl.program_id(0) < real_blocks, 1.0, -1.0)
    o_ref[...] = _bdot(f_ref[...], hf_ref[...]) + sign * _bdot(f_ref[...], hb_ref[...])


def _hy_spectra(f, filt):
    l2, l = f.shape
    rb = 2 * HY_FREQ_BLOCK
    orders = filt.shape[1] // (2 * W)
    return pl.pallas_call(
        functools.partial(_hy_spectrum_kernel, real_blocks=l // rb),
        grid=(l2 // rb, orders),
        in_specs=[pl.BlockSpec((rb, l), lambda i, o: (i, 0)), pl.BlockSpec((l, W), lambda i, o: (0, 2 * o)),
                  pl.BlockSpec((l, W), lambda i, o: (0, 2 * o + 1))],
        out_specs=pl.BlockSpec((rb, W), lambda i, o: (i, o)),
        out_shape=jax.ShapeDtypeStruct((l2, orders * W), F32),
        compiler_params=_cparams("parallel", "parallel"),
        name="hyena_spectra",
    )(f, filt, filt)


def _hyena_kernel(v_ref, x1_ref, x2_ref, g_ref, cw_ref, cb_ref, bias_ref, fc_ref, fs_ref, ftc_ref, fts_ref, gr_ref,
                  gi_ref, o_ref, z_ref, zb_ref, acc_ref, *, l):
    o = pl.program_id(1)
    kb = pl.program_id(2)
    nkb = pl.num_programs(2)
    rc = ROW_TILE

    def conv_rows(ref, sec, c0):
        x, prev, nxt = _with_neighbours(ref, slice(None), c0, rc, l, l)
        w = cw_ref[sec]
        return prev * w[0:1, :] + x * w[1:2, :] + nxt * w[2:3, :] + cb_ref[sec]

    @pl.when(jnp.logical_and(o == 0, kb == 0))
    def _():
        def body(i, carry):
            c0 = pl.multiple_of(i * rc, rc)
            zc = conv_rows(v_ref, 0, c0)
            z_ref[pl.ds(c0, rc), :] = zc
            zb_ref[pl.ds(c0, rc), :] = zc.astype(BF16)
            return carry

        lax.fori_loop(0, l // rc, body, 0)

    @pl.when(kb == 0)
    def _():
        acc_ref[...] = jnp.zeros_like(acc_ref)

    xr = jnp.dot(fc_ref[...], zb_ref[...], preferred_element_type=F32)
    xi = jnp.dot(fs_ref[...], zb_ref[...], preferred_element_type=F32)
    gr, gi = gr_ref[...], gi_ref[...]
    yr = (xr * gr - xi * gi).astype(BF16)
    yi = (xr * gi + xi * gr).astype(BF16)
    acc_ref[...] += (jnp.dot(ftc_ref[...], yr, preferred_element_type=F32)
                     + jnp.dot(fts_ref[...], yi, preferred_element_type=F32))

    @pl.when(kb == nkb - 1)
    def _():
        def body(i, carry):
            c0 = pl.multiple_of(i * rc, rc)
            rows = pl.ds(c0, rc)
            conv = acc_ref[rows, :] * (1.0 / l) + z_ref[rows, :] * bias_ref[o]

            @pl.when(o == 0)
            def _():
                zn = conv_rows(x1_ref, 1, c0) * conv
                z_ref[rows, :] = zn
                zb_ref[rows, :] = zn.astype(BF16)

            @pl.when(o == 1)
            def _():
                zn = conv_rows(x2_ref, 2, c0) * conv
                o_ref[rows, :] = (zn * _silu(g_ref[rows, :].astype(F32))).astype(o_ref.dtype)

            return carry

        lax.fori_loop(0, l // rc, body, 0)


def _hyena(u, l, conv_w, conv_b, hy_bias, f, ft, spectra):
    b = u.shape[0]
    fb = HY_FREQ_BLOCK
    nkb = l // fb
    sec = lambda s: pl.BlockSpec((None, l, W), lambda bi, o, kb, s=s: (bi, 0, s))
    cw = jnp.transpose(conv_w.reshape(3, 3, W), (1, 0, 2))
    f_rows = lambda part: pl.BlockSpec((fb, l), lambda bi, o, kb, part=part: (part * nkb + kb, 0))
    ft_cols = lambda part: pl.BlockSpec((l, fb), lambda bi, o, kb, part=part: (0, part * nkb + kb))
    g_rows = lambda part: pl.BlockSpec((fb, W), lambda bi, o, kb, part=part: (part * nkb + kb, o))
    return pl.pallas_call(
        functools.partial(_hyena_kernel, l=l),
        grid=(b, 2, nkb),
        in_specs=[
            sec(0), sec(1), sec(2), sec(3),
            pl.BlockSpec((3, 3, W), lambda bi, o, kb: (0, 0, 0)),
            pl.BlockSpec((3, 1, W), lambda bi, o, kb: (0, 0, 0)),
            pl.BlockSpec((2, 1, W), lambda bi, o, kb: (0, 0, 0)),
            f_rows(0), f_rows(1), ft_cols(0), ft_cols(1), g_rows(0), g_rows(1),
        ],
        out_specs=pl.BlockSpec((None, l, W), lambda bi, o, kb: (bi, 0, 0)),
        out_shape=jax.ShapeDtypeStruct((b, l, W), BF16),
        scratch_shapes=[pltpu.VMEM((l, W), F32), pltpu.VMEM((l, W), BF16), pltpu.VMEM((l, W), F32)],
        compiler_params=_cparams("parallel", "arbitrary", "arbitrary"),
        name="hyena",
    )(u, u, u, u, cw, conv_b.reshape(3, 1, W), hy_bias.reshape(2, 1, W), f, f, ft, ft, spectra, spectra)


NA_Q_ROWS = 4
NA_K_ROWS = 12
NA_BLOCKS_PER_STEP = 2
NEG_BIG = -1e30


def _na_bias_tables(rpb, rows):
    nblk = rows // NA_Q_ROWS
    heads = rpb.shape[0]
    n_dr, n_dc = 2 * NA_WIN_R - 1, 2 * NA_WIN_C - 1
    cq = np.arange(GRID_W)[:, None]
    kc = np.arange(GRID_W)[None, :]
    c0 = np.clip(cq - NA_WIN_C // 2, 0, GRID_W - NA_WIN_C)
    col_ok = (kc >= c0) & (kc < c0 + NA_WIN_C)
    col_sel = (np.arange(n_dc)[:, None, None] == (kc - cq + NA_WIN_C - 1)[None]).astype(np.float32)
    row_sel = np.zeros((3, NA_Q_ROWS, NA_K_ROWS, n_dr), np.float32)
    for kind, i in enumerate((0, 1, nblk - 1)):
        ks = min(max(NA_Q_ROWS * (i - 1), 0), rows - NA_K_ROWS)
        for qr in range(NA_Q_ROWS):
            rq = NA_Q_ROWS * i + qr
            r0 = min(max(rq - NA_WIN_R // 2, 0), rows - NA_WIN_R)
            for w in range(NA_K_ROWS):
                if r0 <= ks + w < r0 + NA_WIN_R:
                    row_sel[kind, qr, w, ks + w - rq + NA_WIN_R - 1] = 1.0
    tiles = jnp.einsum("phrj,jcx->phrcx", rpb.astype(F32).reshape(heads // 2, 2, n_dr, n_dc), col_sel,
                       precision=lax.Precision.HIGHEST)
    tab = jnp.einsum("kqwr,phrcx->pkhqcwx", row_sel, tiles, precision=lax.Precision.HIGHEST)
    ok = (row_sel.sum(-1) > 0)[None, :, None, :, None, :, None] & col_ok[None, None, None, None, :, None, :]
    tab = jnp.where(ok, tab, NEG_BIG)
    return tab.reshape(heads // 2, 3, 2 * NA_Q_ROWS * GRID_W, NA_K_ROWS * GRID_W)


def _na_kernel(q_ref, k_ref, v_ref, z_ref, bias_ref, o_ref, *, l):
    rows = l // GRID_W
    nblk = rows // NA_Q_ROWS
    qn = NA_Q_ROWS * GRID_W
    kn = NA_K_ROWS * GRID_W
    n_ctx = k_ref.shape[0] - l
    lane_lo = _lane_ids((1, LANES)) < HD
    k_ctx = k_ref[pl.ds(l, n_ctx), :]
    v_ctx = v_ref[pl.ds(l, n_ctx), :]

    def body(it, carry):
        blocks = [it * NA_BLOCKS_PER_STEP + j for j in range(NA_BLOCKS_PER_STEP)]
        each = lambda fn: [fn(j, blocks[j]) for j in range(NA_BLOCKS_PER_STEP)]
        qrows = each(lambda j, i: pl.ds(pl.multiple_of(i * qn, qn), qn))
        krows = each(lambda j, i: pl.ds(
            pl.multiple_of(jnp.clip(NA_Q_ROWS * (i - 1), 0, rows - NA_K_ROWS) * GRID_W, GRID_W), kn))
        kind = each(lambda j, i: jnp.where(i == 0, 0, jnp.where(i == nblk - 1, 2, 1)))

        def stacked_q(j, i):
            q = q_ref[qrows[j], :].astype(F32) * HD ** -0.5
            return jnp.concatenate([jnp.where(lane_lo, q, 0.0), jnp.where(lane_lo, 0.0, q)], axis=0).astype(BF16)

        q2 = each(stacked_q)
        s_win = each(lambda j, i: _bdot_nt(q2[j], k_ref[krows[j], :]) + bias_ref[kind[j]])
        s_ctx = each(lambda j, i: _bdot_nt(q2[j], k_ctx))
        m = each(lambda j, i: jnp.maximum(jnp.max(s_win[j], axis=-1, keepdims=True),
                                          jnp.max(s_ctx[j], axis=-1, keepdims=True)))
        p_win = each(lambda j, i: jnp.exp(s_win[j] - m[j]))
        p_ctx = each(lambda j, i: jnp.exp(s_ctx[j] - m[j]))
        den = each(lambda j, i: jnp.sum(p_win[j], axis=-1, keepdims=True) + jnp.sum(p_ctx[j], axis=-1, keepdims=True))
        out = each(lambda j, i: (_bdot(p_win[j], v_ref[krows[j], :]) + _bdot(p_ctx[j], v_ctx)) / den[j])
        for j in range(NA_BLOCKS_PER_STEP):
            gate = _silu(z_ref[qrows[j], :].astype(F32))
            o_ref[qrows[j], :] = (jnp.where(lane_lo, out[j][:qn], out[j][qn:]) * gate).astype(o_ref.dtype)
        return carry

    lax.fori_loop(0, nblk // NA_BLOCKS_PER_STEP, body, 0)


def _natten(u, l, col_q, col_k, col_v, col_z, bias):
    b, t, _ = u.shape
    lat = lambda col: pl.BlockSpec((None, l, LANES), lambda bi, p, col=col: (bi, 0, col // LANES + p))
    full = lambda col: pl.BlockSpec((None, t, LANES), lambda bi, p, col=col: (bi, 0, col // LANES + p))
    return pl.pallas_call(
        functools.partial(_na_kernel, l=l),
        grid=(b, PAIRS),
        in_specs=[lat(col_q), full(col_k), full(col_v), lat(col_z),
                  pl.BlockSpec((None,) + bias.shape[1:], lambda bi, p: (p, 0, 0, 0))],
        out_specs=pl.BlockSpec((None, l, LANES), lambda bi, p: (bi, 0, p)),
        out_shape=jax.ShapeDtypeStruct((b, l, W), BF16),
        compiler_params=_cparams("parallel", "parallel"),
        name="natten",
    )(u, u, u, u, bias)


def kernel(x, c, ctx, c_ctx, ada_w, ada_b, norm_w, final_norm_w, even_w_in, even_w_out, ret_decay, rw_mu, rw_w0, rw_w2, rw_a0, rw_a2, rw_kk, rw_ka, rw_rk, rw_ln_w, rw_ln_b, odd_w_in, odd_w_out, hy_conv_w, hy_conv_b, hy_w1, hy_b1, hy_f1, hy_w2, hy_b2, hy_f2, hy_w3, hy_bias, na_rpb):
    b, l, d = x.shape
    n_ctx = ctx.shape[1]
    t = l + n_ctx
    depth = ada_w.shape[0]
    assert depth == 2 and l % ROW_TILE == 0 and n_ctx % ROW_TILE == 0 and (l // GRID_W) >= NA_K_ROWS
    assert (l // GRID_W) % (NA_Q_ROWS * NA_BLOCKS_PER_STEP) == 0

    cond_rows = -(-(b + 1) // 8) * 8
    cond = jnp.concatenate([c, c_ctx[None, :], jnp.zeros((cond_rows - b - 1, d), F32)], axis=0)
    mod_all = _ada_mod(cond, ada_w, ada_b)
    mods = [jnp.stack([mod_all[i, :b].reshape(b, 3, d),
                       jnp.broadcast_to(mod_all[i, b].reshape(1, 3, d), (b, 3, d))], axis=1) for i in range(depth)]

    h = (x, ctx)

    col_rz, col_wz, col_shift = 3 * W, 4 * W, 5 * W
    u0 = _norm_proj(h, norm_w[0], mods[0], even_w_in[0].astype(BF16), l, t)
    cos_t, sin_t = _rope_tables(l, t)
    ret_o = _retention(u0, cos_t, sin_t, ret_decay[0], l, 0, W, 2 * W, col_rz)
    rw_o = _rwkv(u0, l, col_shift, col_wz, rw_mu[0], rw_w0[0], rw_w2[0], rw_a0[0], rw_a2[0], rw_kk[0], rw_ka[0],
                 rw_rk[0], rw_ln_w[0], rw_ln_b[0])
    h = _out_proj(ret_o, rw_o, even_w_out[0].astype(BF16), h, mods[0], l)

    u1 = _norm_proj(h, norm_w[1], mods[1], odd_w_in[0].astype(BF16), l, t)
    f, ft = _dft_matrices(l)
    filt = _hy_filters(l, hy_w1[0], hy_b1[0], hy_f1[0], hy_w2[0], hy_b2[0], hy_f2[0], hy_w3[0])
    spectra = _hy_spectra(f, filt)
    hy_o = _hyena(u1, l, hy_conv_w[0], hy_conv_b[0], hy_bias[0], f, ft, spectra)
    na_o = _natten(u1, l, 4 * W, 5 * W, 6 * W, 7 * W, _na_bias_tables(na_rpb[0], l // GRID_W))
    return _out_proj(hy_o, na_o, odd_w_out[0].astype(BF16), h, mods[1], l, final_norm_w)
```

```python
import functools
import math

import jax
import jax.numpy as jnp
import numpy as np
from jax import lax
from jax.experimental import pallas as pl
from jax.experimental.pallas import tpu as pltpu

F32 = jnp.float32
BF16 = jnp.bfloat16

GRID_W = 64
NORM_EPS = 1e-6
ROPE_BASE = 10000.0
HEADS = 8
HD = 64
W = HEADS * HD
LORA = 64
RWKV_GN_EPS = 64e-5
HY_BANDS = 16
HY_TARGET = 1e-2
HY_FAST_PCT = 0.3
HY_SLOW_PCT = 1.5
NA_WIN_R = 8
NA_WIN_C = 16
LANES = 128
PAIRS = W // LANES
RET_CHUNK = 128
RW_CHUNK = 64
RW_SUB = 32
ROW_TILE = 256
HALO = 16
VMEM_LIMIT = 56 * 1024 * 1024


def _cparams(*sem):
    return pltpu.CompilerParams(dimension_semantics=sem, vmem_limit_bytes=VMEM_LIMIT)


def _bdot(a, b):
    return jnp.dot(a.astype(BF16), b.astype(BF16), preferred_element_type=F32)


def _bdot_nt(a, b):
    return lax.dot_general(a.astype(BF16), b.astype(BF16), (((1,), (1,)), ((), ())), preferred_element_type=F32)


def _bdot_tn(a, b):
    return lax.dot_general(a.astype(BF16), b.astype(BF16), (((0,), (0,)), ((), ())), preferred_element_type=F32)


def _split3(x):
    hi = x.astype(BF16)
    r1 = x - hi.astype(F32)
    mid = r1.astype(BF16)
    lo = (r1 - mid.astype(F32)).astype(BF16)
    return hi, mid, lo


def _dot_exact_lhs(m, x):
    mb = m.astype(BF16)
    hi, mid, lo = _split3(x)
    d = lambda p: jnp.dot(mb, p, preferred_element_type=F32)
    return d(hi) + d(mid) + d(lo)


def _dot_hi(a, b):
    ah = a.astype(BF16)
    al = (a - ah.astype(F32)).astype(BF16)
    bh = b.astype(BF16)
    bl = (b - bh.astype(F32)).astype(BF16)
    d = lambda p, q: jnp.dot(p, q, preferred_element_type=F32)
    return d(ah, bh) + d(ah, bl) + d(al, bh)


def _silu(x):
    return x * (1.0 / (1.0 + jnp.exp(-x)))


def _sigmoid(x):
    return 1.0 / (1.0 + jnp.exp(-x))


def _softplus(x):
    return jnp.maximum(x, 0.0) + jnp.log(1.0 + jnp.exp(-jnp.abs(x)))


def _lane_ids(shape):
    return lax.broadcasted_iota(jnp.int32, shape, len(shape) - 1)


def _row_ids(shape):
    return lax.broadcasted_iota(jnp.int32, shape, len(shape) - 2)


def _head_sum_mxu(x, ones_bd):
    hi = x.astype(BF16)
    lo = (x - hi.astype(F32)).astype(BF16)
    tiles = []
    for p in range(x.shape[-1] // LANES):
        sl = slice(p * LANES, (p + 1) * LANES)
        tiles.append(jnp.dot(hi[:, sl], ones_bd, preferred_element_type=F32)
                     + jnp.dot(lo[:, sl], ones_bd, preferred_element_type=F32))
    return tiles[0] if len(tiles) == 1 else jnp.concatenate(tiles, axis=-1)


def _with_neighbours(ref, cols, c0, c, n_rows, lat_rows):
    row = lax.broadcasted_iota(jnp.int32, (c, 1), 0)
    x = ref[pl.ds(c0, c), cols].astype(F32)
    p0 = pl.multiple_of(jnp.maximum(c0 - HALO, 0), HALO)
    n0 = pl.multiple_of(jnp.minimum(c0 + c, n_rows - HALO), HALO)
    prev_row = ref[pl.ds(p0, HALO), cols].astype(F32)[HALO - 1:HALO, :]
    next_row = ref[pl.ds(n0, HALO), cols].astype(F32)[0:1, :]
    has_prev = jnp.logical_and(c0 != 0, c0 != lat_rows)
    has_next = jnp.logical_and(c0 + c != lat_rows, c0 + c != n_rows)
    prev_row = jnp.where(has_prev, prev_row, 0.0)
    next_row = jnp.where(has_next, next_row, 0.0)
    prev = jnp.where(row == 0, prev_row, pltpu.roll(x, 1, 0))
    nxt = jnp.where(row == c - 1, next_row, pltpu.roll(x, c - 1, 0))
    return x, prev, nxt


def _scan_chunk_id(i, n_lat, n_ctx, reverse):
    if reverse:
        return jnp.where(i < n_ctx, n_lat + n_ctx - 1 - i, n_lat - 1 - (i - n_ctx))
    return jnp.where(i < n_ctx, n_lat + i, i - n_ctx)


def _ada_kernel(c_ref, w_ref, b_ref, o_ref):
    o_ref[...] = _dot_hi(_silu(c_ref[...]), w_ref[...]) + b_ref[...]


def _ada_mod(cond, ada_w, ada_b):
    depth, d, d3 = ada_w.shape
    rows = cond.shape[0]
    return pl.pallas_call(
        _ada_kernel,
        grid=(depth, d3 // d),
        in_specs=[
            pl.BlockSpec((rows, d), lambda i, j: (0, 0)),
            pl.BlockSpec((None, d, d), lambda i, j: (i, 0, j)),
            pl.BlockSpec((None, 1, d), lambda i, j: (i, 0, j)),
        ],
        out_specs=pl.BlockSpec((None, rows, d), lambda i, j: (i, 0, j)),
        out_shape=jax.ShapeDtypeStruct((depth, rows, d3), F32),
        compiler_params=_cparams("parallel", "parallel"),
        name="ada_mod",
    )(cond, ada_w, ada_b.reshape(depth, 1, d3))


def _token_rows(h, tm, lat_tiles):
    if isinstance(h, tuple):
        d = h[0].shape[-1]
        return [pl.BlockSpec((None, tm, d), lambda bi, i: (bi, jnp.minimum(i, lat_tiles - 1), 0)),
                pl.BlockSpec((None, tm, d), lambda bi, i: (bi, jnp.maximum(i - lat_tiles, 0), 0))], list(h)
    return [pl.BlockSpec((None, tm, h.shape[-1]), lambda bi, i: (bi, i, 0))], [h]


def _read_token_rows(refs, lat_tiles):
    if len(refs) == 2:
        return jnp.where(pl.program_id(1) >= lat_tiles, refs[1][...], refs[0][...])
    return refs[0][...]


def _norm_proj_kernel(*refs, n_src, lat_tiles):
    nw_ref, mod_ref, w_ref, o_ref = refs[n_src:]
    x = _read_token_rows(refs[:n_src], lat_tiles)
    y = x * lax.rsqrt(jnp.mean(x * x, axis=-1, keepdims=True) + NORM_EPS) * nw_ref[...]
    n = y * (1.0 + mod_ref[1:2, :]) + mod_ref[0:1, :]
    o_ref[...] = _bdot(n, w_ref[...]).astype(o_ref.dtype)


def _norm_proj(h, norm_w, mod, w, n_lat, t):
    d, n = w.shape
    b = mod.shape[0]
    tm = ROW_TILE
    lat_tiles = n_lat // tm
    src_specs, srcs = _token_rows(h, tm, lat_tiles)
    return pl.pallas_call(
        functools.partial(_norm_proj_kernel, n_src=len(srcs), lat_tiles=lat_tiles),
        grid=(b, t // tm),
        in_specs=src_specs + [
            pl.BlockSpec((1, d), lambda bi, i: (0, 0)),
            pl.BlockSpec((None, None, 3, d), lambda bi, i: (bi, (i >= lat_tiles).astype(jnp.int32), 0, 0)),
            pl.BlockSpec((d, n), lambda bi, i: (0, 0)),
        ],
        out_specs=pl.BlockSpec((None, tm, n), lambda bi, i: (bi, i, 0)),
        out_shape=jax.ShapeDtypeStruct((b, t, n), BF16),
        compiler_params=_cparams("parallel", "parallel"),
        name="norm_proj",
    )(*srcs, norm_w.reshape(1, d), mod, w)


def _ret_kernel(q_ref, k_ref, v_ref, z_ref, cos_ref, sin_ref, rd_ref, o_ref, y_acc, st_ref, dm_ref, qd_ref, kd_ref,
                *, n_lat, n_ctx):
    c = RET_CHUNK
    lane = _lane_ids((1, LANES))
    lane_lo = lane < HD
    rope_lo = (lane % HD) < (HD // 2)
    pos = lax.broadcasted_iota(jnp.int32, (c, 1), 0).astype(F32)
    ti = lax.broadcasted_iota(jnp.int32, (c, c), 0)
    si = lax.broadcasted_iota(jnp.int32, (c, c), 1)
    same_head = (_row_ids((LANES, LANES)) < HD) == (_lane_ids((LANES, LANES)) < HD)
    ones_bd = same_head.astype(BF16)
    inst = [(d, pr) for d in range(2) for pr in range(PAIRS)]
    sl = lambda pr: slice(pr * LANES, (pr + 1) * LANES)

    g_chunk = []
    for d, pr in inst:
        lg_a = -jnp.exp(rd_ref[d, 2 * pr])
        lg_b = -jnp.exp(rd_ref[d, 2 * pr + 1])
        lgv = jnp.where(lane_lo, lg_a, lg_b)
        diff = ((ti - si) if d == 0 else (si - ti)).astype(F32)
        keep = diff >= 0
        dmat = lambda lg: jnp.where(keep, jnp.exp(jnp.where(keep, diff, 0.0) * lg), 0.0)
        dm_ref[d, pr] = jnp.concatenate([dmat(lg_a), dmat(lg_b)], axis=0)
        qd_ref[d, :, sl(pr)] = jnp.exp(((pos + 1.0) if d == 0 else (c - pos)) * lgv)
        kd_ref[d, :, sl(pr)] = jnp.exp(((c - 1.0 - pos) if d == 0 else pos) * lgv)
        g_chunk.append(jnp.exp(c * lgv))
    y_acc[...] = jnp.zeros_like(y_acc)
    st_ref[...] = jnp.zeros_like(st_ref)

    def rope(x, rows):
        partner = jnp.where(rope_lo, pltpu.roll(x, LANES - HD // 2, 1), pltpu.roll(x, HD // 2, 1))
        return x * cos_ref[rows, :] + partner * sin_ref[rows, :]

    def stack2(x):
        return jnp.concatenate([jnp.where(lane_lo, x, 0.0), jnp.where(lane_lo, 0.0, x)], axis=0)

    def step(i, carry):
        rows = [pl.ds(pl.multiple_of(_scan_chunk_id(i, n_lat, n_ctx, d == 1) * c, c), c) for d in range(2)]
        each = lambda fn: [fn(j, *inst[j]) for j in range(len(inst))]
        q = each(lambda j, d, pr: rope(q_ref[rows[d], sl(pr)].astype(F32), rows[d]))
        k = each(lambda j, d, pr: rope(k_ref[rows[d], sl(pr)].astype(F32) * HD ** -0.5, rows[d]))
        v = each(lambda j, d, pr: v_ref[rows[d], sl(pr)])
        s0 = each(lambda j, d, pr: st_ref[d, pr])
        scores = each(lambda j, d, pr: _bdot_nt(stack2(q[j]), k[j]) * dm_ref[d, pr])
        cross = each(lambda j, d, pr: _bdot(q[j] * qd_ref[d, :, sl(pr)], s0[j]))
        upd = each(lambda j, d, pr: _bdot_tn(k[j] * kd_ref[d, :, sl(pr)], v[j]))
        intra = each(lambda j, d, pr: _bdot(scores[j], v[j]))
        for j, (d, pr) in enumerate(inst):
            st_ref[d, pr] = s0[j] * g_chunk[j] + jnp.where(same_head, upd[j], 0.0)
            y_acc[rows[d], sl(pr)] += jnp.where(lane_lo, intra[j][:c], intra[j][c:]) + cross[j]
        return carry

    lax.fori_loop(0, n_lat + n_ctx, step, 0)

    def finish(i, carry):
        rows = pl.ds(pl.multiple_of(i * ROW_TILE, ROW_TILE), ROW_TILE)
        y = y_acc[rows, :]
        ms = _head_sum_mxu(y * y, ones_bd) * (1.0 / HD)
        o_ref[rows, :] = (y * lax.rsqrt(ms + NORM_EPS) * _silu(z_ref[rows, :].astype(F32))).astype(o_ref.dtype)
        return carry

    lax.fori_loop(0, (n_lat + n_ctx) * c // ROW_TILE, finish, 0)


def _retention(u, cos_t, sin_t, ret_decay, n_lat_rows, col_q, col_k, col_v, col_z):
    b, t, _ = u.shape
    c = RET_CHUNK
    rd = jnp.broadcast_to(ret_decay.astype(F32)[:, :, None, None], (2, HEADS, 1, LANES))
    blk = lambda col: pl.BlockSpec((None, t, W), lambda bi, col=col: (bi, 0, col // W))
    return pl.pallas_call(
        functools.partial(_ret_kernel, n_lat=n_lat_rows // c, n_ctx=(t - n_lat_rows) // c),
        grid=(b,),
        in_specs=[
            blk(col_q), blk(col_k), blk(col_v), blk(col_z),
            pl.BlockSpec((t, LANES), lambda bi: (0, 0)),
            pl.BlockSpec((t, LANES), lambda bi: (0, 0)),
            pl.BlockSpec((2, HEADS, 1, LANES), lambda bi: (0, 0, 0, 0)),
        ],
        out_specs=pl.BlockSpec((None, t, W), lambda bi: (bi, 0, 0)),
        out_shape=jax.ShapeDtypeStruct((b, t, W), BF16),
        scratch_shapes=[pltpu.VMEM((t, W), F32), pltpu.VMEM((2, PAIRS, LANES, LANES), F32),
                        pltpu.VMEM((2, PAIRS, 2 * c, c), F32), pltpu.VMEM((2, c, W), F32),
                        pltpu.VMEM((2, c, W), F32)],
        compiler_params=_cparams("parallel"),
        name="retention",
    )(u, u, u, u, cos_t, sin_t, rd)


def _rope_tables(n_lat_rows, t):
    pos = jnp.arange(n_lat_rows)
    row = (pos // GRID_W).astype(F32)
    col = (pos % GRID_W).astype(F32)
    nf = HD // 4
    inv = ROPE_BASE ** (-jnp.arange(nf, dtype=F32) / nf)
    ang = jnp.concatenate([row[:, None] * inv, col[:, None] * inv], axis=-1)
    cos, sin = jnp.cos(ang), jnp.sin(ang)
    cos_h = jnp.concatenate([cos, cos], axis=-1)
    sin_h = jnp.concatenate([-sin, sin], axis=-1)
    reps = LANES // HD
    cos_t = jnp.concatenate([jnp.tile(cos_h, (1, reps)), jnp.ones((t - n_lat_rows, LANES), F32)], axis=0)
    sin_t = jnp.concatenate([jnp.tile(sin_h, (1, reps)), jnp.zeros((t - n_lat_rows, LANES), F32)], axis=0)
    return cos_t, sin_t


def _rwkv_kernel(r_ref, k_ref, v_ref, lo_ref, z_ref, mu_ref, mulo_ref, kkw_ref, ka_ref, w0_ref, a0_ref, w2_ref,
                 a2_ref, rk_ref, lnw_ref, lnb_ref, o_ref, y_acc, b_acc, st_ref, prep_ref, plast_ref, *, n_lat, n_ctx):
    c = RW_CHUNK
    c2 = 2 * c
    t_rows = (n_lat + n_ctx) * c
    lat_rows = n_lat * c
    lane_lo = _lane_ids((1, LANES)) < HD
    ri = _row_ids((c2, c2))
    ci = _lane_ids((c2, c2))
    same_head = (ri < c) == (ci < c)
    rt_, ct_ = ri % c, ci % c
    same_sub = (rt_ // RW_SUB) == (ct_ // RW_SUB)
    eye = (ri == ci).astype(F32)
    ones_bd = same_head.astype(BF16)
    t64 = lax.broadcasted_iota(jnp.int32, (c, c), 0)
    s64 = lax.broadcasted_iota(jnp.int32, (c, c), 1)
    masks = []
    for d in range(2):
        before = (ct_ < rt_) if d == 0 else (ct_ > rt_)
        strict = jnp.logical_and(same_head, before)
        incl = jnp.logical_and(same_head, jnp.logical_or(before, ct_ == rt_))
        tri = ((s64 <= t64) if d == 0 else (s64 >= t64)).astype(BF16)
        masks.append((strict, incl, jnp.logical_and(strict, same_sub), tri))

    y_acc[...] = jnp.zeros_like(y_acc)
    b_acc[...] = jnp.zeros_like(b_acc)
    st_ref[...] = jnp.zeros_like(st_ref)

    def stack2(x):
        return jnp.concatenate([jnp.where(lane_lo, x, 0.0), jnp.where(lane_lo, 0.0, x)], axis=0)

    def dup(x):
        return jnp.concatenate([x, x], axis=0)

    def unstack(x):
        return jnp.where(lane_lo, x[:c], x[c:])

    sh_t = lax.broadcasted_iota(jnp.int32, (c, c + 2 * HALO), 0) + HALO
    sh_s = lax.broadcasted_iota(jnp.int32, (c, c + 2 * HALO), 1)
    shift_base = jnp.where(jnp.abs(sh_s - sh_t) == 1, 0.5, 0.0)

    n_steps = n_lat + n_ctx
    fields = ("at", "bt", "kt", "rt", "v", "bonus")

    def step_rows(i, d):
        return pl.ds(pl.multiple_of(_scan_chunk_id(i, n_lat, n_ctx, d == 1) * c, c), c)

    def prepare(i, d):
        tri = masks[d][3]
        c0 = pl.multiple_of(_scan_chunk_id(i, n_lat, n_ctx, d == 1) * c, c)
        p0 = pl.multiple_of(jnp.maximum(c0 - HALO, 0), HALO)
        n0 = pl.multiple_of(jnp.minimum(c0 + c, t_rows - HALO), HALO)
        no_prev = jnp.logical_or(c0 == 0, c0 == lat_rows)
        no_next = jnp.logical_or(c0 + c == lat_rows, c0 + c == t_rows)
        dead_prev = jnp.where(no_prev, HALO - 1, -1)
        dead_next = jnp.where(no_next, HALO + c, -1)
        dead = jnp.logical_or(sh_s == dead_prev, sh_s == dead_next)
        shift = jnp.where(dead, 0.0, shift_base).astype(BF16)

        def mixed(ref, mu):
            x = ref[pl.ds(c0, c), :]
            ext = jnp.concatenate([ref[pl.ds(p0, HALO), :], x, ref[pl.ds(n0, HALO), :]], axis=0)
            x = x.astype(F32)
            return x + mu * (jnp.dot(shift, ext, preferred_element_type=F32) - x)

        r = mixed(r_ref, mu_ref[0])
        k = mixed(k_ref, mu_ref[1])
        v = mixed(v_ref, mu_ref[2])
        lo = mixed(lo_ref, mulo_ref[...])
        kk = k * kkw_ref[...]
        yield
        kk_norm = jnp.sqrt(_head_sum_mxu(kk * kk, ones_bd))
        w_pre = _bdot(jnp.tanh(lo[:, :LANES]), w2_ref[d])
        a_pre = _bdot(lo[:, LANES:], a2_ref[d])
        yield
        kk = kk / jnp.maximum(kk_norm, 1e-12)
        logw = -math.exp(-0.5) * _sigmoid(w0_ref[d] + w_pre)
        a = _sigmoid(a0_ref[d] + a_pre)
        k_d = k * (1.0 + (a - 1.0) * ka_ref[...])
        lw_hi = logw.astype(BF16)
        lw_lo = (logw - lw_hi.astype(F32)).astype(BF16)
        cum = jnp.dot(tri, lw_hi, preferred_element_type=F32) + jnp.dot(tri, lw_lo, preferred_element_type=F32)
        bonus_sum = jnp.concatenate(
            [_bdot((r * k_d * rk_ref[...])[:, pr * LANES:(pr + 1) * LANES], ones_bd) for pr in range(PAIRS)], axis=1)
        yield
        p = jnp.exp(cum)
        inv_p = jnp.exp(-cum)
        staged = dict(at=-kk * jnp.exp(cum - logw), bt=kk * a * inv_p, kt=k_d * inv_p, rt=r * p, v=v,
                      bonus=bonus_sum * v)
        yield "stores next"
        for f, name in enumerate(fields):
            prep_ref[d, f] = staged[name]
        plast_ref[d] = p[c - 1:c, :] if d == 0 else p[0:1, :]

    for d in range(2):
        for _ in prepare(0, d):
            pass

    def step(i, carry):
        rows = [step_rows(i, d) for d in range(2)]
        inst = [(d, pr) for d in range(2) for pr in range(PAIRS)]
        each = lambda fn: [fn(j) for j in range(len(inst))]
        sl = lambda j: slice(inst[j][1] * LANES, (inst[j][1] + 1) * LANES)
        op = lambda j, name: prep_ref[inst[j][0], fields.index(name), :, sl(j)]
        msk = lambda j: masks[inst[j][0]]
        p_last = [plast_ref[d] for d in range(2)]
        for d in range(2):
            b_acc[rows[d], :] += prep_ref[d, fields.index("bonus")]
        nxt = jnp.minimum(i + 1, n_steps - 1)
        staging = [[prepare(nxt, d), None] for d in range(2)]

        def advance_staging():
            for s in staging:
                if s[1] is None:
                    s[1] = next(s[0])

        rhs = each(lambda j: jnp.concatenate([dup(op(j, "bt")), dup(op(j, "kt"))], axis=0).astype(BF16))
        top = each(lambda j: _bdot_nt(stack2(op(j, "at")), rhs[j]))
        s0 = each(lambda j: st_ref[inst[j][0], inst[j][1]])
        proj = each(lambda j: _bdot_nt(jnp.concatenate([op(j, "at"), op(j, "rt")], axis=0), s0[j]))
        v2 = each(lambda j: dup(op(j, "v")).astype(BF16))
        n_all = each(lambda j: jnp.where(msk(j)[0], top[j][:, :c2], 0.0))
        aak_v = each(lambda j: _bdot(jnp.where(msk(j)[0], top[j][:, c2:], 0.0), v2[j]))
        advance_staging()
        n_tr = each(lambda j: n_all[j].T)
        nsub_tr = each(lambda j: jnp.where(masks[1 - inst[j][0]][2], n_tr[j], 0.0))
        t_tr = each(lambda j: eye + nsub_tr[j])
        p_tr = each(lambda j: _bdot(nsub_tr[j], nsub_tr[j]))
        span = 2
        while 2 * span < RW_SUB:
            both = each(lambda j: _bdot(p_tr[j], jnp.concatenate([t_tr[j], p_tr[j]], axis=1)))
            t_tr = each(lambda j: t_tr[j] + both[j][:, :c2])
            p_tr = each(lambda j: both[j][:, c2:])
            span *= 2
            advance_staging()
        t_tr = each(lambda j: t_tr[j] + _bdot(p_tr[j], t_tr[j]))
        corr = each(lambda j: _bdot(n_tr[j] - nsub_tr[j], t_tr[j]))
        tinv_tr = each(lambda j: t_tr[j] + _bdot(t_tr[j], corr[j]))
        u_st = each(lambda j: _bdot_tn(tinv_tr[j], dup(proj[j][:c]) + aak_v[j]))
        bot = each(lambda j: _bdot_nt(stack2(op(j, "rt")), rhs[j]))
        y_st = each(lambda j: _bdot(
            jnp.concatenate([jnp.where(msk(j)[1], bot[j][:, :c2], 0.0), jnp.where(msk(j)[1], bot[j][:, c2:], 0.0)],
                            axis=1),
            jnp.concatenate([u_st[j].astype(BF16), v2[j]], axis=0)))
        upd = each(lambda j: _bdot_tn(jnp.concatenate([unstack(u_st[j]), op(j, "v")], axis=0),
                                      jnp.concatenate([op(j, "bt"), op(j, "kt")], axis=0)))
        for j, (d, pr) in enumerate(inst):
            st_ref[d, pr] = (s0[j] + jnp.where(same_head, upd[j], 0.0)) * p_last[d][:, sl(j)]
            y_acc[rows[d], sl(j)] += proj[j][c:] + unstack(y_st[j])
        for g, _ in staging:
            for _ in g:
                pass
        return carry

    lax.fori_loop(0, n_steps, step, 0)

    def finish(i, carry):
        rows = pl.ds(pl.multiple_of(i * ROW_TILE, ROW_TILE), ROW_TILE)
        y = y_acc[rows, :]
        mean = _head_sum_mxu(y, ones_bd) * (1.0 / HD)
        yc = y - mean
        var = _head_sum_mxu(yc * yc, ones_bd) * (1.0 / HD)
        gn = yc * lax.rsqrt(var + RWKV_GN_EPS) * lnw_ref[...] + lnb_ref[...]
        o_ref[rows, :] = ((gn + b_acc[rows, :]) * _silu(z_ref[rows, :].astype(F32))).astype(o_ref.dtype)
        return carry

    lax.fori_loop(0, t_rows // ROW_TILE, finish, 0)


def _rwkv(u, n_lat_rows, col_shift, col_z, rw_mu, rw_w0, rw_w2, rw_a0, rw_a2, rw_kk, rw_ka, rw_rk, rw_ln_w,
          rw_ln_b):
    b, t, _ = u.shape
    c = RW_CHUNK
    assert c == HD
    col_lo = col_shift + 3 * W
    def pad_dirs(w2):
        z = jnp.zeros_like(w2[0])
        return jnp.stack([jnp.concatenate([w2[0], z], axis=0), jnp.concatenate([z, w2[1]], axis=0)]).astype(BF16)

    row = lambda a: a.reshape(1, W).astype(F32)
    blk = lambda col, width=W: pl.BlockSpec((None, t, width), lambda bi, col=col, width=width: (bi, 0, col // width))
    full = lambda shape: pl.BlockSpec(shape, lambda bi: (0,) * len(shape))
    return pl.pallas_call(
        functools.partial(_rwkv_kernel, n_lat=n_lat_rows // c, n_ctx=(t - n_lat_rows) // c),
        grid=(b,),
        in_specs=[
            blk(col_shift), blk(col_shift + W), blk(col_shift + 2 * W), blk(col_lo, 2 * LANES), blk(col_z),
            full((3, 1, W)), full((1, 2 * LANES)), full((1, W)), full((1, W)), full((2, 1, W)), full((2, 1, W)),
            full((2, 2 * LORA, W)), full((2, 2 * LORA, W)), full((1, W)), full((1, W)), full((1, W)),
        ],
        out_specs=pl.BlockSpec((None, t, W), lambda bi: (bi, 0, 0)),
        out_shape=jax.ShapeDtypeStruct((b, t, W), BF16),
        scratch_shapes=[pltpu.VMEM((t, W), F32), pltpu.VMEM((t, W), F32),
                        pltpu.VMEM((2, PAIRS, LANES, LANES), F32), pltpu.VMEM((2, 6, c, W), F32),
                        pltpu.VMEM((2, 1, W), F32)],
        compiler_params=_cparams("parallel"),
        name="rwkv7",
    )(u, u, u, u, u, rw_mu[:3 * W].reshape(3, 1, W), rw_mu[3 * W:].reshape(1, 2 * LANES), row(rw_kk), row(rw_ka),
      rw_w0.reshape(2, 1, W), rw_a0.reshape(2, 1, W), pad_dirs(rw_w2), pad_dirs(rw_a2), row(rw_rk), row(rw_ln_w),
      row(rw_ln_b))


def _out_proj_kernel(*refs, n_src, lat_tiles, final):
    a_ref, b_ref, w_ref, mod_ref = refs[n_src:n_src + 4]
    wa = a_ref.shape[-1]
    y = _bdot(a_ref[...], w_ref[:wa, :]) + _bdot(b_ref[...], w_ref[wa:, :])
    hn = _read_token_rows(refs[:n_src], lat_tiles) + mod_ref[2:3, :] * y
    if final:
        fnw_ref, o_ref = refs[n_src + 4:]
        o_ref[...] = hn * lax.rsqrt(jnp.mean(hn * hn, axis=-1, keepdims=True) + NORM_EPS) * fnw_ref[...]
    else:
        (o_ref,) = refs[n_src + 4:]
        o_ref[...] = hn


def _out_proj(a, b, w, h, mod, n_lat, final_norm_w=None):
    bsz, rows, wa = a.shape
    d = w.shape[1]
    tm = ROW_TILE
    lat_tiles = n_lat // tm
    final = final_norm_w is not None
    src_specs, srcs = _token_rows(h, tm, lat_tiles)
    in_specs = src_specs + [
        pl.BlockSpec((None, tm, wa), lambda bi, i: (bi, i, 0)),
        pl.BlockSpec((None, tm, b.shape[-1]), lambda bi, i: (bi, i, 0)),
        pl.BlockSpec(w.shape, lambda bi, i: (0, 0)),
        pl.BlockSpec((None, None, 3, d), lambda bi, i: (bi, (i >= lat_tiles).astype(jnp.int32), 0, 0)),
    ]
    args = srcs + [a, b, w, mod]
    if final:
        in_specs.append(pl.BlockSpec((1, d), lambda bi, i: (0, 0)))
        args.append(final_norm_w.reshape(1, d))
    return pl.pallas_call(
        functools.partial(_out_proj_kernel, n_src=len(srcs), lat_tiles=lat_tiles, final=final),
        grid=(bsz, rows // tm),
        in_specs=in_specs,
        out_specs=pl.BlockSpec((None, tm, d), lambda bi, i: (bi, i, 0)),
        out_shape=jax.ShapeDtypeStruct((bsz, rows, d), F32),
        compiler_params=_cparams("parallel", "parallel"),
        name="out_proj_final" if final else "out_proj",
    )(*args)


HY_FAST = 64
HY_SUB = 8
HY_CH_TILE = 256
HY_UNROLL = 16


def _hy_dft_tables(l):
    n = 2 * l
    half = l // HY_FAST
    k1 = jnp.arange(half, dtype=jnp.int32)[None, :, None]
    n1 = jnp.arange(half, dtype=jnp.int32)[None, None, :]
    n2 = jnp.arange(HY_FAST, dtype=jnp.int32)[:, None, None]
    ang = (jnp.pi / n) * (((2 * k1 + 1) * (HY_FAST * n1 + n2)) % (2 * n)).astype(F32)
    m1 = jnp.concatenate([jnp.cos(ang), -jnp.sin(ang)], axis=1)
    groups = HY_FAST // HY_SUB
    eye = jnp.eye(HY_SUB, dtype=F32)
    m1g = m1.reshape(groups, HY_SUB, 2 * half, half)
    big = jnp.transpose(m1g, (0, 2, 1, 3))[:, :, :, :, None] * eye[None, None, :, None, :]
    big = big.reshape(groups, 2 * half * HY_SUB, half * HY_SUB)
    kk = jnp.arange(HY_FAST, dtype=jnp.int32)
    ang2 = (2.0 * jnp.pi / HY_FAST) * ((kk[:, None] * kk[None, :]) % HY_FAST).astype(F32)
    wr, wi = jnp.cos(ang2), -jnp.sin(ang2)
    w2 = jnp.concatenate([jnp.concatenate([wr, -wi], axis=1), jnp.concatenate([wi, wr], axis=1)], axis=0)
    w2_inv = jnp.concatenate([jnp.concatenate([wr, wi], axis=1), jnp.concatenate([-wi, wr], axis=1)], axis=0)
    return big.astype(BF16), jnp.swapaxes(big, 1, 2).astype(BF16), w2.astype(BF16), w2_inv.astype(BF16)


def _hy_stage1(src_ref, m1_ref, bs_ref):
    half, _, c = src_ref.shape
    for g in range(HY_FAST // HY_SUB):
        cols = pl.ds(g * HY_SUB, HY_SUB)
        x = src_ref[:, cols, :].reshape(half * HY_SUB, c).astype(BF16)
        out = jnp.dot(m1_ref[g], x, preferred_element_type=F32)
        bs_ref[:, cols, :] = out.reshape(2 * half, HY_SUB, c)


def _hy_stage2(bs_ref, w2_ref, k1):
    half = bs_ref.shape[0] // 2
    return jnp.dot(w2_ref[...], jnp.concatenate([bs_ref[k1], bs_ref[half + k1]], axis=0).astype(BF16),
                   preferred_element_type=F32)


def _hy_filter_kernel(z_ref, w1_ref, b1_ref, f1_ref, w2_ref, b2_ref, f2_ref, w3_ref, dec_ref, o_ref, hid_ref):
    j = pl.program_id(0)

    @pl.when(j == 0)
    def _():
        hid = jnp.sin(f1_ref[...] * (_dot_hi(z_ref[...], w1_ref[...]) + b1_ref[...]))
        hid_ref[...] = jnp.sin(f2_ref[...] * (_dot_hi(hid, w2_ref[...]) + b2_ref[...]))

    h = _dot_hi(hid_ref[...], w3_ref[...]) * dec_ref[...]
    row = lax.broadcasted_iota(jnp.int32, h.shape, 0)
    o_ref[...] = jnp.where(jnp.logical_and(j % 2 == 1, row == 0), 0.0, h).astype(o_ref.dtype)


def _hy_filters(l, w1, b1, f1, w2, b2, f2, w3):
    t = jnp.linspace(0.0, 1.0, l)[:, None]
    bands = jnp.linspace(1e-4, HY_BANDS - 1, HY_BANDS)
    ang = (2.0 * math.pi / l) * jnp.arange(l, dtype=F32)[:, None] * bands[None]
    z = jnp.concatenate([t, jnp.cos(ang), -jnp.sin(ang)], axis=-1)
    emb = z.shape[1]
    emb_pad = 64
    z = jnp.pad(z, ((0, 0), (0, emb_pad - emb)))
    w1p = jnp.pad(w1, ((0, emb_pad - emb), (0, 0)))
    deltas = jnp.abs(jnp.linspace(math.log(HY_TARGET) / HY_SLOW_PCT, math.log(HY_TARGET) / HY_FAST_PCT, W))
    dec = jnp.exp(-t * deltas)
    ffn = w1.shape[1]
    nblk = w3.shape[1] // W
    full = lambda shape: pl.BlockSpec(shape, lambda j: (0,) * len(shape))
    return pl.pallas_call(
        _hy_filter_kernel,
        grid=(nblk,),
        in_specs=[full((l, emb_pad)), full((emb_pad, ffn)), full((1, ffn)), full((1, ffn)), full((ffn, ffn)),
                  full((1, ffn)), full((1, ffn)), pl.BlockSpec((ffn, W), lambda j: (0, j)), full((l, W))],
        out_specs=pl.BlockSpec((l, W), lambda j: (0, j)),
        out_shape=jax.ShapeDtypeStruct((l, nblk * W), F32),
        scratch_shapes=[pltpu.VMEM((l, ffn), F32)],
        compiler_params=_cparams("arbitrary"),
        name="hyena_filters",
    )(z, w1p, b1.reshape(1, -1), f1.reshape(1, -1), w2, b2.reshape(1, -1), f2.reshape(1, -1), w3, dec)


def _hy_spectrum_kernel(hf_ref, hb_ref, m1_ref, w2_ref, o_ref, bs_ref, xf_ref, *, half):
    _hy_stage1(hf_ref, m1_ref, bs_ref)

    def forward(k1, carry):
        xf_ref[k1] = _hy_stage2(bs_ref, w2_ref, k1)
        return carry

    lax.fori_loop(0, half, forward, 0, unroll=HY_UNROLL)
    _hy_stage1(hb_ref, m1_ref, bs_ref)

    def combine(k1, carry):
        xf = xf_ref[k1]
        xb = _hy_stage2(bs_ref, w2_ref, k1)
        o_ref[k1] = jnp.concatenate([xf[:HY_FAST] + xb[:HY_FAST], xf[HY_FAST:] - xb[HY_FAST:]],
                                    axis=0).astype(o_ref.dtype)
        return carry

    lax.fori_loop(0, half, combine, 0, unroll=HY_UNROLL)


def _hy_spectra(filt, m1, w2):
    l = filt.shape[0]
    half = l // HY_FAST
    ct = HY_CH_TILE
    nct = W // ct
    orders = filt.shape[1] // (2 * W)
    filt = filt.reshape(half, HY_FAST, filt.shape[1])
    return pl.pallas_call(
        functools.partial(_hy_spectrum_kernel, half=half),
        grid=(orders, nct),
        in_specs=[pl.BlockSpec((half, HY_FAST, ct), lambda o, c: (0, 0, 2 * o * nct + c)),
                  pl.BlockSpec((half, HY_FAST, ct), lambda o, c: (0, 0, (2 * o + 1) * nct + c)),
                  pl.BlockSpec(m1.shape, lambda o, c: (0, 0, 0)), pl.BlockSpec(w2.shape, lambda o, c: (0, 0))],
        out_specs=pl.BlockSpec((None, half, 2 * HY_FAST, ct), lambda o, c: (o, 0, 0, c)),
        out_shape=jax.ShapeDtypeStruct((orders, half, 2 * HY_FAST, W), BF16),
        scratch_shapes=[pltpu.VMEM((2 * half, HY_FAST, ct), F32), pltpu.VMEM((half, 2 * HY_FAST, ct), F32)],
        compiler_params=_cparams("parallel", "parallel"),
        name="hyena_spectra",
    )(filt, filt, m1, w2)


def _hyena_kernel(v_ref, x1_ref, x2_ref, g_ref, cw_ref, cb_ref, bias_ref, m1_ref, m1t_ref, w2_ref, w2i_ref, gs_ref,
                  o_ref, z_ref, bs_ref, cs_ref, acc_ref, *, l):
    o = pl.program_id(2)
    rc = ROW_TILE
    half = l // HY_FAST
    ct = z_ref.shape[-1]

    def conv_rows(ref, sec, c0):
        x, prev, nxt = _with_neighbours(ref, slice(None), c0, rc, l, l)
        w = cw_ref[sec]
        return prev * w[0:1, :] + x * w[1:2, :] + nxt * w[2:3, :] + cb_ref[sec]

    slabs = rc // HY_FAST

    @pl.when(o == 0)
    def _():
        def body(i, carry):
            c0 = pl.multiple_of(i * rc, rc)
            z_ref[pl.ds(i * slabs, slabs)] = conv_rows(v_ref, 0, c0).reshape(slabs, HY_FAST, ct)
            return carry

        lax.fori_loop(0, l // rc, body, 0)

    _hy_stage1(z_ref, m1_ref, bs_ref)

    def per_k1(k1, carry):
        x = _hy_stage2(bs_ref, w2_ref, k1)
        xr, xi = x[:HY_FAST], x[HY_FAST:]
        g = gs_ref[k1].astype(F32)
        gr, gi = g[:HY_FAST], g[HY_FAST:]
        y = jnp.concatenate([xr * gr - xi * gi, xr * gi + xi * gr], axis=0).astype(BF16)
        cs_ref[k1] = jnp.dot(w2i_ref[...], y, preferred_element_type=F32)
        return carry

    lax.fori_loop(0, half, per_k1, 0, unroll=HY_UNROLL)

    for g in range(HY_FAST // HY_SUB):
        c_re = cs_ref[:, pl.ds(g * HY_SUB, HY_SUB), :].reshape(half * HY_SUB, ct)
        c_im = cs_ref[:, pl.ds(HY_FAST + g * HY_SUB, HY_SUB), :].reshape(half * HY_SUB, ct)
        out = jnp.dot(m1t_ref[g], jnp.concatenate([c_re, c_im], axis=0).astype(BF16), preferred_element_type=F32)
        acc_ref[:, pl.ds(g * HY_SUB, HY_SUB), :] = out.reshape(half, HY_SUB, ct)

    def finish(i, carry):
        c0 = pl.multiple_of(i * rc, rc)
        rows = pl.ds(c0, rc)
        tile = pl.ds(i * slabs, slabs)
        conv = (acc_ref[tile] * (1.0 / l) + z_ref[tile] * bias_ref[o]).reshape(rc, ct)

        @pl.when(o == 0)
        def _():
            z_ref[tile] = (conv_rows(x1_ref, 1, c0) * conv).reshape(slabs, HY_FAST, ct)

        @pl.when(o == 1)
        def _():
            zn = conv_rows(x2_ref, 2, c0) * conv
            o_ref[rows, :] = (zn * _silu(g_ref[rows, :].astype(F32))).astype(o_ref.dtype)

        return carry

    lax.fori_loop(0, l // rc, finish, 0)


def _hyena(u, l, conv_w, conv_b, hy_bias, tables, spectra):
    b = u.shape[0]
    ct = HY_CH_TILE
    nct = W // ct
    half = l // HY_FAST
    sec = lambda s: pl.BlockSpec((None, l, ct), lambda bi, c, o, s=s: (bi, 0, s * nct + c))
    cw = jnp.transpose(conv_w.reshape(3, 3, W), (1, 0, 2))
    whole = lambda a: pl.BlockSpec(a.shape, lambda bi, c, o: (0,) * a.ndim)
    return pl.pallas_call(
        functools.partial(_hyena_kernel, l=l),
        grid=(b, nct, 2),
        in_specs=[
            sec(0), sec(1), sec(2), sec(3),
            pl.BlockSpec((3, 3, ct), lambda bi, c, o: (0, 0, c)),
            pl.BlockSpec((3, 1, ct), lambda bi, c, o: (0, 0, c)),
            pl.BlockSpec((2, 1, ct), lambda bi, c, o: (0, 0, c)),
            whole(tables[0]), whole(tables[1]), whole(tables[2]), whole(tables[3]),
            pl.BlockSpec((None, half, 2 * HY_FAST, ct), lambda bi, c, o: (o, 0, 0, c)),
        ],
        out_specs=pl.BlockSpec((None, l, ct), lambda bi, c, o: (bi, 0, c)),
        out_shape=jax.ShapeDtypeStruct((b, l, W), BF16),
        scratch_shapes=[pltpu.VMEM((half, HY_FAST, ct), F32), pltpu.VMEM((2 * half, HY_FAST, ct), F32),
                        pltpu.VMEM((half, 2 * HY_FAST, ct), F32), pltpu.VMEM((half, HY_FAST, ct), F32)],
        compiler_params=_cparams("parallel", "parallel", "arbitrary"),
        name="hyena",
    )(u, u, u, u, cw, conv_b.reshape(3, 1, W), hy_bias.reshape(2, 1, W), *tables, spectra)


NA_Q_ROWS = 4
NA_K_ROWS = 12
NA_BLOCKS_PER_STEP = 2
NEG_BIG = -1e30


def _na_bias_tables(rpb, rows):
    heads = rpb.shape[0]
    n_dr, n_dc = 2 * NA_WIN_R - 1, 2 * NA_WIN_C - 1
    cq = np.arange(GRID_W)[:, None]
    kc = np.arange(GRID_W)[None, :]
    c0 = np.clip(cq - NA_WIN_C // 2, 0, GRID_W - NA_WIN_C)
    col_ok = (kc >= c0) & (kc < c0 + NA_WIN_C)
    col_sel = (np.arange(n_dc)[:, None, None] == (kc - cq + NA_WIN_C - 1)[None]).astype(np.float32)
    tiles = jnp.einsum("phrj,jcx->phrcx", rpb.astype(F32).reshape(heads // 2, 2, n_dr, n_dc), col_sel,
                       precision=lax.Precision.HIGHEST)
    tiles = jnp.where(col_ok, tiles, NEG_BIG)
    tiles = jnp.pad(tiles, ((0, 0), (0, 0), (0, 2), (0, 0), (0, 0)), constant_values=NEG_BIG)
    return jnp.concatenate([tiles[:, :, :n_dr + 1], tiles[:, :, 1:]], axis=-1)


def _na_kernel(q_ref, k_ref, v_ref, z_ref, bias_ref, o_ref, *, l):
    rows = l // GRID_W
    nblk = rows // NA_Q_ROWS
    qn = NA_Q_ROWS * GRID_W
    kn = NA_K_ROWS * GRID_W
    n_ctx = k_ref.shape[0] - l
    lane_lo = _lane_ids((1, LANES)) < HD
    k_ctx = k_ref[pl.ds(l, n_ctx), :]
    v_ctx = v_ref[pl.ds(l, n_ctx), :]

    def window_bias(i):
        ks = jnp.clip(NA_Q_ROWS * (i - 1), 0, rows - NA_K_ROWS)
        row_groups = []
        for hh in range(2):
            for qr in range(NA_Q_ROWS):
                rq = NA_Q_ROWS * i + qr
                r0 = jnp.clip(rq - NA_WIN_R // 2, 0, rows - NA_WIN_R)
                pieces = []
                for w in range(0, NA_K_ROWS, 2):
                    kr = ks + w
                    in_win = lambda r: jnp.logical_and(r >= r0, r < r0 + NA_WIN_R).astype(jnp.int32)
                    ok = jnp.where(lane_lo, in_win(kr), in_win(kr + 1)) > 0
                    tile = bias_ref[hh, jnp.clip(kr - rq + NA_WIN_R - 1, 0, 2 * NA_WIN_R - 2)]
                    pieces.append(jnp.where(ok, tile, NEG_BIG))
                row_groups.append(jnp.concatenate(pieces, axis=1))
        return jnp.concatenate(row_groups, axis=0)

    def body(it, carry):
        blocks = [it * NA_BLOCKS_PER_STEP + j for j in range(NA_BLOCKS_PER_STEP)]
        each = lambda fn: [fn(j, blocks[j]) for j in range(NA_BLOCKS_PER_STEP)]
        qrows = each(lambda j, i: pl.ds(pl.multiple_of(i * qn, qn), qn))
        krows = each(lambda j, i: pl.ds(
            pl.multiple_of(jnp.clip(NA_Q_ROWS * (i - 1), 0, rows - NA_K_ROWS) * GRID_W, GRID_W), kn))

        def stacked_q(j, i):
            q = q_ref[qrows[j], :].astype(F32) * HD ** -0.5
            return jnp.concatenate([jnp.where(lane_lo, q, 0.0), jnp.where(lane_lo, 0.0, q)], axis=0).astype(BF16)

        q2 = each(stacked_q)
        s_win = each(lambda j, i: _bdot_nt(q2[j], k_ref[krows[j], :]) + window_bias(i))
        s_ctx = each(lambda j, i: _bdot_nt(q2[j], k_ctx))
        m = each(lambda j, i: jnp.maximum(jnp.max(s_win[j], axis=-1, keepdims=True),
                                          jnp.max(s_ctx[j], axis=-1, keepdims=True)))
        p_win = each(lambda j, i: jnp.exp(s_win[j] - m[j]))
        p_ctx = each(lambda j, i: jnp.exp(s_ctx[j] - m[j]))
        den = each(lambda j, i: jnp.sum(p_win[j], axis=-1, keepdims=True) + jnp.sum(p_ctx[j], axis=-1, keepdims=True))
        out = each(lambda j, i: (_bdot(p_win[j], v_ref[krows[j], :]) + _bdot(p_ctx[j], v_ctx)) / den[j])
        for j in range(NA_BLOCKS_PER_STEP):
            gate = _silu(z_ref[qrows[j], :].astype(F32))
            o_ref[qrows[j], :] = (jnp.where(lane_lo, out[j][:qn], out[j][qn:]) * gate).astype(o_ref.dtype)
        return carry

    lax.fori_loop(0, nblk // NA_BLOCKS_PER_STEP, body, 0)


def _natten(u, l, col_q, col_k, col_v, col_z, bias):
    b, t, _ = u.shape
    lat = lambda col: pl.BlockSpec((None, l, LANES), lambda bi, p, col=col: (bi, 0, col // LANES + p))
    full = lambda col: pl.BlockSpec((None, t, LANES), lambda bi, p, col=col: (bi, 0, col // LANES + p))
    return pl.pallas_call(
        functools.partial(_na_kernel, l=l),
        grid=(b, PAIRS),
        in_specs=[lat(col_q), full(col_k), full(col_v), lat(col_z),
                  pl.BlockSpec((None,) + bias.shape[1:], lambda bi, p: (p, 0, 0, 0, 0))],
        out_specs=pl.BlockSpec((None, l, LANES), lambda bi, p: (bi, 0, p)),
        out_shape=jax.ShapeDtypeStruct((b, l, W), BF16),
        compiler_params=_cparams("parallel", "parallel"),
        name="natten",
    )(u, u, u, u, bias)


def kernel(x, c, ctx, c_ctx, ada_w, ada_b, norm_w, final_norm_w, even_w_in, even_w_out, ret_decay, rw_mu, rw_w0, rw_w2, rw_a0, rw_a2, rw_kk, rw_ka, rw_rk, rw_ln_w, rw_ln_b, odd_w_in, odd_w_out, hy_conv_w, hy_conv_b, hy_w1, hy_b1, hy_f1, hy_w2, hy_b2, hy_f2, hy_w3, hy_bias, na_rpb):
    b, l, d = x.shape
    n_ctx = ctx.shape[1]
    t = l + n_ctx
    depth = ada_w.shape[0]
    assert depth == 2 and l % ROW_TILE == 0 and n_ctx % ROW_TILE == 0 and (l // GRID_W) >= NA_K_ROWS
    assert (l // GRID_W) % (NA_Q_ROWS * NA_BLOCKS_PER_STEP) == 0

    cond_rows = -(-(b + 1) // 8) * 8
    cond = jnp.concatenate([c, c_ctx[None, :], jnp.zeros((cond_rows - b - 1, d), F32)], axis=0)
    mod_all = _ada_mod(cond, ada_w, ada_b)
    mods = [jnp.stack([mod_all[i, :b].reshape(b, 3, d),
                       jnp.broadcast_to(mod_all[i, b].reshape(1, 3, d), (b, 3, d))], axis=1) for i in range(depth)]

    h = (x, ctx)

    col_rz, col_wz, col_shift = 3 * W, 4 * W, 5 * W
    u0 = _norm_proj(h, norm_w[0], mods[0], even_w_in[0].astype(BF16), l, t)
    cos_t, sin_t = _rope_tables(l, t)
    ret_o = _retention(u0, cos_t, sin_t, ret_decay[0], l, 0, W, 2 * W, col_rz)
    rw_o = _rwkv(u0, l, col_shift, col_wz, rw_mu[0], rw_w0[0], rw_w2[0], rw_a0[0], rw_a2[0], rw_kk[0], rw_ka[0],
                 rw_rk[0], rw_ln_w[0], rw_ln_b[0])
    h = _out_proj(ret_o, rw_o, even_w_out[0].astype(BF16), h, mods[0], l)

    u1 = _norm_proj(h, norm_w[1], mods[1], odd_w_in[0].astype(BF16), l, t)
    tables = _hy_dft_tables(l)
    filt = _hy_filters(l, hy_w1[0], hy_b1[0], hy_f1[0], hy_w2[0], hy_b2[0], hy_f2[0], hy_w3[0])
    spectra = _hy_spectra(filt, tables[0], tables[2])
    hy_o = _hyena(u1, l, hy_conv_w[0], hy_conv_b[0], hy_bias[0], tables, spectra)
    na_o = _natten(u1, l, 4 * W, 5 * W, 6 * W, 7 * W, _na_bias_tables(na_rpb[0], l // GRID_W))
    return _out_proj(hy_o, na_o, odd_w_out[0].astype(BF16), h, mods[1], l, final_norm_w)
```

```python
import functools
import math

import jax
import jax.numpy as jnp
import numpy as np
from jax import lax
from jax.experimental import pallas as pl
from jax.experimental.pallas import tpu as pltpu

F32 = jnp.float32
BF16 = jnp.bfloat16

GRID_W = 64
NORM_EPS = 1e-6
ROPE_BASE = 10000.0
HEADS = 8
HD = 64
W = HEADS * HD
LORA = 64
RWKV_GN_EPS = 64e-5
HY_BANDS = 16
HY_TARGET = 1e-2
HY_FAST_PCT = 0.3
HY_SLOW_PCT = 1.5
NA_WIN_R = 8
NA_WIN_C = 16
LANES = 128
PAIRS = W // LANES
RET_CHUNK = 128
RW_CHUNK = 64
RW_SUB = 32
ROW_TILE = 256
HALO = 16
VMEM_LIMIT = 56 * 1024 * 1024


def _cparams(*sem):
    return pltpu.CompilerParams(dimension_semantics=sem, vmem_limit_bytes=VMEM_LIMIT)


def _bdot(a, b):
    return jnp.dot(a.astype(BF16), b.astype(BF16), preferred_element_type=F32)


def _bdot_nt(a, b):
    return lax.dot_general(a.astype(BF16), b.astype(BF16), (((1,), (1,)), ((), ())), preferred_element_type=F32)


def _bdot_tn(a, b):
    return lax.dot_general(a.astype(BF16), b.astype(BF16), (((0,), (0,)), ((), ())), preferred_element_type=F32)


def _dot_hi(a, b):
    ah = a.astype(BF16)
    al = (a - ah.astype(F32)).astype(BF16)
    bh = b.astype(BF16)
    bl = (b - bh.astype(F32)).astype(BF16)
    d = lambda p, q: jnp.dot(p, q, preferred_element_type=F32)
    return d(ah, bh) + d(ah, bl) + d(al, bh)


def _silu(x):
    return x * (1.0 / (1.0 + jnp.exp(-x)))


def _sigmoid(x):
    return 1.0 / (1.0 + jnp.exp(-x))


def _lane_ids(shape):
    return lax.broadcasted_iota(jnp.int32, shape, len(shape) - 1)


def _row_ids(shape):
    return lax.broadcasted_iota(jnp.int32, shape, len(shape) - 2)


def _head_sum_mxu(x, ones_bd):
    hi = x.astype(BF16)
    lo = (x - hi.astype(F32)).astype(BF16)
    tiles = []
    for p in range(x.shape[-1] // LANES):
        sl = slice(p * LANES, (p + 1) * LANES)
        tiles.append(jnp.dot(hi[:, sl], ones_bd, preferred_element_type=F32)
                     + jnp.dot(lo[:, sl], ones_bd, preferred_element_type=F32))
    return tiles[0] if len(tiles) == 1 else jnp.concatenate(tiles, axis=-1)


def _with_neighbours(ref, cols, c0, c, n_rows, lat_rows):
    row = lax.broadcasted_iota(jnp.int32, (c, 1), 0)
    x = ref[pl.ds(c0, c), cols].astype(F32)
    p0 = pl.multiple_of(jnp.maximum(c0 - HALO, 0), HALO)
    n0 = pl.multiple_of(jnp.minimum(c0 + c, n_rows - HALO), HALO)
    prev_row = ref[pl.ds(p0, HALO), cols].astype(F32)[HALO - 1:HALO, :]
    next_row = ref[pl.ds(n0, HALO), cols].astype(F32)[0:1, :]
    has_prev = jnp.logical_and(c0 != 0, c0 != lat_rows)
    has_next = jnp.logical_and(c0 + c != lat_rows, c0 + c != n_rows)
    prev_row = jnp.where(has_prev, prev_row, 0.0)
    next_row = jnp.where(has_next, next_row, 0.0)
    prev = jnp.where(row == 0, prev_row, pltpu.roll(x, 1, 0))
    nxt = jnp.where(row == c - 1, next_row, pltpu.roll(x, c - 1, 0))
    return x, prev, nxt


def _scan_chunk_id(i, n_lat, n_ctx, reverse):
    if reverse:
        return jnp.where(i < n_ctx, n_lat + n_ctx - 1 - i, n_lat - 1 - (i - n_ctx))
    return jnp.where(i < n_ctx, n_lat + i, i - n_ctx)


def _ada_kernel(c_ref, w_ref, b_ref, o_ref):
    o_ref[...] = _dot_hi(_silu(c_ref[...]), w_ref[...]) + b_ref[...]


def _ada_mod(cond, ada_w, ada_b):
    depth, d, d3 = ada_w.shape
    rows = cond.shape[0]
    return pl.pallas_call(
        _ada_kernel,
        grid=(depth, d3 // d),
        in_specs=[
            pl.BlockSpec((rows, d), lambda i, j: (0, 0)),
            pl.BlockSpec((None, d, d), lambda i, j: (i, 0, j)),
            pl.BlockSpec((None, 1, d), lambda i, j: (i, 0, j)),
        ],
        out_specs=pl.BlockSpec((None, rows, d), lambda i, j: (i, 0, j)),
        out_shape=jax.ShapeDtypeStruct((depth, rows, d3), F32),
        compiler_params=_cparams("parallel", "parallel"),
        name="ada_mod",
    )(cond, ada_w, ada_b.reshape(depth, 1, d3))


def _token_rows(h, tm, lat_tiles):
    if isinstance(h, tuple):
        d = h[0].shape[-1]
        return [pl.BlockSpec((None, tm, d), lambda bi, i: (bi, jnp.minimum(i, lat_tiles - 1), 0)),
                pl.BlockSpec((None, tm, d), lambda bi, i: (bi, jnp.maximum(i - lat_tiles, 0), 0))], list(h)
    return [pl.BlockSpec((None, tm, h.shape[-1]), lambda bi, i: (bi, i, 0))], [h]


def _read_token_rows(refs, lat_tiles):
    if len(refs) == 2:
        return jnp.where(pl.program_id(1) >= lat_tiles, refs[1][...], refs[0][...])
    return refs[0][...]


def _norm_proj_kernel(*refs, n_src, lat_tiles):
    nw_ref, mod_ref, w_ref, o_ref = refs[n_src:]
    x = _read_token_rows(refs[:n_src], lat_tiles)
    y = x * lax.rsqrt(jnp.mean(x * x, axis=-1, keepdims=True) + NORM_EPS) * nw_ref[...]
    n = y * (1.0 + mod_ref[1:2, :]) + mod_ref[0:1, :]
    o_ref[...] = _bdot(n, w_ref[...]).astype(o_ref.dtype)


def _norm_proj(h, norm_w, mod, w, n_lat, t):
    d, n = w.shape
    b = mod.shape[0]
    tm = ROW_TILE
    lat_tiles = n_lat // tm
    src_specs, srcs = _token_rows(h, tm, lat_tiles)
    return pl.pallas_call(
        functools.partial(_norm_proj_kernel, n_src=len(srcs), lat_tiles=lat_tiles),
        grid=(b, t // tm),
        in_specs=src_specs + [
            pl.BlockSpec((1, d), lambda bi, i: (0, 0)),
            pl.BlockSpec((None, None, 3, d), lambda bi, i: (bi, (i >= lat_tiles).astype(jnp.int32), 0, 0)),
            pl.BlockSpec((d, n), lambda bi, i: (0, 0)),
        ],
        out_specs=pl.BlockSpec((None, tm, n), lambda bi, i: (bi, i, 0)),
        out_shape=jax.ShapeDtypeStruct((b, t, n), BF16),
        compiler_params=_cparams("parallel", "parallel"),
        name="norm_proj",
    )(*srcs, norm_w.reshape(1, d), mod, w)


def _ret_kernel(q_ref, k_ref, v_ref, z_ref, cos_ref, sin_ref, rd_ref, o_ref, y_acc, st_ref, dm_ref, qd_ref, kd_ref,
                *, n_lat, n_ctx):
    c = RET_CHUNK
    lane = _lane_ids((1, LANES))
    lane_lo = lane < HD
    rope_lo = (lane % HD) < (HD // 2)
    pos = lax.broadcasted_iota(jnp.int32, (c, 1), 0).astype(F32)
    ti = lax.broadcasted_iota(jnp.int32, (c, c), 0)
    si = lax.broadcasted_iota(jnp.int32, (c, c), 1)
    same_head = (_row_ids((LANES, LANES)) < HD) == (_lane_ids((LANES, LANES)) < HD)
    ones_bd = same_head.astype(BF16)
    inst = [(d, pr) for d in range(2) for pr in range(PAIRS)]
    sl = lambda pr: slice(pr * LANES, (pr + 1) * LANES)

    g_chunk = []
    for d, pr in inst:
        lg_a = -jnp.exp(rd_ref[d, 2 * pr])
        lg_b = -jnp.exp(rd_ref[d, 2 * pr + 1])
        lgv = jnp.where(lane_lo, lg_a, lg_b)
        diff = ((ti - si) if d == 0 else (si - ti)).astype(F32)
        keep = diff >= 0
        dmat = lambda lg: jnp.where(keep, jnp.exp(jnp.where(keep, diff, 0.0) * lg), 0.0)
        dm_ref[d, pr] = jnp.concatenate([dmat(lg_a), dmat(lg_b)], axis=0)
        qd_ref[d, :, sl(pr)] = jnp.exp(((pos + 1.0) if d == 0 else (c - pos)) * lgv)
        kd_ref[d, :, sl(pr)] = jnp.exp(((c - 1.0 - pos) if d == 0 else pos) * lgv)
        g_chunk.append(jnp.exp(c * lgv))
    y_acc[...] = jnp.zeros_like(y_acc)
    st_ref[...] = jnp.zeros_like(st_ref)

    def rope(x, rows):
        partner = jnp.where(rope_lo, pltpu.roll(x, LANES - HD // 2, 1), pltpu.roll(x, HD // 2, 1))
        return x * cos_ref[rows, :] + partner * sin_ref[rows, :]

    def stack2(x):
        return jnp.concatenate([jnp.where(lane_lo, x, 0.0), jnp.where(lane_lo, 0.0, x)], axis=0)

    def step(i, carry):
        rows = [pl.ds(pl.multiple_of(_scan_chunk_id(i, n_lat, n_ctx, d == 1) * c, c), c) for d in range(2)]
        each = lambda fn: [fn(j, *inst[j]) for j in range(len(inst))]
        q = each(lambda j, d, pr: rope(q_ref[rows[d], sl(pr)].astype(F32), rows[d]))
        k = each(lambda j, d, pr: rope(k_ref[rows[d], sl(pr)].astype(F32) * HD ** -0.5, rows[d]))
        v = each(lambda j, d, pr: v_ref[rows[d], sl(pr)])
        s0 = each(lambda j, d, pr: st_ref[d, pr])
        scores = each(lambda j, d, pr: _bdot_nt(stack2(q[j]), k[j]) * dm_ref[d, pr])
        cross = each(lambda j, d, pr: _bdot(q[j] * qd_ref[d, :, sl(pr)], s0[j]))
        upd = each(lambda j, d, pr: _bdot_tn(k[j] * kd_ref[d, :, sl(pr)], v[j]))
        intra = each(lambda j, d, pr: _bdot(scores[j], v[j]))
        for j, (d, pr) in enumerate(inst):
            st_ref[d, pr] = s0[j] * g_chunk[j] + jnp.where(same_head, upd[j], 0.0)
            y_acc[rows[d], sl(pr)] += jnp.where(lane_lo, intra[j][:c], intra[j][c:]) + cross[j]
        return carry

    lax.fori_loop(0, n_lat + n_ctx, step, 0)

    def finish(i, carry):
        rows = pl.ds(pl.multiple_of(i * ROW_TILE, ROW_TILE), ROW_TILE)
        y = y_acc[rows, :]
        ms = _head_sum_mxu(y * y, ones_bd) * (1.0 / HD)
        o_ref[rows, :] = (y * lax.rsqrt(ms + NORM_EPS) * _silu(z_ref[rows, :].astype(F32))).astype(o_ref.dtype)
        return carry

    lax.fori_loop(0, (n_lat + n_ctx) * c // ROW_TILE, finish, 0)


def _retention(u, cos_t, sin_t, ret_decay, n_lat_rows, col_q, col_k, col_v, col_z):
    b, t, _ = u.shape
    c = RET_CHUNK
    rd = jnp.broadcast_to(ret_decay.astype(F32)[:, :, None, None], (2, HEADS, 1, LANES))
    blk = lambda col: pl.BlockSpec((None, t, W), lambda bi, col=col: (bi, 0, col // W))
    return pl.pallas_call(
        functools.partial(_ret_kernel, n_lat=n_lat_rows // c, n_ctx=(t - n_lat_rows) // c),
        grid=(b,),
        in_specs=[
            blk(col_q), blk(col_k), blk(col_v), blk(col_z),
            pl.BlockSpec((t, LANES), lambda bi: (0, 0)),
            pl.BlockSpec((t, LANES), lambda bi: (0, 0)),
            pl.BlockSpec((2, HEADS, 1, LANES), lambda bi: (0, 0, 0, 0)),
        ],
        out_specs=pl.BlockSpec((None, t, W), lambda bi: (bi, 0, 0)),
        out_shape=jax.ShapeDtypeStruct((b, t, W), BF16),
        scratch_shapes=[pltpu.VMEM((t, W), F32), pltpu.VMEM((2, PAIRS, LANES, LANES), F32),
                        pltpu.VMEM((2, PAIRS, 2 * c, c), F32), pltpu.VMEM((2, c, W), F32),
                        pltpu.VMEM((2, c, W), F32)],
        compiler_params=_cparams("parallel"),
        name="retention",
    )(u, u, u, u, cos_t, sin_t, rd)


def _rope_tables(n_lat_rows, t):
    pos = jnp.arange(n_lat_rows)
    row = (pos // GRID_W).astype(F32)
    col = (pos % GRID_W).astype(F32)
    nf = HD // 4
    inv = ROPE_BASE ** (-jnp.arange(nf, dtype=F32) / nf)
    ang = jnp.concatenate([row[:, None] * inv, col[:, None] * inv], axis=-1)
    cos, sin = jnp.cos(ang), jnp.sin(ang)
    cos_h = jnp.concatenate([cos, cos], axis=-1)
    sin_h = jnp.concatenate([-sin, sin], axis=-1)
    reps = LANES // HD
    cos_t = jnp.concatenate([jnp.tile(cos_h, (1, reps)), jnp.ones((t - n_lat_rows, LANES), F32)], axis=0)
    sin_t = jnp.concatenate([jnp.tile(sin_h, (1, reps)), jnp.zeros((t - n_lat_rows, LANES), F32)], axis=0)
    return cos_t, sin_t


def _rwkv_kernel(r_ref, k_ref, v_ref, lo_ref, z_ref, mu_ref, mulo_ref, kkw_ref, ka_ref, w0_ref, a0_ref, w2_ref,
                 a2_ref, rk_ref, lnw_ref, lnb_ref, o_ref, y_acc, b_acc, st_ref, prep_ref, plast_ref, tinv_ref, aakv_ref,
                 arbk_ref, *, n_lat, n_ctx):
    c = RW_CHUNK
    c2 = 2 * c
    t_rows = (n_lat + n_ctx) * c
    lat_rows = n_lat * c
    lane_lo = _lane_ids((1, LANES)) < HD
    ri = _row_ids((c2, c2))
    ci = _lane_ids((c2, c2))
    same_head = (ri < c) == (ci < c)
    rt_, ct_ = ri % c, ci % c
    same_sub = (rt_ // RW_SUB) == (ct_ // RW_SUB)
    eye = (ri == ci).astype(F32)
    ones_bd = same_head.astype(BF16)
    t64 = lax.broadcasted_iota(jnp.int32, (c, c), 0)
    s64 = lax.broadcasted_iota(jnp.int32, (c, c), 1)
    masks = []
    for d in range(2):
        before = (ct_ < rt_) if d == 0 else (ct_ > rt_)
        strict = jnp.logical_and(same_head, before)
        incl = jnp.logical_and(same_head, jnp.logical_or(before, ct_ == rt_))
        tri = ((s64 <= t64) if d == 0 else (s64 >= t64)).astype(BF16)
        masks.append((strict, incl, jnp.logical_and(strict, same_sub), tri))

    y_acc[...] = jnp.zeros_like(y_acc)
    b_acc[...] = jnp.zeros_like(b_acc)
    st_ref[...] = jnp.zeros_like(st_ref)

    def stack2(x):
        return jnp.concatenate([jnp.where(lane_lo, x, 0.0), jnp.where(lane_lo, 0.0, x)], axis=0)

    def dup(x):
        return jnp.concatenate([x, x], axis=0)

    def unstack(x):
        return jnp.where(lane_lo, x[:c], x[c:])

    sh_t = lax.broadcasted_iota(jnp.int32, (c, c + 2 * HALO), 0) + HALO
    sh_s = lax.broadcasted_iota(jnp.int32, (c, c + 2 * HALO), 1)
    shift_base = jnp.where(jnp.abs(sh_s - sh_t) == 1, 0.5, 0.0)

    n_steps = n_lat + n_ctx
    fields = ("at", "bt", "kt", "rt", "v", "bonus")

    def step_rows(i, d):
        return pl.ds(pl.multiple_of(_scan_chunk_id(i, n_lat, n_ctx, d == 1) * c, c), c)

    def prepare(i, d, slot):
        tri = masks[d][3]
        c0 = pl.multiple_of(_scan_chunk_id(i, n_lat, n_ctx, d == 1) * c, c)
        p0 = pl.multiple_of(jnp.maximum(c0 - HALO, 0), HALO)
        n0 = pl.multiple_of(jnp.minimum(c0 + c, t_rows - HALO), HALO)
        no_prev = jnp.logical_or(c0 == 0, c0 == lat_rows)
        no_next = jnp.logical_or(c0 + c == lat_rows, c0 + c == t_rows)
        dead_prev = jnp.where(no_prev, HALO - 1, -1)
        dead_next = jnp.where(no_next, HALO + c, -1)
        dead = jnp.logical_or(sh_s == dead_prev, sh_s == dead_next)
        shift = jnp.where(dead, 0.0, shift_base).astype(BF16)

        def mixed(ref, mu):
            x = ref[pl.ds(c0, c), :]
            ext = jnp.concatenate([ref[pl.ds(p0, HALO), :], x, ref[pl.ds(n0, HALO), :]], axis=0)
            x = x.astype(F32)
            return x + mu * (jnp.dot(shift, ext, preferred_element_type=F32) - x)

        r = mixed(r_ref, mu_ref[0])
        k = mixed(k_ref, mu_ref[1])
        v = mixed(v_ref, mu_ref[2])
        lo = mixed(lo_ref, mulo_ref[...])
        kk = k * kkw_ref[...]
        yield
        kk_norm = jnp.sqrt(_head_sum_mxu(kk * kk, ones_bd))
        w_pre = _bdot(jnp.tanh(lo[:, :LANES]), w2_ref[d])
        a_pre = _bdot(lo[:, LANES:], a2_ref[d])
        yield
        kk = kk / jnp.maximum(kk_norm, 1e-12)
        logw = -math.exp(-0.5) * _sigmoid(w0_ref[d] + w_pre)
        a = _sigmoid(a0_ref[d] + a_pre)
        k_d = k * (1.0 + (a - 1.0) * ka_ref[...])
        lw_hi = logw.astype(BF16)
        lw_lo = (logw - lw_hi.astype(F32)).astype(BF16)
        cum = jnp.dot(tri, lw_hi, preferred_element_type=F32) + jnp.dot(tri, lw_lo, preferred_element_type=F32)
        bonus_sum = jnp.concatenate(
            [_bdot((r * k_d * rk_ref[...])[:, pr * LANES:(pr + 1) * LANES], ones_bd) for pr in range(PAIRS)], axis=1)
        yield
        p = jnp.exp(cum)
        inv_p = jnp.exp(-cum)
        staged = dict(at=-kk * jnp.exp(cum - logw), bt=kk * a * inv_p, kt=k_d * inv_p, rt=r * p, v=v,
                      bonus=bonus_sum * v)
        yield "stores next"
        for f, name in enumerate(fields):
            prep_ref[slot, d, f] = staged[name]
        plast_ref[slot, d] = p[c - 1:c, :] if d == 0 else p[0:1, :]

    everyone = [(d, pr) for d in range(2) for pr in range(PAIRS)]
    each = lambda fn: [fn(j, *everyone[j]) for j in range(len(everyone))]
    lanes = lambda pr: slice(pr * LANES, (pr + 1) * LANES)

    def independent_half(slot):
        op = lambda name, d, pr: prep_ref[slot, d, fields.index(name), :, lanes(pr)]
        rhs = each(lambda j, d, pr: jnp.concatenate([dup(op("bt", d, pr)), dup(op("kt", d, pr))], axis=0).astype(BF16))
        top = each(lambda j, d, pr: _bdot_nt(stack2(op("at", d, pr)), rhs[j]))
        yield
        n_all = each(lambda j, d, pr: jnp.where(masks[d][0], top[j][:, :c2], 0.0))
        aak_v = each(lambda j, d, pr: _bdot(jnp.where(masks[d][0], top[j][:, c2:], 0.0),
                                            dup(op("v", d, pr)).astype(BF16)))
        n_tr = each(lambda j, d, pr: n_all[j].T)
        nsub_tr = each(lambda j, d, pr: jnp.where(masks[1 - d][2], n_tr[j], 0.0))
        t_tr = each(lambda j, d, pr: eye + nsub_tr[j])
        p_tr = each(lambda j, d, pr: _bdot(nsub_tr[j], nsub_tr[j]))
        yield
        span = 2
        while 2 * span < RW_SUB:
            both = each(lambda j, d, pr: _bdot(p_tr[j], jnp.concatenate([t_tr[j], p_tr[j]], axis=1)))
            t_tr = each(lambda j, d, pr: t_tr[j] + both[j][:, :c2])
            p_tr = each(lambda j, d, pr: both[j][:, c2:])
            span *= 2
            yield
        t_tr = each(lambda j, d, pr: t_tr[j] + _bdot(p_tr[j], t_tr[j]))
        yield
        corr = each(lambda j, d, pr: _bdot(n_tr[j] - nsub_tr[j], t_tr[j]))
        bot = each(lambda j, d, pr: _bdot_nt(stack2(op("rt", d, pr)), rhs[j]))
        yield
        tinv_tr = each(lambda j, d, pr: t_tr[j] + _bdot(t_tr[j], corr[j]))
        for j, (d, pr) in enumerate(everyone):
            tinv_ref[slot, j] = tinv_tr[j].astype(BF16)
            aakv_ref[slot, j] = aak_v[j]
            arbk_ref[slot, j] = jnp.where(jnp.concatenate([masks[d][1], masks[d][1]], axis=1), bot[j], 0.0).astype(BF16)

    def dependent_half(slot, i):
        op = lambda name, d, pr: prep_ref[slot, d, fields.index(name), :, lanes(pr)]
        rows = [step_rows(i, d) for d in range(2)]
        s0 = each(lambda j, d, pr: st_ref[d, pr])
        proj = each(lambda j, d, pr: _bdot_nt(jnp.concatenate([op("at", d, pr), op("rt", d, pr)], axis=0), s0[j]))
        yield
        u_st = each(lambda j, d, pr: _bdot_tn(tinv_ref[slot, j], dup(proj[j][:c]) + aakv_ref[slot, j]))
        yield
        y_st = each(lambda j, d, pr: _bdot(arbk_ref[slot, j], jnp.concatenate(
            [u_st[j].astype(BF16), dup(op("v", d, pr)).astype(BF16)], axis=0)))
        upd = each(lambda j, d, pr: _bdot_tn(jnp.concatenate([unstack(u_st[j]), op("v", d, pr)], axis=0),
                                             jnp.concatenate([op("bt", d, pr), op("kt", d, pr)], axis=0)))
        yield
        for j, (d, pr) in enumerate(everyone):
            st_ref[d, pr] = (s0[j] + jnp.where(same_head, upd[j], 0.0)) * plast_ref[slot, d][:, lanes(pr)]
            y_acc[rows[d], lanes(pr)] += proj[j][c:] + unstack(y_st[j])
        for d in range(2):
            b_acc[rows[d], :] += prep_ref[slot, d, fields.index("bonus")]

    def interleave(main, staging):
        main = list(main)
        held = [[g, None] for g in staging]
        turn = 0
        while main:
            for entry in list(main):
                g, every = entry
                if turn % every == 0 and next(g, "done") == "done":
                    main.remove(entry)
            for s in held:
                if s[1] is None:
                    s[1] = next(s[0])
            turn += 1
        for g, _ in held:
            for _ in g:
                pass

    interleave([(prepare(0, d, 0), 1) for d in range(2)], [])
    interleave([(independent_half(0), 1)], [prepare(1, d, 1) for d in range(2)])

    def step(i, carry):
        slot = i % 2
        nxt = jnp.minimum(i + 1, n_steps - 1)
        interleave([(dependent_half(1 - slot, i - 1), 1), (independent_half(slot), 1)],
                   [prepare(nxt, d, 1 - slot) for d in range(2)])
        return carry

    lax.fori_loop(1, n_steps, step, 0)
    interleave([(dependent_half((n_steps - 1) % 2, n_steps - 1), 1)], [])

    def finish(i, carry):
        rows = pl.ds(pl.multiple_of(i * ROW_TILE, ROW_TILE), ROW_TILE)
        y = y_acc[rows, :]
        mean = _head_sum_mxu(y, ones_bd) * (1.0 / HD)
        yc = y - mean
        var = _head_sum_mxu(yc * yc, ones_bd) * (1.0 / HD)
        gn = yc * lax.rsqrt(var + RWKV_GN_EPS) * lnw_ref[...] + lnb_ref[...]
        o_ref[rows, :] = ((gn + b_acc[rows, :]) * _silu(z_ref[rows, :].astype(F32))).astype(o_ref.dtype)
        return carry

    lax.fori_loop(0, t_rows // ROW_TILE, finish, 0)


def _rwkv(u, n_lat_rows, col_shift, col_z, rw_mu, rw_w0, rw_w2, rw_a0, rw_a2, rw_kk, rw_ka, rw_rk, rw_ln_w,
          rw_ln_b):
    b, t, _ = u.shape
    c = RW_CHUNK
    assert c == HD
    col_lo = col_shift + 3 * W
    def pad_dirs(w2):
        z = jnp.zeros_like(w2[0])
        return jnp.stack([jnp.concatenate([w2[0], z], axis=0), jnp.concatenate([z, w2[1]], axis=0)]).astype(BF16)

    row = lambda a: a.reshape(1, W).astype(F32)
    blk = lambda col, width=W: pl.BlockSpec((None, t, width), lambda bi, col=col, width=width: (bi, 0, col // width))
    full = lambda shape: pl.BlockSpec(shape, lambda bi: (0,) * len(shape))
    return pl.pallas_call(
        functools.partial(_rwkv_kernel, n_lat=n_lat_rows // c, n_ctx=(t - n_lat_rows) // c),
        grid=(b,),
        in_specs=[
            blk(col_shift), blk(col_shift + W), blk(col_shift + 2 * W), blk(col_lo, 2 * LANES), blk(col_z),
            full((3, 1, W)), full((1, 2 * LANES)), full((1, W)), full((1, W)), full((2, 1, W)), full((2, 1, W)),
            full((2, 2 * LORA, W)), full((2, 2 * LORA, W)), full((1, W)), full((1, W)), full((1, W)),
        ],
        out_specs=pl.BlockSpec((None, t, W), lambda bi: (bi, 0, 0)),
        out_shape=jax.ShapeDtypeStruct((b, t, W), BF16),
        scratch_shapes=[pltpu.VMEM((t, W), F32), pltpu.VMEM((t, W), F32),
                        pltpu.VMEM((2, PAIRS, LANES, LANES), F32), pltpu.VMEM((2, 2, 6, c, W), F32),
                        pltpu.VMEM((2, 2, 1, W), F32), pltpu.VMEM((2, 2 * PAIRS, 2 * c, 2 * c), BF16),
                        pltpu.VMEM((2, 2 * PAIRS, 2 * c, LANES), F32), pltpu.VMEM((2, 2 * PAIRS, 2 * c, 4 * c), BF16)],
        compiler_params=_cparams("parallel"),
        name="rwkv7",
    )(u, u, u, u, u, rw_mu[:3 * W].reshape(3, 1, W), rw_mu[3 * W:].reshape(1, 2 * LANES), row(rw_kk), row(rw_ka),
      rw_w0.reshape(2, 1, W), rw_a0.reshape(2, 1, W), pad_dirs(rw_w2), pad_dirs(rw_a2), row(rw_rk), row(rw_ln_w),
      row(rw_ln_b))


def _out_proj_kernel(*refs, n_src, lat_tiles, tail, ctx_cols):
    a_ref, b_ref, w_ref, mod_ref = refs[n_src:n_src + 4]
    wa = a_ref.shape[-1]
    y = _bdot(a_ref[...], w_ref[:wa, :]) + _bdot(b_ref[...], w_ref[wa:, :])
    hn = _read_token_rows(refs[:n_src], lat_tiles) + mod_ref[2:3, :] * y
    if tail == "final_norm":
        fnw_ref, o_ref = refs[n_src + 4:]
        o_ref[...] = hn * lax.rsqrt(jnp.mean(hn * hn, axis=-1, keepdims=True) + NORM_EPS) * fnw_ref[...]
    else:
        nw_ref, mod2_ref, w2_ref, o_ref, u_ref = refs[n_src + 4:]
        o_ref[...] = hn
        yn = hn * lax.rsqrt(jnp.mean(hn * hn, axis=-1, keepdims=True) + NORM_EPS) * nw_ref[...]
        n = yn * (1.0 + mod2_ref[1:2, :]) + mod2_ref[0:1, :]
        is_ctx = pl.program_id(1) >= lat_tiles

        @pl.when(jnp.logical_not(is_ctx))
        def _():
            u_ref[...] = _bdot(n, w2_ref[...]).astype(u_ref.dtype)

        @pl.when(is_ctx)
        def _():
            lo, hi = ctx_cols
            u_ref[...] = jnp.zeros_like(u_ref)
            u_ref[:, lo:hi] = _bdot(n, w2_ref[:, lo:hi]).astype(u_ref.dtype)


def _out_proj(a, b, w, h, mod, n_lat, final_norm_w=None, next_proj=None):
    bsz, rows, wa = a.shape
    d = w.shape[1]
    tm = ROW_TILE
    lat_tiles = n_lat // tm
    src_specs, srcs = _token_rows(h, tm, lat_tiles)
    kind = lambda bi, i: (bi, (i >= lat_tiles).astype(jnp.int32), 0, 0)
    in_specs = src_specs + [
        pl.BlockSpec((None, tm, wa), lambda bi, i: (bi, i, 0)),
        pl.BlockSpec((None, tm, b.shape[-1]), lambda bi, i: (bi, i, 0)),
        pl.BlockSpec(w.shape, lambda bi, i: (0, 0)),
        pl.BlockSpec((None, None, 3, d), kind),
    ]
    args = srcs + [a, b, w, mod]
    out_specs = pl.BlockSpec((None, tm, d), lambda bi, i: (bi, i, 0))
    out_shape = jax.ShapeDtypeStruct((bsz, rows, d), F32)
    ctx_cols = None
    if final_norm_w is not None:
        tail = "final_norm"
        in_specs.append(pl.BlockSpec((1, d), lambda bi, i: (0, 0)))
        args.append(final_norm_w.reshape(1, d))
    else:
        tail = "next_proj"
        norm_w, mod2, w2, ctx_cols = next_proj
        n2 = w2.shape[1]
        in_specs += [pl.BlockSpec((1, d), lambda bi, i: (0, 0)), pl.BlockSpec((None, None, 3, d), kind),
                     pl.BlockSpec((d, n2), lambda bi, i: (0, 0))]
        args += [norm_w.reshape(1, d), mod2, w2]
        out_specs = [out_specs, pl.BlockSpec((None, tm, n2), lambda bi, i: (bi, i, 0))]
        out_shape = [out_shape, jax.ShapeDtypeStruct((bsz, rows, n2), BF16)]
    return pl.pallas_call(
        functools.partial(_out_proj_kernel, n_src=len(srcs), lat_tiles=lat_tiles, tail=tail, ctx_cols=ctx_cols),
        grid=(bsz, rows // tm),
        in_specs=in_specs,
        out_specs=out_specs,
        out_shape=out_shape,
        compiler_params=_cparams("parallel", "parallel"),
        name="out_proj_" + tail,
    )(*args)


HY_FAST = 64
HY_SUB = 8
HY_CH_TILE = 256
HY_UNROLL = 16


def _hy_dft_tables(l):
    n = 2 * l
    half = l // HY_FAST
    k1 = jnp.arange(half, dtype=jnp.int32)[None, :, None]
    n1 = jnp.arange(half, dtype=jnp.int32)[None, None, :]
    n2 = jnp.arange(HY_FAST, dtype=jnp.int32)[:, None, None]
    ang = (jnp.pi / n) * (((2 * k1 + 1) * (HY_FAST * n1 + n2)) % (2 * n)).astype(F32)
    m1 = jnp.concatenate([jnp.cos(ang), -jnp.sin(ang)], axis=1)
    groups = HY_FAST // HY_SUB
    m1g = m1.reshape(groups, HY_SUB, 2 * half, half)
    jj = np.arange(HY_SUB)[:, None, None]
    row_sel = (np.arange(2 * half * HY_SUB)[None, :, None] == np.arange(2 * half)[None, None, :] * HY_SUB + jj)
    col_sel = (np.arange(half * HY_SUB)[None, None, :] == np.arange(half)[None, :, None] * HY_SUB + jj)
    place = lambda out: jnp.einsum("jRr,gjrn,jnC->" + out, row_sel.astype(np.float32), m1g,
                                   col_sel.astype(np.float32), precision=lax.Precision.HIGHEST).astype(BF16)
    big, big_t = place("gRC"), place("gCR")
    kk = jnp.arange(HY_FAST, dtype=jnp.int32)
    ang2 = (2.0 * jnp.pi / HY_FAST) * ((kk[:, None] * kk[None, :]) % HY_FAST).astype(F32)
    wr, wi = jnp.cos(ang2), -jnp.sin(ang2)
    w2 = jnp.concatenate([jnp.concatenate([wr, -wi], axis=1), jnp.concatenate([wi, wr], axis=1)], axis=0)
    w2_inv = jnp.concatenate([jnp.concatenate([wr, wi], axis=1), jnp.concatenate([-wi, wr], axis=1)], axis=0)
    return big, big_t, w2.astype(BF16), w2_inv.astype(BF16)


def _hy_stage1(src_ref, m1_ref, bs_ref):
    half, _, c = src_ref.shape
    for g in range(HY_FAST // HY_SUB):
        cols = pl.ds(g * HY_SUB, HY_SUB)
        x = src_ref[:, cols, :].reshape(half * HY_SUB, c).astype(BF16)
        out = jnp.dot(m1_ref[g], x, preferred_element_type=F32)
        bs_ref[:, cols, :] = out.reshape(2 * half, HY_SUB, c)


def _hy_stage2(bs_ref, w2_ref, k1):
    half = bs_ref.shape[0] // 2
    return jnp.dot(w2_ref[...], jnp.concatenate([bs_ref[k1], bs_ref[half + k1]], axis=0).astype(BF16),
                   preferred_element_type=F32)


def _hy_filter_kernel(z_ref, w1_ref, b1_ref, f1_ref, w2_ref, b2_ref, f2_ref, w3_ref, dec_ref, o_ref, hid_ref):
    j = pl.program_id(0)

    @pl.when(j == 0)
    def _():
        hid = jnp.sin(f1_ref[...] * (_dot_hi(z_ref[...], w1_ref[...]) + b1_ref[...]))
        hid_ref[...] = jnp.sin(f2_ref[...] * (_dot_hi(hid, w2_ref[...]) + b2_ref[...]))

    h = _dot_hi(hid_ref[...], w3_ref[...]) * dec_ref[...]
    row = lax.broadcasted_iota(jnp.int32, h.shape, 0)
    o_ref[...] = jnp.where(jnp.logical_and(j % 2 == 1, row == 0), 0.0, h).astype(o_ref.dtype)


def _hy_filters(l, w1, b1, f1, w2, b2, f2, w3):
    t = jnp.linspace(0.0, 1.0, l)[:, None]
    bands = jnp.linspace(1e-4, HY_BANDS - 1, HY_BANDS)
    ang = (2.0 * math.pi / l) * jnp.arange(l, dtype=F32)[:, None] * bands[None]
    z = jnp.concatenate([t, jnp.cos(ang), -jnp.sin(ang)], axis=-1)
    emb = z.shape[1]
    emb_pad = 64
    z = jnp.pad(z, ((0, 0), (0, emb_pad - emb)))
    w1p = jnp.pad(w1, ((0, emb_pad - emb), (0, 0)))
    deltas = jnp.abs(jnp.linspace(math.log(HY_TARGET) / HY_SLOW_PCT, math.log(HY_TARGET) / HY_FAST_PCT, W))
    dec = jnp.exp(-t * deltas)
    ffn = w1.shape[1]
    nblk = w3.shape[1] // W
    full = lambda shape: pl.BlockSpec(shape, lambda j: (0,) * len(shape))
    return pl.pallas_call(
        _hy_filter_kernel,
        grid=(nblk,),
        in_specs=[full((l, emb_pad)), full((emb_pad, ffn)), full((1, ffn)), full((1, ffn)), full((ffn, ffn)),
                  full((1, ffn)), full((1, ffn)), pl.BlockSpec((ffn, W), lambda j: (0, j)), full((l, W))],
        out_specs=pl.BlockSpec((l, W), lambda j: (0, j)),
        out_shape=jax.ShapeDtypeStruct((l, nblk * W), F32),
        scratch_shapes=[pltpu.VMEM((l, ffn), F32)],
        compiler_params=_cparams("arbitrary"),
        name="hyena_filters",
    )(z, w1p, b1.reshape(1, -1), f1.reshape(1, -1), w2, b2.reshape(1, -1), f2.reshape(1, -1), w3, dec)


def _hy_spectrum_kernel(hf_ref, hb_ref, m1_ref, w2_ref, o_ref, bs_ref, xf_ref, *, half):
    _hy_stage1(hf_ref, m1_ref, bs_ref)

    def forward(k1, carry):
        xf_ref[k1] = _hy_stage2(bs_ref, w2_ref, k1)
        return carry

    lax.fori_loop(0, half, forward, 0, unroll=HY_UNROLL)
    _hy_stage1(hb_ref, m1_ref, bs_ref)

    def combine(k1, carry):
        xf = xf_ref[k1]
        xb = _hy_stage2(bs_ref, w2_ref, k1)
        o_ref[k1] = jnp.concatenate([xf[:HY_FAST] + xb[:HY_FAST], xf[HY_FAST:] - xb[HY_FAST:]],
                                    axis=0).astype(o_ref.dtype)
        return carry

    lax.fori_loop(0, half, combine, 0, unroll=HY_UNROLL)


def _hy_spectra(filt, m1, w2):
    l = filt.shape[0]
    half = l // HY_FAST
    ct = HY_CH_TILE
    nct = W // ct
    orders = filt.shape[1] // (2 * W)
    filt = filt.reshape(half, HY_FAST, filt.shape[1])
    return pl.pallas_call(
        functools.partial(_hy_spectrum_kernel, half=half),
        grid=(orders, nct),
        in_specs=[pl.BlockSpec((half, HY_FAST, ct), lambda o, c: (0, 0, 2 * o * nct + c)),
                  pl.BlockSpec((half, HY_FAST, ct), lambda o, c: (0, 0, (2 * o + 1) * nct + c)),
                  pl.BlockSpec(m1.shape, lambda o, c: (0, 0, 0)), pl.BlockSpec(w2.shape, lambda o, c: (0, 0))],
        out_specs=pl.BlockSpec((None, half, 2 * HY_FAST, ct), lambda o, c: (o, 0, 0, c)),
        out_shape=jax.ShapeDtypeStruct((orders, half, 2 * HY_FAST, W), BF16),
        scratch_shapes=[pltpu.VMEM((2 * half, HY_FAST, ct), F32), pltpu.VMEM((half, 2 * HY_FAST, ct), F32)],
        compiler_params=_cparams("parallel", "parallel"),
        name="hyena_spectra",
    )(filt, filt, m1, w2)


def _hyena_kernel(v_ref, x1_ref, x2_ref, g_ref, cw_ref, cb_ref, bias_ref, m1_ref, m1t_ref, w2_ref, w2i_ref, gs_ref,
                  o_ref, z_ref, bs_ref, cs_ref, acc_ref, *, l):
    o = pl.program_id(2)
    rc = ROW_TILE
    half = l // HY_FAST
    ct = z_ref.shape[-1]

    def conv_rows(ref, sec, c0):
        x, prev, nxt = _with_neighbours(ref, slice(None), c0, rc, l, l)
        w = cw_ref[sec]
        return prev * w[0:1, :] + x * w[1:2, :] + nxt * w[2:3, :] + cb_ref[sec]

    slabs = rc // HY_FAST

    @pl.when(o == 0)
    def _():
        def body(i, carry):
            c0 = pl.multiple_of(i * rc, rc)
            z_ref[pl.ds(i * slabs, slabs)] = conv_rows(v_ref, 0, c0).reshape(slabs, HY_FAST, ct)
            return carry

        lax.fori_loop(0, l // rc, body, 0)

    _hy_stage1(z_ref, m1_ref, bs_ref)

    def per_k1(k1, carry):
        x = _hy_stage2(bs_ref, w2_ref, k1)
        xr, xi = x[:HY_FAST], x[HY_FAST:]
        g = gs_ref[k1].astype(F32)
        gr, gi = g[:HY_FAST], g[HY_FAST:]
        y = jnp.concatenate([xr * gr - xi * gi, xr * gi + xi * gr], axis=0).astype(BF16)
        cs_ref[k1] = jnp.dot(w2i_ref[...], y, preferred_element_type=F32)
        return carry

    lax.fori_loop(0, half, per_k1, 0, unroll=HY_UNROLL)

    for g in range(HY_FAST // HY_SUB):
        c_re = cs_ref[:, pl.ds(g * HY_SUB, HY_SUB), :].reshape(half * HY_SUB, ct)
        c_im = cs_ref[:, pl.ds(HY_FAST + g * HY_SUB, HY_SUB), :].reshape(half * HY_SUB, ct)
        out = jnp.dot(m1t_ref[g], jnp.concatenate([c_re, c_im], axis=0).astype(BF16), preferred_element_type=F32)
        acc_ref[:, pl.ds(g * HY_SUB, HY_SUB), :] = out.reshape(half, HY_SUB, ct)

    def finish(i, carry):
        c0 = pl.multiple_of(i * rc, rc)
        rows = pl.ds(c0, rc)
        tile = pl.ds(i * slabs, slabs)
        conv = (acc_ref[tile] * (1.0 / l) + z_ref[tile] * bias_ref[o]).reshape(rc, ct)

        @pl.when(o == 0)
        def _():
            z_ref[tile] = (conv_rows(x1_ref, 1, c0) * conv).reshape(slabs, HY_FAST, ct)

        @pl.when(o == 1)
        def _():
            zn = conv_rows(x2_ref, 2, c0) * conv
            o_ref[rows, :] = (zn * _silu(g_ref[rows, :].astype(F32))).astype(o_ref.dtype)

        return carry

    lax.fori_loop(0, l // rc, finish, 0)


def _hyena(u, l, conv_w, conv_b, hy_bias, tables, spectra):
    b = u.shape[0]
    ct = HY_CH_TILE
    nct = W // ct
    half = l // HY_FAST
    sec = lambda s: pl.BlockSpec((None, l, ct), lambda bi, c, o, s=s: (bi, 0, s * nct + c))
    cw = jnp.transpose(conv_w.reshape(3, 3, W), (1, 0, 2))
    whole = lambda a: pl.BlockSpec(a.shape, lambda bi, c, o: (0,) * a.ndim)
    return pl.pallas_call(
        functools.partial(_hyena_kernel, l=l),
        grid=(b, nct, 2),
        in_specs=[
            sec(0), sec(1), sec(2), sec(3),
            pl.BlockSpec((3, 3, ct), lambda bi, c, o: (0, 0, c)),
            pl.BlockSpec((3, 1, ct), lambda bi, c, o: (0, 0, c)),
            pl.BlockSpec((2, 1, ct), lambda bi, c, o: (0, 0, c)),
            whole(tables[0]), whole(tables[1]), whole(tables[2]), whole(tables[3]),
            pl.BlockSpec((None, half, 2 * HY_FAST, ct), lambda bi, c, o: (o, 0, 0, c)),
        ],
        out_specs=pl.BlockSpec((None, l, ct), lambda bi, c, o: (bi, 0, c)),
        out_shape=jax.ShapeDtypeStruct((b, l, W), BF16),
        scratch_shapes=[pltpu.VMEM((half, HY_FAST, ct), F32), pltpu.VMEM((2 * half, HY_FAST, ct), F32),
                        pltpu.VMEM((half, 2 * HY_FAST, ct), F32), pltpu.VMEM((half, HY_FAST, ct), F32)],
        compiler_params=_cparams("parallel", "parallel", "arbitrary"),
        name="hyena",
    )(u, u, u, u, cw, conv_b.reshape(3, 1, W), hy_bias.reshape(2, 1, W), *tables, spectra)


NA_Q_ROWS = 4
NA_K_ROWS = 12
NA_BLOCKS_PER_STEP = 2
NEG_BIG = -1e30


def _na_bias_tables(rpb, rows):
    heads = rpb.shape[0]
    n_dr, n_dc = 2 * NA_WIN_R - 1, 2 * NA_WIN_C - 1
    cq = np.arange(GRID_W)[:, None]
    kc = np.arange(GRID_W)[None, :]
    c0 = np.clip(cq - NA_WIN_C // 2, 0, GRID_W - NA_WIN_C)
    col_ok = (kc >= c0) & (kc < c0 + NA_WIN_C)
    col_sel = (np.arange(n_dc)[:, None, None] == (kc - cq + NA_WIN_C - 1)[None]).astype(np.float32)
    tiles = jnp.einsum("phrj,jcx->phrcx", rpb.astype(F32).reshape(heads // 2, 2, n_dr, n_dc), col_sel,
                       precision=lax.Precision.HIGHEST)
    tiles = jnp.where(col_ok, tiles, NEG_BIG)
    tiles = jnp.pad(tiles, ((0, 0), (0, 0), (0, 2), (0, 0), (0, 0)), constant_values=NEG_BIG)
    return jnp.concatenate([tiles[:, :, :n_dr + 1], tiles[:, :, 1:]], axis=-1)


def _na_kernel(q_ref, k_ref, v_ref, z_ref, bias_ref, o_ref, *, l):
    rows = l // GRID_W
    nblk = rows // NA_Q_ROWS
    qn = NA_Q_ROWS * GRID_W
    kn = NA_K_ROWS * GRID_W
    n_ctx = k_ref.shape[0] - l
    lane_lo = _lane_ids((1, LANES)) < HD
    k_ctx = k_ref[pl.ds(l, n_ctx), :]
    v_ctx = v_ref[pl.ds(l, n_ctx), :]

    def window_bias(i):
        ks = jnp.clip(NA_Q_ROWS * (i - 1), 0, rows - NA_K_ROWS)
        row_groups = []
        for hh in range(2):
            for qr in range(NA_Q_ROWS):
                rq = NA_Q_ROWS * i + qr
                r0 = jnp.clip(rq - NA_WIN_R // 2, 0, rows - NA_WIN_R)
                pieces = []
                for w in range(0, NA_K_ROWS, 2):
                    kr = ks + w
                    in_win = lambda r: jnp.logical_and(r >= r0, r < r0 + NA_WIN_R).astype(jnp.int32)
                    ok = jnp.where(lane_lo, in_win(kr), in_win(kr + 1)) > 0
                    tile = bias_ref[hh, jnp.clip(kr - rq + NA_WIN_R - 1, 0, 2 * NA_WIN_R - 2)]
                    pieces.append(jnp.where(ok, tile, NEG_BIG))
                row_groups.append(jnp.concatenate(pieces, axis=1))
        return jnp.concatenate(row_groups, axis=0)

    def body(it, carry):
        blocks = [it * NA_BLOCKS_PER_STEP + j for j in range(NA_BLOCKS_PER_STEP)]
        each = lambda fn: [fn(j, blocks[j]) for j in range(NA_BLOCKS_PER_STEP)]
        qrows = each(lambda j, i: pl.ds(pl.multiple_of(i * qn, qn), qn))
        krows = each(lambda j, i: pl.ds(
            pl.multiple_of(jnp.clip(NA_Q_ROWS * (i - 1), 0, rows - NA_K_ROWS) * GRID_W, GRID_W), kn))

        def stacked_q(j, i):
            q = q_ref[qrows[j], :].astype(F32) * HD ** -0.5
            return jnp.concatenate([jnp.where(lane_lo, q, 0.0), jnp.where(lane_lo, 0.0, q)], axis=0).astype(BF16)

        q2 = each(stacked_q)
        s_win = each(lambda j, i: _bdot_nt(q2[j], k_ref[krows[j], :]) + window_bias(i))
        s_ctx = each(lambda j, i: _bdot_nt(q2[j], k_ctx))
        m = each(lambda j, i: jnp.maximum(jnp.max(s_win[j], axis=-1, keepdims=True),
                                          jnp.max(s_ctx[j], axis=-1, keepdims=True)))
        p_win = each(lambda j, i: jnp.exp(s_win[j] - m[j]))
        p_ctx = each(lambda j, i: jnp.exp(s_ctx[j] - m[j]))
        den = each(lambda j, i: jnp.sum(p_win[j], axis=-1, keepdims=True) + jnp.sum(p_ctx[j], axis=-1, keepdims=True))
        out = each(lambda j, i: (_bdot(p_win[j], v_ref[krows[j], :]) + _bdot(p_ctx[j], v_ctx)) / den[j])
        for j in range(NA_BLOCKS_PER_STEP):
            gate = _silu(z_ref[qrows[j], :].astype(F32))
            o_ref[qrows[j], :] = (jnp.where(lane_lo, out[j][:qn], out[j][qn:]) * gate).astype(o_ref.dtype)
        return carry

    lax.fori_loop(0, nblk // NA_BLOCKS_PER_STEP, body, 0)


def _natten(u, l, col_q, col_k, col_v, col_z, bias):
    b, t, _ = u.shape
    lat = lambda col: pl.BlockSpec((None, l, LANES), lambda bi, p, col=col: (bi, 0, col // LANES + p))
    full = lambda col: pl.BlockSpec((None, t, LANES), lambda bi, p, col=col: (bi, 0, col // LANES + p))
    return pl.pallas_call(
        functools.partial(_na_kernel, l=l),
        grid=(b, PAIRS),
        in_specs=[lat(col_q), full(col_k), full(col_v), lat(col_z),
                  pl.BlockSpec((None,) + bias.shape[1:], lambda bi, p: (p, 0, 0, 0, 0))],
        out_specs=pl.BlockSpec((None, l, LANES), lambda bi, p: (bi, 0, p)),
        out_shape=jax.ShapeDtypeStruct((b, l, W), BF16),
        compiler_params=_cparams("parallel", "parallel"),
        name="natten",
    )(u, u, u, u, bias)


def kernel(x, c, ctx, c_ctx, ada_w, ada_b, norm_w, final_norm_w, even_w_in, even_w_out, ret_decay, rw_mu, rw_w0, rw_w2, rw_a0, rw_a2, rw_kk, rw_ka, rw_rk, rw_ln_w, rw_ln_b, odd_w_in, odd_w_out, hy_conv_w, hy_conv_b, hy_w1, hy_b1, hy_f1, hy_w2, hy_b2, hy_f2, hy_w3, hy_bias, na_rpb):
    b, l, d = x.shape
    n_ctx = ctx.shape[1]
    t = l + n_ctx
    depth = ada_w.shape[0]
    assert depth == 2 and l % ROW_TILE == 0 and n_ctx % ROW_TILE == 0 and (l // GRID_W) >= NA_K_ROWS
    assert (l // GRID_W) % (NA_Q_ROWS * NA_BLOCKS_PER_STEP) == 0

    cond_rows = -(-(b + 1) // 8) * 8
    cond = jnp.concatenate([c, c_ctx[None, :], jnp.zeros((cond_rows - b - 1, d), F32)], axis=0)
    mod_all = _ada_mod(cond, ada_w, ada_b)
    mods = [jnp.stack([mod_all[i, :b].reshape(b, 3, d),
                       jnp.broadcast_to(mod_all[i, b].reshape(1, 3, d), (b, 3, d))], axis=1) for i in range(depth)]

    h = (x, ctx)

    col_rz, col_wz, col_shift = 3 * W, 4 * W, 5 * W
    u0 = _norm_proj(h, norm_w[0], mods[0], even_w_in[0].astype(BF16), l, t)
    cos_t, sin_t = _rope_tables(l, t)
    ret_o = _retention(u0, cos_t, sin_t, ret_decay[0], l, 0, W, 2 * W, col_rz)
    rw_o = _rwkv(u0, l, col_shift, col_wz, rw_mu[0], rw_w0[0], rw_w2[0], rw_a0[0], rw_a2[0], rw_kk[0], rw_ka[0],
                 rw_rk[0], rw_ln_w[0], rw_ln_b[0])
    h, u1 = _out_proj(ret_o, rw_o, even_w_out[0].astype(BF16), h, mods[0], l,
                      next_proj=(norm_w[1], mods[1], odd_w_in[0].astype(BF16), (5 * W, 7 * W)))
    tables = _hy_dft_tables(l)
    filt = _hy_filters(l, hy_w1[0], hy_b1[0], hy_f1[0], hy_w2[0], hy_b2[0], hy_f2[0], hy_w3[0])
    spectra = _hy_spectra(filt, tables[0], tables[2])
    hy_o = _hyena(u1, l, hy_conv_w[0], hy_conv_b[0], hy_bias[0], tables, spectra)
    na_o = _natten(u1, l, 4 * W, 5 * W, 6 * W, 7 * W, _na_bias_tables(na_rpb[0], l // GRID_W))
    return _out_proj(hy_o, na_o, odd_w_out[0].astype(BF16), h, mods[1], l, final_norm_w)
```

```python
import functools
import math

import jax
import jax.numpy as jnp
import numpy as np
from jax import lax
from jax.experimental import pallas as pl
from jax.experimental.pallas import tpu as pltpu

F32 = jnp.float32
BF16 = jnp.bfloat16

GRID_W = 64
NORM_EPS = 1e-6
ROPE_BASE = 10000.0
HEADS = 8
HD = 64
W = HEADS * HD
LORA = 64
RWKV_GN_EPS = 64e-5
HY_BANDS = 16
HY_TARGET = 1e-2
HY_FAST_PCT = 0.3
HY_SLOW_PCT = 1.5
NA_WIN_R = 8
NA_WIN_C = 16
LANES = 128
PAIRS = W // LANES
RET_CHUNK = 128
RW_CHUNK = 64
RW_SUB = 32
ROW_TILE = 256
HALO = 16
VMEM_LIMIT = 56 * 1024 * 1024


def _cparams(*sem):
    return pltpu.CompilerParams(dimension_semantics=sem, vmem_limit_bytes=VMEM_LIMIT)


def _bdot(a, b):
    return jnp.dot(a.astype(BF16), b.astype(BF16), preferred_element_type=F32)


def _bdot_nt(a, b):
    return lax.dot_general(a.astype(BF16), b.astype(BF16), (((1,), (1,)), ((), ())), preferred_element_type=F32)


def _bdot_tn(a, b):
    return lax.dot_general(a.astype(BF16), b.astype(BF16), (((0,), (0,)), ((), ())), preferred_element_type=F32)


def _dot_hi(a, b):
    ah = a.astype(BF16)
    al = (a - ah.astype(F32)).astype(BF16)
    bh = b.astype(BF16)
    bl = (b - bh.astype(F32)).astype(BF16)
    d = lambda p, q: jnp.dot(p, q, preferred_element_type=F32)
    return d(ah, bh) + d(ah, bl) + d(al, bh)


def _silu(x):
    return x * (1.0 / (1.0 + jnp.exp(-x)))


def _sigmoid(x):
    return 1.0 / (1.0 + jnp.exp(-x))


def _lane_ids(shape):
    return lax.broadcasted_iota(jnp.int32, shape, len(shape) - 1)


def _row_ids(shape):
    return lax.broadcasted_iota(jnp.int32, shape, len(shape) - 2)


def _head_sum_mxu(x, ones_bd):
    hi = x.astype(BF16)
    lo = (x - hi.astype(F32)).astype(BF16)
    tiles = []
    for p in range(x.shape[-1] // LANES):
        sl = slice(p * LANES, (p + 1) * LANES)
        tiles.append(jnp.dot(hi[:, sl], ones_bd, preferred_element_type=F32)
                     + jnp.dot(lo[:, sl], ones_bd, preferred_element_type=F32))
    return tiles[0] if len(tiles) == 1 else jnp.concatenate(tiles, axis=-1)


def _with_neighbours(ref, cols, c0, c, n_rows, lat_rows):
    row = lax.broadcasted_iota(jnp.int32, (c, 1), 0)
    x = ref[pl.ds(c0, c), cols].astype(F32)
    p0 = pl.multiple_of(jnp.maximum(c0 - HALO, 0), HALO)
    n0 = pl.multiple_of(jnp.minimum(c0 + c, n_rows - HALO), HALO)
    prev_row = ref[pl.ds(p0, HALO), cols].astype(F32)[HALO - 1:HALO, :]
    next_row = ref[pl.ds(n0, HALO), cols].astype(F32)[0:1, :]
    has_prev = jnp.logical_and(c0 != 0, c0 != lat_rows)
    has_next = jnp.logical_and(c0 + c != lat_rows, c0 + c != n_rows)
    prev_row = jnp.where(has_prev, prev_row, 0.0)
    next_row = jnp.where(has_next, next_row, 0.0)
    prev = jnp.where(row == 0, prev_row, pltpu.roll(x, 1, 0))
    nxt = jnp.where(row == c - 1, next_row, pltpu.roll(x, c - 1, 0))
    return x, prev, nxt


def _scan_chunk_id(i, n_lat, n_ctx, reverse):
    if reverse:
        return jnp.where(i < n_ctx, n_lat + n_ctx - 1 - i, n_lat - 1 - (i - n_ctx))
    return jnp.where(i < n_ctx, n_lat + i, i - n_ctx)


def _ada_kernel(c_ref, w_ref, b_ref, o_ref):
    o_ref[...] = _dot_hi(_silu(c_ref[...]), w_ref[...]) + b_ref[...]


def _ada_mod(cond, ada_w, ada_b):
    depth, d, d3 = ada_w.shape
    rows = cond.shape[0]
    return pl.pallas_call(
        _ada_kernel,
        grid=(depth, d3 // d),
        in_specs=[
            pl.BlockSpec((rows, d), lambda i, j: (0, 0)),
            pl.BlockSpec((None, d, d), lambda i, j: (i, 0, j)),
            pl.BlockSpec((None, 1, d), lambda i, j: (i, 0, j)),
        ],
        out_specs=pl.BlockSpec((None, rows, d), lambda i, j: (i, 0, j)),
        out_shape=jax.ShapeDtypeStruct((depth, rows, d3), F32),
        compiler_params=_cparams("parallel", "parallel"),
        name="ada_mod",
    )(cond, ada_w, ada_b.reshape(depth, 1, d3))


def _token_rows(h, tm, lat_tiles):
    if isinstance(h, tuple):
        d = h[0].shape[-1]
        return [pl.BlockSpec((None, tm, d), lambda bi, i: (bi, jnp.minimum(i, lat_tiles - 1), 0)),
                pl.BlockSpec((None, tm, d), lambda bi, i: (bi, jnp.maximum(i - lat_tiles, 0), 0))], list(h)
    return [pl.BlockSpec((None, tm, h.shape[-1]), lambda bi, i: (bi, i, 0))], [h]


def _read_token_rows(refs, lat_tiles):
    if len(refs) == 2:
        return jnp.where(pl.program_id(1) >= lat_tiles, refs[1][...], refs[0][...])
    return refs[0][...]


def _norm_proj_kernel(*refs, n_src, lat_tiles):
    nw_ref, mod_ref, w_ref, o_ref = refs[n_src:]
    x = _read_token_rows(refs[:n_src], lat_tiles)
    y = x * lax.rsqrt(jnp.mean(x * x, axis=-1, keepdims=True) + NORM_EPS) * nw_ref[...]
    n = y * (1.0 + mod_ref[1:2, :]) + mod_ref[0:1, :]
    o_ref[...] = _bdot(n, w_ref[...]).astype(o_ref.dtype)


def _norm_proj(h, norm_w, mod, w, n_lat, t):
    d, n = w.shape
    b = mod.shape[0]
    tm = ROW_TILE
    lat_tiles = n_lat // tm
    src_specs, srcs = _token_rows(h, tm, lat_tiles)
    return pl.pallas_call(
        functools.partial(_norm_proj_kernel, n_src=len(srcs), lat_tiles=lat_tiles),
        grid=(b, t // tm),
        in_specs=src_specs + [
            pl.BlockSpec((1, d), lambda bi, i: (0, 0)),
            pl.BlockSpec((None, None, 3, d), lambda bi, i: (bi, (i >= lat_tiles).astype(jnp.int32), 0, 0)),
            pl.BlockSpec((d, n), lambda bi, i: (0, 0)),
        ],
        out_specs=pl.BlockSpec((None, tm, n), lambda bi, i: (bi, i, 0)),
        out_shape=jax.ShapeDtypeStruct((b, t, n), BF16),
        compiler_params=_cparams("parallel", "parallel"),
        name="norm_proj",
    )(*srcs, norm_w.reshape(1, d), mod, w)


def _ret_kernel(q_ref, k_ref, v_ref, z_ref, cos_ref, sin_ref, rd_ref, o_ref, y_acc, st_ref, dm_ref, qd_ref, kd_ref,
                *, n_lat, n_ctx):
    c = RET_CHUNK
    lane = _lane_ids((1, LANES))
    lane_lo = lane < HD
    rope_lo = (lane % HD) < (HD // 2)
    pos = lax.broadcasted_iota(jnp.int32, (c, 1), 0).astype(F32)
    ti = lax.broadcasted_iota(jnp.int32, (c, c), 0)
    si = lax.broadcasted_iota(jnp.int32, (c, c), 1)
    same_head = (_row_ids((LANES, LANES)) < HD) == (_lane_ids((LANES, LANES)) < HD)
    ones_bd = same_head.astype(BF16)
    inst = [(d, pr) for d in range(2) for pr in range(PAIRS)]
    sl = lambda pr: slice(pr * LANES, (pr + 1) * LANES)

    g_chunk = []
    for d, pr in inst:
        lg_a = -jnp.exp(rd_ref[d, 2 * pr])
        lg_b = -jnp.exp(rd_ref[d, 2 * pr + 1])
        lgv = jnp.where(lane_lo, lg_a, lg_b)
        diff = ((ti - si) if d == 0 else (si - ti)).astype(F32)
        keep = diff >= 0
        dmat = lambda lg: jnp.where(keep, jnp.exp(jnp.where(keep, diff, 0.0) * lg), 0.0)
        dm_ref[d, pr] = jnp.concatenate([dmat(lg_a), dmat(lg_b)], axis=0)
        qd_ref[d, :, sl(pr)] = jnp.exp(((pos + 1.0) if d == 0 else (c - pos)) * lgv)
        kd_ref[d, :, sl(pr)] = jnp.exp(((c - 1.0 - pos) if d == 0 else pos) * lgv)
        g_chunk.append(jnp.exp(c * lgv))
    y_acc[...] = jnp.zeros_like(y_acc)
    st_ref[...] = jnp.zeros_like(st_ref)

    def rope(x, rows):
        partner = jnp.where(rope_lo, pltpu.roll(x, LANES - HD // 2, 1), pltpu.roll(x, HD // 2, 1))
        return x * cos_ref[rows, :] + partner * sin_ref[rows, :]

    def stack2(x):
        return jnp.concatenate([jnp.where(lane_lo, x, 0.0), jnp.where(lane_lo, 0.0, x)], axis=0)

    def step(i, carry):
        rows = [pl.ds(pl.multiple_of(_scan_chunk_id(i, n_lat, n_ctx, d == 1) * c, c), c) for d in range(2)]
        each = lambda fn: [fn(j, *inst[j]) for j in range(len(inst))]
        q = each(lambda j, d, pr: rope(q_ref[rows[d], sl(pr)].astype(F32), rows[d]))
        k = each(lambda j, d, pr: rope(k_ref[rows[d], sl(pr)].astype(F32) * HD ** -0.5, rows[d]))
        v = each(lambda j, d, pr: v_ref[rows[d], sl(pr)])
        s0 = each(lambda j, d, pr: st_ref[d, pr])
        scores = each(lambda j, d, pr: _bdot_nt(stack2(q[j]), k[j]) * dm_ref[d, pr])
        cross = each(lambda j, d, pr: _bdot(q[j] * qd_ref[d, :, sl(pr)], s0[j]))
        upd = each(lambda j, d, pr: _bdot_tn(k[j] * kd_ref[d, :, sl(pr)], v[j]))
        intra = each(lambda j, d, pr: _bdot(scores[j], v[j]))
        for j, (d, pr) in enumerate(inst):
            st_ref[d, pr] = s0[j] * g_chunk[j] + jnp.where(same_head, upd[j], 0.0)
            y_acc[rows[d], sl(pr)] += jnp.where(lane_lo, intra[j][:c], intra[j][c:]) + cross[j]
        return carry

    lax.fori_loop(0, n_lat + n_ctx, step, 0)

    def finish(i, carry):
        rows = pl.ds(pl.multiple_of(i * ROW_TILE, ROW_TILE), ROW_TILE)
        y = y_acc[rows, :]
        ms = _head_sum_mxu(y * y, ones_bd) * (1.0 / HD)
        o_ref[rows, :] = (y * lax.rsqrt(ms + NORM_EPS) * _silu(z_ref[rows, :].astype(F32))).astype(o_ref.dtype)
        return carry

    lax.fori_loop(0, (n_lat + n_ctx) * c // ROW_TILE, finish, 0)


def _retention(u, cos_t, sin_t, ret_decay, n_lat_rows, col_q, col_k, col_v, col_z):
    b, t, _ = u.shape
    c = RET_CHUNK
    rd = jnp.broadcast_to(ret_decay.astype(F32)[:, :, None, None], (2, HEADS, 1, LANES))
    blk = lambda col: pl.BlockSpec((None, t, W), lambda bi, col=col: (bi, 0, col // W))
    return pl.pallas_call(
        functools.partial(_ret_kernel, n_lat=n_lat_rows // c, n_ctx=(t - n_lat_rows) // c),
        grid=(b,),
        in_specs=[
            blk(col_q), blk(col_k), blk(col_v), blk(col_z),
            pl.BlockSpec((t, LANES), lambda bi: (0, 0)),
            pl.BlockSpec((t, LANES), lambda bi: (0, 0)),
            pl.BlockSpec((2, HEADS, 1, LANES), lambda bi: (0, 0, 0, 0)),
        ],
        out_specs=pl.BlockSpec((None, t, W), lambda bi: (bi, 0, 0)),
        out_shape=jax.ShapeDtypeStruct((b, t, W), BF16),
        scratch_shapes=[pltpu.VMEM((t, W), F32), pltpu.VMEM((2, PAIRS, LANES, LANES), F32),
                        pltpu.VMEM((2, PAIRS, 2 * c, c), F32), pltpu.VMEM((2, c, W), F32),
                        pltpu.VMEM((2, c, W), F32)],
        compiler_params=_cparams("parallel"),
        name="retention",
    )(u, u, u, u, cos_t, sin_t, rd)


def _rope_tables(n_lat_rows, t):
    pos = jnp.arange(n_lat_rows)
    row = (pos // GRID_W).astype(F32)
    col = (pos % GRID_W).astype(F32)
    nf = HD // 4
    inv = ROPE_BASE ** (-jnp.arange(nf, dtype=F32) / nf)
    ang = jnp.concatenate([row[:, None] * inv, col[:, None] * inv], axis=-1)
    cos, sin = jnp.cos(ang), jnp.sin(ang)
    cos_h = jnp.concatenate([cos, cos], axis=-1)
    sin_h = jnp.concatenate([-sin, sin], axis=-1)
    reps = LANES // HD
    cos_t = jnp.concatenate([jnp.tile(cos_h, (1, reps)), jnp.ones((t - n_lat_rows, LANES), F32)], axis=0)
    sin_t = jnp.concatenate([jnp.tile(sin_h, (1, reps)), jnp.zeros((t - n_lat_rows, LANES), F32)], axis=0)
    return cos_t, sin_t


def _rwkv_kernel(r_ref, k_ref, v_ref, lo_ref, z_ref, mu_ref, mulo_ref, kkw_ref, ka_ref, w0_ref, a0_ref, w2_ref,
                 a2_ref, rk_ref, lnw_ref, lnb_ref, o_ref, y_acc, b_acc, st_ref, prep_ref, plast_ref, tinv_ref, aakv_ref,
                 arbk_ref, *, n_lat, n_ctx):
    c = RW_CHUNK
    c2 = 2 * c
    t_rows = (n_lat + n_ctx) * c
    lat_rows = n_lat * c
    lane_lo = _lane_ids((1, LANES)) < HD
    ri = _row_ids((c2, c2))
    ci = _lane_ids((c2, c2))
    same_head = (ri < c) == (ci < c)
    rt_, ct_ = ri % c, ci % c
    same_sub = (rt_ // RW_SUB) == (ct_ // RW_SUB)
    eye = (ri == ci).astype(F32)
    ones_bd = same_head.astype(BF16)
    t64 = lax.broadcasted_iota(jnp.int32, (c, c), 0)
    s64 = lax.broadcasted_iota(jnp.int32, (c, c), 1)
    masks = []
    for d in range(2):
        before = (ct_ < rt_) if d == 0 else (ct_ > rt_)
        strict = jnp.logical_and(same_head, before)
        incl = jnp.logical_and(same_head, jnp.logical_or(before, ct_ == rt_))
        tri = ((s64 <= t64) if d == 0 else (s64 >= t64)).astype(BF16)
        masks.append((strict, incl, jnp.logical_and(strict, same_sub), tri))

    y_acc[...] = jnp.zeros_like(y_acc)
    b_acc[...] = jnp.zeros_like(b_acc)
    st_ref[...] = jnp.zeros_like(st_ref)

    def stack2(x):
        return jnp.concatenate([jnp.where(lane_lo, x, 0.0), jnp.where(lane_lo, 0.0, x)], axis=0)

    def dup(x):
        return jnp.concatenate([x, x], axis=0)

    def unstack(x):
        return jnp.where(lane_lo, x[:c], x[c:])

    sh_t = lax.broadcasted_iota(jnp.int32, (c, c + 2 * HALO), 0) + HALO
    sh_s = lax.broadcasted_iota(jnp.int32, (c, c + 2 * HALO), 1)
    shift_base = jnp.where(jnp.abs(sh_s - sh_t) == 1, 0.5, 0.0)

    n_steps = n_lat + n_ctx
    fields = ("at", "bt", "kt", "rt", "v", "bonus")

    def step_rows(i, d):
        return pl.ds(pl.multiple_of(_scan_chunk_id(i, n_lat, n_ctx, d == 1) * c, c), c)

    def prepare(i, d, slot):
        tri = masks[d][3]
        c0 = pl.multiple_of(_scan_chunk_id(i, n_lat, n_ctx, d == 1) * c, c)
        p0 = pl.multiple_of(jnp.maximum(c0 - HALO, 0), HALO)
        n0 = pl.multiple_of(jnp.minimum(c0 + c, t_rows - HALO), HALO)
        no_prev = jnp.logical_or(c0 == 0, c0 == lat_rows)
        no_next = jnp.logical_or(c0 + c == lat_rows, c0 + c == t_rows)
        dead_prev = jnp.where(no_prev, HALO - 1, -1)
        dead_next = jnp.where(no_next, HALO + c, -1)
        dead = jnp.logical_or(sh_s == dead_prev, sh_s == dead_next)
        shift = jnp.where(dead, 0.0, shift_base).astype(BF16)

        def mixed(ref, mu):
            x = ref[pl.ds(c0, c), :]
            ext = jnp.concatenate([ref[pl.ds(p0, HALO), :], x, ref[pl.ds(n0, HALO), :]], axis=0)
            x = x.astype(F32)
            return x + mu * (jnp.dot(shift, ext, preferred_element_type=F32) - x)

        r = mixed(r_ref, mu_ref[0])
        k = mixed(k_ref, mu_ref[1])
        v = mixed(v_ref, mu_ref[2])
        lo = mixed(lo_ref, mulo_ref[...])
        kk = k * kkw_ref[...]
        yield
        kk_norm = jnp.sqrt(_head_sum_mxu(kk * kk, ones_bd))
        w_pre = _bdot(jnp.tanh(lo[:, :LANES]), w2_ref[d])
        a_pre = _bdot(lo[:, LANES:], a2_ref[d])
        yield
        kk = kk / jnp.maximum(kk_norm, 1e-12)
        logw = -math.exp(-0.5) * _sigmoid(w0_ref[d] + w_pre)
        a = _sigmoid(a0_ref[d] + a_pre)
        k_d = k * (1.0 + (a - 1.0) * ka_ref[...])
        lw_hi = logw.astype(BF16)
        lw_lo = (logw - lw_hi.astype(F32)).astype(BF16)
        cum = jnp.dot(tri, lw_hi, preferred_element_type=F32) + jnp.dot(tri, lw_lo, preferred_element_type=F32)
        bonus_sum = jnp.concatenate(
            [_bdot((r * k_d * rk_ref[...])[:, pr * LANES:(pr + 1) * LANES], ones_bd) for pr in range(PAIRS)], axis=1)
        yield
        p = jnp.exp(cum)
        inv_p = jnp.exp(-cum)
        staged = dict(at=-kk * jnp.exp(cum - logw), bt=kk * a * inv_p, kt=k_d * inv_p, rt=r * p, v=v,
                      bonus=bonus_sum * v)
        yield "stores next"
        for f, name in enumerate(fields):
            prep_ref[slot, d, f] = staged[name]
        plast_ref[slot, d] = p[c - 1:c, :] if d == 0 else p[0:1, :]

    everyone = [(d, pr) for d in range(2) for pr in range(PAIRS)]
    each = lambda fn: [fn(j, *everyone[j]) for j in range(len(everyone))]
    lanes = lambda pr: slice(pr * LANES, (pr + 1) * LANES)

    def independent_half(slot):
        op = lambda name, d, pr: prep_ref[slot, d, fields.index(name), :, lanes(pr)]
        rhs = each(lambda j, d, pr: jnp.concatenate([dup(op("bt", d, pr)), dup(op("kt", d, pr))], axis=0).astype(BF16))
        top = each(lambda j, d, pr: _bdot_nt(stack2(op("at", d, pr)), rhs[j]))
        yield
        n_all = each(lambda j, d, pr: jnp.where(masks[d][0], top[j][:, :c2], 0.0))
        aak_v = each(lambda j, d, pr: _bdot(jnp.where(masks[d][0], top[j][:, c2:], 0.0),
                                            dup(op("v", d, pr)).astype(BF16)))
        n_tr = each(lambda j, d, pr: n_all[j].T)
        nsub_tr = each(lambda j, d, pr: jnp.where(masks[1 - d][2], n_tr[j], 0.0))
        t_tr = each(lambda j, d, pr: eye + nsub_tr[j])
        p_tr = each(lambda j, d, pr: _bdot(nsub_tr[j], nsub_tr[j]))
        yield
        span = 2
        while 2 * span < RW_SUB:
            both = each(lambda j, d, pr: _bdot(p_tr[j], jnp.concatenate([t_tr[j], p_tr[j]], axis=1)))
            t_tr = each(lambda j, d, pr: t_tr[j] + both[j][:, :c2])
            p_tr = each(lambda j, d, pr: both[j][:, c2:])
            span *= 2
            yield
        t_tr = each(lambda j, d, pr: t_tr[j] + _bdot(p_tr[j], t_tr[j]))
        yield
        corr = each(lambda j, d, pr: _bdot(n_tr[j] - nsub_tr[j], t_tr[j]))
        bot = each(lambda j, d, pr: _bdot_nt(stack2(op("rt", d, pr)), rhs[j]))
        yield
        tinv_tr = each(lambda j, d, pr: t_tr[j] + _bdot(t_tr[j], corr[j]))
        for j, (d, pr) in enumerate(everyone):
            tinv_ref[slot, j] = tinv_tr[j].astype(BF16)
            aakv_ref[slot, j] = aak_v[j]
            arbk_ref[slot, j] = jnp.where(jnp.concatenate([masks[d][1], masks[d][1]], axis=1), bot[j], 0.0).astype(BF16)

    def dependent_half(slot, i):
        op = lambda name, d, pr: prep_ref[slot, d, fields.index(name), :, lanes(pr)]
        rows = [step_rows(i, d) for d in range(2)]
        s0 = each(lambda j, d, pr: st_ref[d, pr])
        proj = each(lambda j, d, pr: _bdot_nt(jnp.concatenate([op("at", d, pr), op("rt", d, pr)], axis=0), s0[j]))
        yield
        u_st = each(lambda j, d, pr: _bdot_tn(tinv_ref[slot, j], dup(proj[j][:c]) + aakv_ref[slot, j]))
        yield
        y_st = each(lambda j, d, pr: _bdot(arbk_ref[slot, j], jnp.concatenate(
            [u_st[j].astype(BF16), dup(op("v", d, pr)).astype(BF16)], axis=0)))
        upd = each(lambda j, d, pr: _bdot_tn(jnp.concatenate([unstack(u_st[j]), op("v", d, pr)], axis=0),
                                             jnp.concatenate([op("bt", d, pr), op("kt", d, pr)], axis=0)))
        yield
        for j, (d, pr) in enumerate(everyone):
            st_ref[d, pr] = (s0[j] + jnp.where(same_head, upd[j], 0.0)) * plast_ref[slot, d][:, lanes(pr)]
            y_acc[rows[d], lanes(pr)] += proj[j][c:] + unstack(y_st[j])
        for d in range(2):
            b_acc[rows[d], :] += prep_ref[slot, d, fields.index("bonus")]

    def interleave(main, staging):
        main = list(main)
        held = [[g, None] for g in staging]
        turn = 0
        while main:
            for entry in list(main):
                g, every = entry
                if turn % every == 0 and next(g, "done") == "done":
                    main.remove(entry)
            for s in held:
                if s[1] is None:
                    s[1] = next(s[0])
            turn += 1
        for g, _ in held:
            for _ in g:
                pass

    interleave([(prepare(0, d, 0), 1) for d in range(2)], [])
    interleave([(independent_half(0), 1)], [prepare(1, d, 1) for d in range(2)])

    def step(i, carry):
        slot = i % 2
        nxt = jnp.minimum(i + 1, n_steps - 1)
        interleave([(dependent_half(1 - slot, i - 1), 1), (independent_half(slot), 1)],
                   [prepare(nxt, d, 1 - slot) for d in range(2)])
        return carry

    lax.fori_loop(1, n_steps, step, 0)
    interleave([(dependent_half((n_steps - 1) % 2, n_steps - 1), 1)], [])

    def finish(i, carry):
        rows = pl.ds(pl.multiple_of(i * ROW_TILE, ROW_TILE), ROW_TILE)
        y = y_acc[rows, :]
        mean = _head_sum_mxu(y, ones_bd) * (1.0 / HD)
        yc = y - mean
        var = _head_sum_mxu(yc * yc, ones_bd) * (1.0 / HD)
        gn = yc * lax.rsqrt(var + RWKV_GN_EPS) * lnw_ref[...] + lnb_ref[...]
        o_ref[rows, :] = ((gn + b_acc[rows, :]) * _silu(z_ref[rows, :].astype(F32))).astype(o_ref.dtype)
        return carry

    lax.fori_loop(0, t_rows // ROW_TILE, finish, 0)


def _rwkv(u, n_lat_rows, col_shift, col_z, rw_mu, rw_w0, rw_w2, rw_a0, rw_a2, rw_kk, rw_ka, rw_rk, rw_ln_w,
          rw_ln_b):
    b, t, _ = u.shape
    c = RW_CHUNK
    assert c == HD
    col_lo = col_shift + 3 * W
    def pad_dirs(w2):
        z = jnp.zeros_like(w2[0])
        return jnp.stack([jnp.concatenate([w2[0], z], axis=0), jnp.concatenate([z, w2[1]], axis=0)]).astype(BF16)

    row = lambda a: a.reshape(1, W).astype(F32)
    blk = lambda col, width=W: pl.BlockSpec((None, t, width), lambda bi, col=col, width=width: (bi, 0, col // width))
    full = lambda shape: pl.BlockSpec(shape, lambda bi: (0,) * len(shape))
    return pl.pallas_call(
        functools.partial(_rwkv_kernel, n_lat=n_lat_rows // c, n_ctx=(t - n_lat_rows) // c),
        grid=(b,),
        in_specs=[
            blk(col_shift), blk(col_shift + W), blk(col_shift + 2 * W), blk(col_lo, 2 * LANES), blk(col_z),
            full((3, 1, W)), full((1, 2 * LANES)), full((1, W)), full((1, W)), full((2, 1, W)), full((2, 1, W)),
            full((2, 2 * LORA, W)), full((2, 2 * LORA, W)), full((1, W)), full((1, W)), full((1, W)),
        ],
        out_specs=pl.BlockSpec((None, t, W), lambda bi: (bi, 0, 0)),
        out_shape=jax.ShapeDtypeStruct((b, t, W), BF16),
        scratch_shapes=[pltpu.VMEM((t, W), F32), pltpu.VMEM((t, W), F32),
                        pltpu.VMEM((2, PAIRS, LANES, LANES), F32), pltpu.VMEM((2, 2, 6, c, W), F32),
                        pltpu.VMEM((2, 2, 1, W), F32), pltpu.VMEM((2, 2 * PAIRS, 2 * c, 2 * c), BF16),
                        pltpu.VMEM((2, 2 * PAIRS, 2 * c, LANES), F32), pltpu.VMEM((2, 2 * PAIRS, 2 * c, 4 * c), BF16)],
        compiler_params=_cparams("parallel"),
        name="rwkv7",
    )(u, u, u, u, u, rw_mu[:3 * W].reshape(3, 1, W), rw_mu[3 * W:].reshape(1, 2 * LANES), row(rw_kk), row(rw_ka),
      rw_w0.reshape(2, 1, W), rw_a0.reshape(2, 1, W), pad_dirs(rw_w2), pad_dirs(rw_a2), row(rw_rk), row(rw_ln_w),
      row(rw_ln_b))


def _out_proj_kernel(*refs, n_src, lat_tiles, tail, ctx_cols):
    a_ref, b_ref, w_ref, mod_ref = refs[n_src:n_src + 4]
    wa = a_ref.shape[-1]
    y = _bdot(a_ref[...], w_ref[:wa, :]) + _bdot(b_ref[...], w_ref[wa:, :])
    hn = _read_token_rows(refs[:n_src], lat_tiles) + mod_ref[2:3, :] * y
    if tail == "final_norm":
        fnw_ref, o_ref = refs[n_src + 4:]
        o_ref[...] = hn * lax.rsqrt(jnp.mean(hn * hn, axis=-1, keepdims=True) + NORM_EPS) * fnw_ref[...]
    else:
        nw_ref, mod2_ref, w2_ref, o_ref, u_ref = refs[n_src + 4:]
        o_ref[...] = hn
        yn = hn * lax.rsqrt(jnp.mean(hn * hn, axis=-1, keepdims=True) + NORM_EPS) * nw_ref[...]
        n = yn * (1.0 + mod2_ref[1:2, :]) + mod2_ref[0:1, :]
        is_ctx = pl.program_id(1) >= lat_tiles

        @pl.when(jnp.logical_not(is_ctx))
        def _():
            u_ref[...] = _bdot(n, w2_ref[...]).astype(u_ref.dtype)

        @pl.when(is_ctx)
        def _():
            lo, hi = ctx_cols
            u_ref[...] = jnp.zeros_like(u_ref)
            u_ref[:, lo:hi] = _bdot(n, w2_ref[:, lo:hi]).astype(u_ref.dtype)


def _out_proj(a, b, w, h, mod, n_lat, final_norm_w=None, next_proj=None):
    bsz, rows, wa = a.shape
    d = w.shape[1]
    tm = ROW_TILE
    lat_tiles = n_lat // tm
    src_specs, srcs = _token_rows(h, tm, lat_tiles)
    kind = lambda bi, i: (bi, (i >= lat_tiles).astype(jnp.int32), 0, 0)
    in_specs = src_specs + [
        pl.BlockSpec((None, tm, wa), lambda bi, i: (bi, i, 0)),
        pl.BlockSpec((None, tm, b.shape[-1]), lambda bi, i: (bi, i, 0)),
        pl.BlockSpec(w.shape, lambda bi, i: (0, 0)),
        pl.BlockSpec((None, None, 3, d), kind),
    ]
    args = srcs + [a, b, w, mod]
    out_specs = pl.BlockSpec((None, tm, d), lambda bi, i: (bi, i, 0))
    out_shape = jax.ShapeDtypeStruct((bsz, rows, d), F32)
    ctx_cols = None
    if final_norm_w is not None:
        tail = "final_norm"
        in_specs.append(pl.BlockSpec((1, d), lambda bi, i: (0, 0)))
        args.append(final_norm_w.reshape(1, d))
    else:
        tail = "next_proj"
        norm_w, mod2, w2, ctx_cols = next_proj
        n2 = w2.shape[1]
        in_specs += [pl.BlockSpec((1, d), lambda bi, i: (0, 0)), pl.BlockSpec((None, None, 3, d), kind),
                     pl.BlockSpec((d, n2), lambda bi, i: (0, 0))]
        args += [norm_w.reshape(1, d), mod2, w2]
        out_specs = [out_specs, pl.BlockSpec((None, tm, n2), lambda bi, i: (bi, i, 0))]
        out_shape = [out_shape, jax.ShapeDtypeStruct((bsz, rows, n2), BF16)]
    return pl.pallas_call(
        functools.partial(_out_proj_kernel, n_src=len(srcs), lat_tiles=lat_tiles, tail=tail, ctx_cols=ctx_cols),
        grid=(bsz, rows // tm),
        in_specs=in_specs,
        out_specs=out_specs,
        out_shape=out_shape,
        compiler_params=_cparams("parallel", "parallel"),
        name="out_proj_" + tail,
    )(*args)


HY_FAST = 64
HY_SUB = 8
HY_CH_TILE = 256
HY_UNROLL = 16


def _hy_dft_tables(l):
    n = 2 * l
    half = l // HY_FAST
    k1 = jnp.arange(half, dtype=jnp.int32)[None, :, None]
    n1 = jnp.arange(half, dtype=jnp.int32)[None, None, :]
    n2 = jnp.arange(HY_FAST, dtype=jnp.int32)[:, None, None]
    ang = (jnp.pi / n) * (((2 * k1 + 1) * (HY_FAST * n1 + n2)) % (2 * n)).astype(F32)
    m1 = jnp.concatenate([jnp.cos(ang), -jnp.sin(ang)], axis=1)
    groups = HY_FAST // HY_SUB
    m1g = m1.reshape(groups, HY_SUB, 2 * half, half)
    jj = np.arange(HY_SUB)[:, None, None]
    row_sel = (np.arange(2 * half * HY_SUB)[None, :, None] == np.arange(2 * half)[None, None, :] * HY_SUB + jj)
    col_sel = (np.arange(half * HY_SUB)[None, None, :] == np.arange(half)[None, :, None] * HY_SUB + jj)
    place = lambda out: jnp.einsum("jRr,gjrn,jnC->" + out, row_sel.astype(np.float32), m1g,
                                   col_sel.astype(np.float32), precision=lax.Precision.HIGHEST).astype(BF16)
    big, big_t = place("gRC"), place("gCR")
    kk = jnp.arange(HY_FAST, dtype=jnp.int32)
    ang2 = (2.0 * jnp.pi / HY_FAST) * ((kk[:, None] * kk[None, :]) % HY_FAST).astype(F32)
    wr, wi = jnp.cos(ang2), -jnp.sin(ang2)
    w2 = jnp.concatenate([jnp.concatenate([wr, -wi], axis=1), jnp.concatenate([wi, wr], axis=1)], axis=0)
    w2_inv = jnp.concatenate([jnp.concatenate([wr, wi], axis=1), jnp.concatenate([-wi, wr], axis=1)], axis=0)
    return big, big_t, w2.astype(BF16), w2_inv.astype(BF16)


def _hy_stage1(src_ref, m1_ref, bs_ref):
    half, _, c = src_ref.shape
    for g in range(HY_FAST // HY_SUB):
        cols = pl.ds(g * HY_SUB, HY_SUB)
        x = src_ref[:, cols, :].reshape(half * HY_SUB, c).astype(BF16)
        out = jnp.dot(m1_ref[g], x, preferred_element_type=F32)
        bs_ref[:, cols, :] = out.reshape(2 * half, HY_SUB, c)


def _hy_stage2(bs_ref, w2_ref, k1):
    half = bs_ref.shape[0] // 2
    return jnp.dot(w2_ref[...], jnp.concatenate([bs_ref[k1], bs_ref[half + k1]], axis=0).astype(BF16),
                   preferred_element_type=F32)


def _hy_filter_kernel(z_ref, w1_ref, b1_ref, f1_ref, w2_ref, b2_ref, f2_ref, w3_ref, dec_ref, o_ref, hid_ref):
    j = pl.program_id(0)

    @pl.when(j == 0)
    def _():
        hid = jnp.sin(f1_ref[...] * (_dot_hi(z_ref[...], w1_ref[...]) + b1_ref[...]))
        hid_ref[...] = jnp.sin(f2_ref[...] * (_dot_hi(hid, w2_ref[...]) + b2_ref[...]))

    h = _dot_hi(hid_ref[...], w3_ref[...]) * dec_ref[...]
    row = lax.broadcasted_iota(jnp.int32, h.shape, 0)
    o_ref[...] = jnp.where(jnp.logical_and(j % 2 == 1, row == 0), 0.0, h).astype(o_ref.dtype)


def _hy_filters(l, w1, b1, f1, w2, b2, f2, w3):
    t = jnp.linspace(0.0, 1.0, l)[:, None]
    bands = jnp.linspace(1e-4, HY_BANDS - 1, HY_BANDS)
    ang = (2.0 * math.pi / l) * jnp.arange(l, dtype=F32)[:, None] * bands[None]
    z = jnp.concatenate([t, jnp.cos(ang), -jnp.sin(ang)], axis=-1)
    emb = z.shape[1]
    emb_pad = 64
    z = jnp.pad(z, ((0, 0), (0, emb_pad - emb)))
    w1p = jnp.pad(w1, ((0, emb_pad - emb), (0, 0)))
    deltas = jnp.abs(jnp.linspace(math.log(HY_TARGET) / HY_SLOW_PCT, math.log(HY_TARGET) / HY_FAST_PCT, W))
    dec = jnp.exp(-t * deltas)
    ffn = w1.shape[1]
    nblk = w3.shape[1] // W
    full = lambda shape: pl.BlockSpec(shape, lambda j: (0,) * len(shape))
    return pl.pallas_call(
        _hy_filter_kernel,
        grid=(nblk,),
        in_specs=[full((l, emb_pad)), full((emb_pad, ffn)), full((1, ffn)), full((1, ffn)), full((ffn, ffn)),
                  full((1, ffn)), full((1, ffn)), pl.BlockSpec((ffn, W), lambda j: (0, j)), full((l, W))],
        out_specs=pl.BlockSpec((l, W), lambda j: (0, j)),
        out_shape=jax.ShapeDtypeStruct((l, nblk * W), F32),
        scratch_shapes=[pltpu.VMEM((l, ffn), F32)],
        compiler_params=_cparams("arbitrary"),
        name="hyena_filters",
    )(z, w1p, b1.reshape(1, -1), f1.reshape(1, -1), w2, b2.reshape(1, -1), f2.reshape(1, -1), w3, dec)


def _hy_spectrum_kernel(hf_ref, hb_ref, m1_ref, w2_ref, o_ref, bs_ref, xf_ref, *, half):
    _hy_stage1(hf_ref, m1_ref, bs_ref)

    def forward(k1, carry):
        xf_ref[k1] = _hy_stage2(bs_ref, w2_ref, k1)
        return carry

    lax.fori_loop(0, half, forward, 0, unroll=HY_UNROLL)
    _hy_stage1(hb_ref, m1_ref, bs_ref)

    def combine(k1, carry):
        xf = xf_ref[k1]
        xb = _hy_stage2(bs_ref, w2_ref, k1)
        o_ref[k1] = jnp.concatenate([xf[:HY_FAST] + xb[:HY_FAST], xf[HY_FAST:] - xb[HY_FAST:]],
                                    axis=0).astype(o_ref.dtype)
        return carry

    lax.fori_loop(0, half, combine, 0, unroll=HY_UNROLL)


def _hy_spectra(filt, m1, w2):
    l = filt.shape[0]
    half = l // HY_FAST
    ct = HY_CH_TILE
    nct = W // ct
    orders = filt.shape[1] // (2 * W)
    filt = filt.reshape(half, HY_FAST, filt.shape[1])
    return pl.pallas_call(
        functools.partial(_hy_spectrum_kernel, half=half),
        grid=(orders, nct),
        in_specs=[pl.BlockSpec((half, HY_FAST, ct), lambda o, c: (0, 0, 2 * o * nct + c)),
                  pl.BlockSpec((half, HY_FAST, ct), lambda o, c: (0, 0, (2 * o + 1) * nct + c)),
                  pl.BlockSpec(m1.shape, lambda o, c: (0, 0, 0)), pl.BlockSpec(w2.shape, lambda o, c: (0, 0))],
        out_specs=pl.BlockSpec((None, half, 2 * HY_FAST, ct), lambda o, c: (o, 0, 0, c)),
        out_shape=jax.ShapeDtypeStruct((orders, half, 2 * HY_FAST, W), BF16),
        scratch_shapes=[pltpu.VMEM((2 * half, HY_FAST, ct), F32), pltpu.VMEM((half, 2 * HY_FAST, ct), F32)],
        compiler_params=_cparams("parallel", "parallel"),
        name="hyena_spectra",
    )(filt, filt, m1, w2)


def _hyena_kernel(v_ref, x1_ref, x2_ref, g_ref, cw_ref, cb_ref, bias_ref, m1_ref, m1t_ref, w2_ref, w2i_ref, gs_ref,
                  o_ref, z_ref, bs_ref, cs_ref, acc_ref, *, l):
    o = pl.program_id(2)
    rc = ROW_TILE
    half = l // HY_FAST
    ct = z_ref.shape[-1]

    def conv_rows(ref, sec, c0):
        x, prev, nxt = _with_neighbours(ref, slice(None), c0, rc, l, l)
        w = cw_ref[sec]
        return prev * w[0:1, :] + x * w[1:2, :] + nxt * w[2:3, :] + cb_ref[sec]

    slabs = rc // HY_FAST

    @pl.when(o == 0)
    def _():
        def body(i, carry):
            c0 = pl.multiple_of(i * rc, rc)
            z_ref[pl.ds(i * slabs, slabs)] = conv_rows(v_ref, 0, c0).reshape(slabs, HY_FAST, ct)
            return carry

        lax.fori_loop(0, l // rc, body, 0)

    _hy_stage1(z_ref, m1_ref, bs_ref)

    def per_k1(k1, carry):
        x = _hy_stage2(bs_ref, w2_ref, k1)
        xr, xi = x[:HY_FAST], x[HY_FAST:]
        g = gs_ref[k1].astype(F32)
        gr, gi = g[:HY_FAST], g[HY_FAST:]
        y = jnp.concatenate([xr * gr - xi * gi, xr * gi + xi * gr], axis=0).astype(BF16)
        cs_ref[k1] = jnp.dot(w2i_ref[...], y, preferred_element_type=F32)
        return carry

    lax.fori_loop(0, half, per_k1, 0, unroll=HY_UNROLL)

    for g in range(HY_FAST // HY_SUB):
        c_re = cs_ref[:, pl.ds(g * HY_SUB, HY_SUB), :].reshape(half * HY_SUB, ct)
        c_im = cs_ref[:, pl.ds(HY_FAST + g * HY_SUB, HY_SUB), :].reshape(half * HY_SUB, ct)
        out = jnp.dot(m1t_ref[g], jnp.concatenate([c_re, c_im], axis=0).astype(BF16), preferred_element_type=F32)
        acc_ref[:, pl.ds(g * HY_SUB, HY_SUB), :] = out.reshape(half, HY_SUB, ct)

    def finish(i, carry):
        c0 = pl.multiple_of(i * rc, rc)
        rows = pl.ds(c0, rc)
        tile = pl.ds(i * slabs, slabs)
        conv = (acc_ref[tile] * (1.0 / l) + z_ref[tile] * bias_ref[o]).reshape(rc, ct)

        @pl.when(o == 0)
        def _():
            z_ref[tile] = (conv_rows(x1_ref, 1, c0) * conv).reshape(slabs, HY_FAST, ct)

        @pl.when(o == 1)
        def _():
            zn = conv_rows(x2_ref, 2, c0) * conv
            o_ref[rows, :] = (zn * _silu(g_ref[rows, :].astype(F32))).astype(o_ref.dtype)

        return carry

    lax.fori_loop(0, l // rc, finish, 0)


def _hyena(u, l, conv_w, conv_b, hy_bias, tables, spectra):
    b = u.shape[0]
    ct = HY_CH_TILE
    nct = W // ct
    half = l // HY_FAST
    sec = lambda s: pl.BlockSpec((None, l, ct), lambda bi, c, o, s=s: (bi, 0, s * nct + c))
    cw = jnp.transpose(conv_w.reshape(3, 3, W), (1, 0, 2))
    whole = lambda a: pl.BlockSpec(a.shape, lambda bi, c, o: (0,) * a.ndim)
    return pl.pallas_call(
        functools.partial(_hyena_kernel, l=l),
        grid=(b, nct, 2),
        in_specs=[
            sec(0), sec(1), sec(2), sec(3),
            pl.BlockSpec((3, 3, ct), lambda bi, c, o: (0, 0, c)),
            pl.BlockSpec((3, 1, ct), lambda bi, c, o: (0, 0, c)),
            pl.BlockSpec((2, 1, ct), lambda bi, c, o: (0, 0, c)),
            whole(tables[0]), whole(tables[1]), whole(tables[2]), whole(tables[3]),
            pl.BlockSpec((None, half, 2 * HY_FAST, ct), lambda bi, c, o: (o, 0, 0, c)),
        ],
        out_specs=pl.BlockSpec((None, l, ct), lambda bi, c, o: (bi, 0, c)),
        out_shape=jax.ShapeDtypeStruct((b, l, W), BF16),
        scratch_shapes=[pltpu.VMEM((half, HY_FAST, ct), F32), pltpu.VMEM((2 * half, HY_FAST, ct), F32),
                        pltpu.VMEM((half, 2 * HY_FAST, ct), F32), pltpu.VMEM((half, HY_FAST, ct), F32)],
        compiler_params=_cparams("parallel", "parallel", "arbitrary"),
        name="hyena",
    )(u, u, u, u, cw, conv_b.reshape(3, 1, W), hy_bias.reshape(2, 1, W), *tables, spectra)


NA_ROWS_PER_STEP = 8
NEG_BIG = -1e30


def _na_bias_tables(rpb, rows):
    heads = rpb.shape[0]
    n_dr, n_dc = 2 * NA_WIN_R - 1, 2 * NA_WIN_C - 1
    cq = np.arange(GRID_W)[:, None]
    kc = np.arange(GRID_W)[None, :]
    c0 = np.clip(cq - NA_WIN_C // 2, 0, GRID_W - NA_WIN_C)
    col_ok = (kc >= c0) & (kc < c0 + NA_WIN_C)
    col_sel = (np.arange(n_dc)[:, None, None] == (kc - cq + NA_WIN_C - 1)[None]).astype(np.float32)
    tiles = jnp.einsum("phrj,jcx->phrcx", rpb.astype(F32).reshape(heads // 2, 2, n_dr, n_dc), col_sel,
                       precision=lax.Precision.HIGHEST)
    tiles = jnp.where(col_ok, tiles, NEG_BIG)
    tiles = jnp.pad(tiles, ((0, 0), (0, 0), (0, 2), (0, 0), (0, 0)), constant_values=NEG_BIG)
    return jnp.concatenate([tiles[:, :, :n_dr + 1], tiles[:, :, 1:]], axis=-1)


def _na_kernel(q_ref, k_ref, v_ref, z_ref, bias_ref, o_ref, *, l):
    rows = l // GRID_W
    kn = NA_WIN_R * GRID_W
    n_ctx = k_ref.shape[0] - l
    lane_lo = _lane_ids((1, LANES)) < HD
    k_ctx = k_ref[pl.ds(l, n_ctx), :]
    v_ctx = v_ref[pl.ds(l, n_ctx), :]

    def body(it, carry):
        each = lambda fn: [fn(j, it * NA_ROWS_PER_STEP + j) for j in range(NA_ROWS_PER_STEP)]
        r0 = each(lambda j, rq: jnp.clip(rq - NA_WIN_R // 2, 0, rows - NA_WIN_R))
        qrows = each(lambda j, rq: pl.ds(pl.multiple_of(rq * GRID_W, GRID_W), GRID_W))
        krows = each(lambda j, rq: pl.ds(pl.multiple_of(r0[j] * GRID_W, GRID_W), kn))

        def stacked_q(j, rq):
            q = q_ref[qrows[j], :].astype(F32) * HD ** -0.5
            return jnp.concatenate([jnp.where(lane_lo, q, 0.0), jnp.where(lane_lo, 0.0, q)], axis=0).astype(BF16)

        def window_bias(j, rq):
            base = r0[j] - rq + NA_WIN_R - 1
            return jnp.concatenate(
                [jnp.concatenate([bias_ref[hh, base + w] for w in range(0, NA_WIN_R, 2)], axis=1) for hh in range(2)],
                axis=0)

        q2 = each(stacked_q)
        s_win = each(lambda j, rq: _bdot_nt(q2[j], k_ref[krows[j], :]) + window_bias(j, rq))
        s_ctx = each(lambda j, rq: _bdot_nt(q2[j], k_ctx))
        m = each(lambda j, rq: jnp.maximum(jnp.max(s_win[j], axis=-1, keepdims=True),
                                           jnp.max(s_ctx[j], axis=-1, keepdims=True)))
        p_win = each(lambda j, rq: jnp.exp(s_win[j] - m[j]))
        p_ctx = each(lambda j, rq: jnp.exp(s_ctx[j] - m[j]))
        den = each(lambda j, rq: jnp.sum(p_win[j], axis=-1, keepdims=True) + jnp.sum(p_ctx[j], axis=-1, keepdims=True))
        out = each(lambda j, rq: (_bdot(p_win[j], v_ref[krows[j], :]) + _bdot(p_ctx[j], v_ctx)) / den[j])
        for j in range(NA_ROWS_PER_STEP):
            gate = _silu(z_ref[qrows[j], :].astype(F32))
            o_ref[qrows[j], :] = (jnp.where(lane_lo, out[j][:GRID_W], out[j][GRID_W:]) * gate).astype(o_ref.dtype)
        return carry

    lax.fori_loop(0, rows // NA_ROWS_PER_STEP, body, 0)


def _natten(u, l, col_q, col_k, col_v, col_z, bias):
    b, t, _ = u.shape
    lat = lambda col: pl.BlockSpec((None, l, LANES), lambda bi, p, col=col: (bi, 0, col // LANES + p))
    full = lambda col: pl.BlockSpec((None, t, LANES), lambda bi, p, col=col: (bi, 0, col // LANES + p))
    return pl.pallas_call(
        functools.partial(_na_kernel, l=l),
        grid=(b, PAIRS),
        in_specs=[lat(col_q), full(col_k), full(col_v), lat(col_z),
                  pl.BlockSpec((None,) + bias.shape[1:], lambda bi, p: (p, 0, 0, 0, 0))],
        out_specs=pl.BlockSpec((None, l, LANES), lambda bi, p: (bi, 0, p)),
        out_shape=jax.ShapeDtypeStruct((b, l, W), BF16),
        compiler_params=_cparams("parallel", "parallel"),
        name="natten",
    )(u, u, u, u, bias)


def kernel(x, c, ctx, c_ctx, ada_w, ada_b, norm_w, final_norm_w, even_w_in, even_w_out, ret_decay, rw_mu, rw_w0, rw_w2, rw_a0, rw_a2, rw_kk, rw_ka, rw_rk, rw_ln_w, rw_ln_b, odd_w_in, odd_w_out, hy_conv_w, hy_conv_b, hy_w1, hy_b1, hy_f1, hy_w2, hy_b2, hy_f2, hy_w3, hy_bias, na_rpb):
    b, l, d = x.shape
    n_ctx = ctx.shape[1]
    t = l + n_ctx
    depth = ada_w.shape[0]
    assert depth == 2 and l % ROW_TILE == 0 and n_ctx % ROW_TILE == 0 and (l // GRID_W) >= NA_WIN_R
    assert (l // GRID_W) % NA_ROWS_PER_STEP == 0

    cond_rows = -(-(b + 1) // 8) * 8
    cond = jnp.concatenate([c, c_ctx[None, :], jnp.zeros((cond_rows - b - 1, d), F32)], axis=0)
    mod_all = _ada_mod(cond, ada_w, ada_b)
    mods = [jnp.stack([mod_all[i, :b].reshape(b, 3, d),
                       jnp.broadcast_to(mod_all[i, b].reshape(1, 3, d), (b, 3, d))], axis=1) for i in range(depth)]

    h = (x, ctx)

    col_rz, col_wz, col_shift = 3 * W, 4 * W, 5 * W
    u0 = _norm_proj(h, norm_w[0], mods[0], even_w_in[0].astype(BF16), l, t)
    cos_t, sin_t = _rope_tables(l, t)
    ret_o = _retention(u0, cos_t, sin_t, ret_decay[0], l, 0, W, 2 * W, col_rz)
    rw_o = _rwkv(u0, l, col_shift, col_wz, rw_mu[0], rw_w0[0], rw_w2[0], rw_a0[0], rw_a2[0], rw_kk[0], rw_ka[0],
                 rw_rk[0], rw_ln_w[0], rw_ln_b[0])
    h, u1 = _out_proj(ret_o, rw_o, even_w_out[0].astype(BF16), h, mods[0], l,
                      next_proj=(norm_w[1], mods[1], odd_w_in[0].astype(BF16), (5 * W, 7 * W)))
    tables = _hy_dft_tables(l)
    filt = _hy_filters(l, hy_w1[0], hy_b1[0], hy_f1[0], hy_w2[0], hy_b2[0], hy_f2[0], hy_w3[0])
    spectra = _hy_spectra(filt, tables[0], tables[2])
    hy_o = _hyena(u1, l, hy_conv_w[0], hy_conv_b[0], hy_bias[0], tables, spectra)
    na_o = _natten(u1, l, 4 * W, 5 * W, 6 * W, 7 * W, _na_bias_tables(na_rpb[0], l // GRID_W))
    return _out_proj(hy_o, na_o, odd_w_out[0].astype(BF16), h, mods[1], l, final_norm_w)
```

```python
import functools
import math

import jax
import jax.numpy as jnp
import numpy as np
from jax import lax
from jax.experimental import pallas as pl
from jax.experimental.pallas import tpu as pltpu

F32 = jnp.float32
BF16 = jnp.bfloat16

GRID_W = 64
NORM_EPS = 1e-6
ROPE_BASE = 10000.0
HEADS = 8
HD = 64
W = HEADS * HD
LORA = 64
RWKV_GN_EPS = 64e-5
HY_BANDS = 16
HY_TARGET = 1e-2
HY_FAST_PCT = 0.3
HY_SLOW_PCT = 1.5
NA_WIN_R = 8
NA_WIN_C = 16
LANES = 128
PAIRS = W // LANES
RET_CHUNK = 128
RET_STEPS = 2
RW_CHUNK = 64
RW_SUB = 32
ROW_TILE = 256
HALO = 16
VMEM_LIMIT = 56 * 1024 * 1024


def _cparams(*sem):
    return pltpu.CompilerParams(dimension_semantics=sem, vmem_limit_bytes=VMEM_LIMIT)


def _bdot(a, b):
    return jnp.dot(a.astype(BF16), b.astype(BF16), preferred_element_type=F32)


def _bdot_nt(a, b):
    return lax.dot_general(a.astype(BF16), b.astype(BF16), (((1,), (1,)), ((), ())), preferred_element_type=F32)


def _bdot_tn(a, b):
    return lax.dot_general(a.astype(BF16), b.astype(BF16), (((0,), (0,)), ((), ())), preferred_element_type=F32)


def _dot_hi(a, b):
    ah = a.astype(BF16)
    al = (a - ah.astype(F32)).astype(BF16)
    bh = b.astype(BF16)
    bl = (b - bh.astype(F32)).astype(BF16)
    d = lambda p, q: jnp.dot(p, q, preferred_element_type=F32)
    return d(ah, bh) + d(ah, bl) + d(al, bh)


def _silu(x):
    return x * (1.0 / (1.0 + jnp.exp(-x)))


def _sigmoid(x):
    return 1.0 / (1.0 + jnp.exp(-x))


def _lane_ids(shape):
    return lax.broadcasted_iota(jnp.int32, shape, len(shape) - 1)


def _row_ids(shape):
    return lax.broadcasted_iota(jnp.int32, shape, len(shape) - 2)


def _head_sum_mxu(x, ones_bd):
    hi = x.astype(BF16)
    lo = (x - hi.astype(F32)).astype(BF16)
    tiles = []
    for p in range(x.shape[-1] // LANES):
        sl = slice(p * LANES, (p + 1) * LANES)
        tiles.append(jnp.dot(hi[:, sl], ones_bd, preferred_element_type=F32)
                     + jnp.dot(lo[:, sl], ones_bd, preferred_element_type=F32))
    return tiles[0] if len(tiles) == 1 else jnp.concatenate(tiles, axis=-1)


def _with_neighbours(ref, cols, c0, c, n_rows, lat_rows):
    row = lax.broadcasted_iota(jnp.int32, (c, 1), 0)
    x = ref[pl.ds(c0, c), cols].astype(F32)
    p0 = pl.multiple_of(jnp.maximum(c0 - HALO, 0), HALO)
    n0 = pl.multiple_of(jnp.minimum(c0 + c, n_rows - HALO), HALO)
    prev_row = ref[pl.ds(p0, HALO), cols].astype(F32)[HALO - 1:HALO, :]
    next_row = ref[pl.ds(n0, HALO), cols].astype(F32)[0:1, :]
    has_prev = jnp.logical_and(c0 != 0, c0 != lat_rows)
    has_next = jnp.logical_and(c0 + c != lat_rows, c0 + c != n_rows)
    prev_row = jnp.where(has_prev, prev_row, 0.0)
    next_row = jnp.where(has_next, next_row, 0.0)
    prev = jnp.where(row == 0, prev_row, pltpu.roll(x, 1, 0))
    nxt = jnp.where(row == c - 1, next_row, pltpu.roll(x, c - 1, 0))
    return x, prev, nxt


def _scan_chunk_id(i, n_lat, n_ctx, reverse):
    if reverse:
        return jnp.where(i < n_ctx, n_lat + n_ctx - 1 - i, n_lat - 1 - (i - n_ctx))
    return jnp.where(i < n_ctx, n_lat + i, i - n_ctx)


def _ada_kernel(c_ref, w_ref, b_ref, o_ref):
    o_ref[...] = _dot_hi(_silu(c_ref[...]), w_ref[...]) + b_ref[...]


def _ada_mod(cond, ada_w, ada_b):
    depth, d, d3 = ada_w.shape
    rows = cond.shape[0]
    return pl.pallas_call(
        _ada_kernel,
        grid=(depth, d3 // d),
        in_specs=[
            pl.BlockSpec((rows, d), lambda i, j: (0, 0)),
            pl.BlockSpec((None, d, d), lambda i, j: (i, 0, j)),
            pl.BlockSpec((None, 1, d), lambda i, j: (i, 0, j)),
        ],
        out_specs=pl.BlockSpec((None, rows, d), lambda i, j: (i, 0, j)),
        out_shape=jax.ShapeDtypeStruct((depth, rows, d3), F32),
        compiler_params=_cparams("parallel", "parallel"),
        name="ada_mod",
    )(cond, ada_w, ada_b.reshape(depth, 1, d3))


def _token_rows(h, tm, lat_tiles):
    if isinstance(h, tuple):
        d = h[0].shape[-1]
        return [pl.BlockSpec((None, tm, d), lambda bi, i: (bi, jnp.minimum(i, lat_tiles - 1), 0)),
                pl.BlockSpec((None, tm, d), lambda bi, i: (bi, jnp.maximum(i - lat_tiles, 0), 0))], list(h)
    return [pl.BlockSpec((None, tm, h.shape[-1]), lambda bi, i: (bi, i, 0))], [h]


def _read_token_rows(refs, lat_tiles):
    if len(refs) == 2:
        return jnp.where(pl.program_id(1) >= lat_tiles, refs[1][...], refs[0][...])
    return refs[0][...]


def _norm_proj_kernel(*refs, n_src, lat_tiles):
    nw_ref, mod_ref, w_ref, o_ref = refs[n_src:]
    x = _read_token_rows(refs[:n_src], lat_tiles)
    y = x * lax.rsqrt(jnp.mean(x * x, axis=-1, keepdims=True) + NORM_EPS) * nw_ref[...]
    n = y * (1.0 + mod_ref[1:2, :]) + mod_ref[0:1, :]
    o_ref[...] = _bdot(n, w_ref[...]).astype(o_ref.dtype)


def _norm_proj(h, norm_w, mod, w, n_lat, t):
    d, n = w.shape
    b = mod.shape[0]
    tm = ROW_TILE
    lat_tiles = n_lat // tm
    src_specs, srcs = _token_rows(h, tm, lat_tiles)
    return pl.pallas_call(
        functools.partial(_norm_proj_kernel, n_src=len(srcs), lat_tiles=lat_tiles),
        grid=(b, t // tm),
        in_specs=src_specs + [
            pl.BlockSpec((1, d), lambda bi, i: (0, 0)),
            pl.BlockSpec((None, None, 3, d), lambda bi, i: (bi, (i >= lat_tiles).astype(jnp.int32), 0, 0)),
            pl.BlockSpec((d, n), lambda bi, i: (0, 0)),
        ],
        out_specs=pl.BlockSpec((None, tm, n), lambda bi, i: (bi, i, 0)),
        out_shape=jax.ShapeDtypeStruct((b, t, n), BF16),
        compiler_params=_cparams("parallel", "parallel"),
        name="norm_proj",
    )(*srcs, norm_w.reshape(1, d), mod, w)


def _ret_kernel(q_ref, k_ref, v_ref, z_ref, cos_ref, sin_ref, rd_ref, o_ref, y_acc, st_ref, dm_ref, qd_ref, kd_ref,
                *, n_lat, n_ctx):
    c = RET_CHUNK
    lane = _lane_ids((1, LANES))
    lane_lo = lane < HD
    rope_lo = (lane % HD) < (HD // 2)
    pos = lax.broadcasted_iota(jnp.int32, (c, 1), 0).astype(F32)
    ti = lax.broadcasted_iota(jnp.int32, (c, c), 0)
    si = lax.broadcasted_iota(jnp.int32, (c, c), 1)
    same_head = (_row_ids((LANES, LANES)) < HD) == (_lane_ids((LANES, LANES)) < HD)
    ones_bd = same_head.astype(BF16)
    inst = [(d, pr) for d in range(2) for pr in range(PAIRS)]
    sl = lambda pr: slice(pr * LANES, (pr + 1) * LANES)

    g_chunk = []
    for d, pr in inst:
        lg_a = -jnp.exp(rd_ref[d, 2 * pr])
        lg_b = -jnp.exp(rd_ref[d, 2 * pr + 1])
        lgv = jnp.where(lane_lo, lg_a, lg_b)
        diff = ((ti - si) if d == 0 else (si - ti)).astype(F32)
        keep = diff >= 0
        dmat = lambda lg: jnp.where(keep, jnp.exp(jnp.where(keep, diff, 0.0) * lg), 0.0)
        dm_ref[d, pr] = jnp.concatenate([dmat(lg_a), dmat(lg_b)], axis=0)
        qd_ref[d, :, sl(pr)] = jnp.exp(((pos + 1.0) if d == 0 else (c - pos)) * lgv)
        kd_ref[d, :, sl(pr)] = jnp.exp(((c - 1.0 - pos) if d == 0 else pos) * lgv)
        g_chunk.append(jnp.exp(c * lgv))
    y_acc[...] = jnp.zeros_like(y_acc)
    st_ref[...] = jnp.zeros_like(st_ref)

    def rope(x, rows):
        partner = jnp.where(rope_lo, pltpu.roll(x, LANES - HD // 2, 1), pltpu.roll(x, HD // 2, 1))
        return x * cos_ref[rows, :] + partner * sin_ref[rows, :]

    def stack2(x):
        return jnp.concatenate([jnp.where(lane_lo, x, 0.0), jnp.where(lane_lo, 0.0, x)], axis=0)

    steps = max(s for s in range(1, RET_STEPS + 1) if (n_lat + n_ctx) % s == 0)

    def step(i, carry):
        work = [(sub,) + dp for sub in range(steps) for dp in inst]
        rows = [[pl.ds(pl.multiple_of(_scan_chunk_id(steps * i + sub, n_lat, n_ctx, d == 1) * c, c), c)
                 for d in range(2)] for sub in range(steps)]
        each = lambda fn: [fn(j, *work[j]) for j in range(len(work))]
        q = each(lambda j, sub, d, pr: rope(q_ref[rows[sub][d], sl(pr)].astype(F32), rows[sub][d]))
        k = each(lambda j, sub, d, pr: rope(k_ref[rows[sub][d], sl(pr)].astype(F32) * HD ** -0.5, rows[sub][d]))
        v = each(lambda j, sub, d, pr: v_ref[rows[sub][d], sl(pr)])
        scores = each(lambda j, sub, d, pr: _bdot_nt(stack2(q[j]), k[j]) * dm_ref[d, pr])
        upd = each(lambda j, sub, d, pr: _bdot_tn(k[j] * kd_ref[d, :, sl(pr)], v[j]))
        intra = each(lambda j, sub, d, pr: _bdot(scores[j], v[j]))
        state = [st_ref[d, pr] for d, pr in inst]
        for sub in range(steps):
            for n, (d, pr) in enumerate(inst):
                j = sub * len(inst) + n
                cross = _bdot(q[j] * qd_ref[d, :, sl(pr)], state[n])
                state[n] = state[n] * g_chunk[n] + jnp.where(same_head, upd[j], 0.0)
                y_acc[rows[sub][d], sl(pr)] += jnp.where(lane_lo, intra[j][:c], intra[j][c:]) + cross
        for n, (d, pr) in enumerate(inst):
            st_ref[d, pr] = state[n]
        return carry

    lax.fori_loop(0, (n_lat + n_ctx) // steps, step, 0)

    def finish(i, carry):
        rows = pl.ds(pl.multiple_of(i * ROW_TILE, ROW_TILE), ROW_TILE)
        y = y_acc[rows, :]
        ms = _head_sum_mxu(y * y, ones_bd) * (1.0 / HD)
        o_ref[rows, :] = (y * lax.rsqrt(ms + NORM_EPS) * _silu(z_ref[rows, :].astype(F32))).astype(o_ref.dtype)
        return carry

    lax.fori_loop(0, (n_lat + n_ctx) * c // ROW_TILE, finish, 0)


def _retention(u, cos_t, sin_t, ret_decay, n_lat_rows, col_q, col_k, col_v, col_z):
    b, t, _ = u.shape
    c = RET_CHUNK
    rd = jnp.broadcast_to(ret_decay.astype(F32)[:, :, None, None], (2, HEADS, 1, LANES))
    blk = lambda col: pl.BlockSpec((None, t, W), lambda bi, col=col: (bi, 0, col // W))
    return pl.pallas_call(
        functools.partial(_ret_kernel, n_lat=n_lat_rows // c, n_ctx=(t - n_lat_rows) // c),
        grid=(b,),
        in_specs=[
            blk(col_q), blk(col_k), blk(col_v), blk(col_z),
            pl.BlockSpec((t, LANES), lambda bi: (0, 0)),
            pl.BlockSpec((t, LANES), lambda bi: (0, 0)),
            pl.BlockSpec((2, HEADS, 1, LANES), lambda bi: (0, 0, 0, 0)),
        ],
        out_specs=pl.BlockSpec((None, t, W), lambda bi: (bi, 0, 0)),
        out_shape=jax.ShapeDtypeStruct((b, t, W), BF16),
        scratch_shapes=[pltpu.VMEM((t, W), F32), pltpu.VMEM((2, PAIRS, LANES, LANES), F32),
                        pltpu.VMEM((2, PAIRS, 2 * c, c), F32), pltpu.VMEM((2, c, W), F32),
                        pltpu.VMEM((2, c, W), F32)],
        compiler_params=_cparams("parallel"),
        name="retention",
    )(u, u, u, u, cos_t, sin_t, rd)


def _rope_tables(n_lat_rows, t):
    pos = jnp.arange(n_lat_rows)
    row = (pos // GRID_W).astype(F32)
    col = (pos % GRID_W).astype(F32)
    nf = HD // 4
    inv = ROPE_BASE ** (-jnp.arange(nf, dtype=F32) / nf)
    ang = jnp.concatenate([row[:, None] * inv, col[:, None] * inv], axis=-1)
    cos, sin = jnp.cos(ang), jnp.sin(ang)
    cos_h = jnp.concatenate([cos, cos], axis=-1)
    sin_h = jnp.concatenate([-sin, sin], axis=-1)
    reps = LANES // HD
    cos_t = jnp.concatenate([jnp.tile(cos_h, (1, reps)), jnp.ones((t - n_lat_rows, LANES), F32)], axis=0)
    sin_t = jnp.concatenate([jnp.tile(sin_h, (1, reps)), jnp.zeros((t - n_lat_rows, LANES), F32)], axis=0)
    return cos_t, sin_t


def _rwkv_kernel(r_ref, k_ref, v_ref, lo_ref, z_ref, mu_ref, mulo_ref, kkw_ref, ka_ref, w0_ref, a0_ref, w2_ref,
                 a2_ref, rk_ref, lnw_ref, lnb_ref, o_ref, y_acc, b_acc, st_ref, prep_ref, plast_ref, tinv_ref, aakv_ref,
                 arbk_ref, *, n_lat, n_ctx):
    c = RW_CHUNK
    c2 = 2 * c
    t_rows = (n_lat + n_ctx) * c
    lat_rows = n_lat * c
    lane_lo = _lane_ids((1, LANES)) < HD
    ri = _row_ids((c2, c2))
    ci = _lane_ids((c2, c2))
    same_head = (ri < c) == (ci < c)
    rt_, ct_ = ri % c, ci % c
    same_sub = (rt_ // RW_SUB) == (ct_ // RW_SUB)
    eye = (ri == ci).astype(F32)
    ones_bd = same_head.astype(BF16)
    t64 = lax.broadcasted_iota(jnp.int32, (c, c), 0)
    s64 = lax.broadcasted_iota(jnp.int32, (c, c), 1)
    masks = []
    for d in range(2):
        before = (ct_ < rt_) if d == 0 else (ct_ > rt_)
        strict = jnp.logical_and(same_head, before)
        incl = jnp.logical_and(same_head, jnp.logical_or(before, ct_ == rt_))
        tri = ((s64 <= t64) if d == 0 else (s64 >= t64)).astype(BF16)
        masks.append((strict, incl, jnp.logical_and(strict, same_sub), tri))

    y_acc[...] = jnp.zeros_like(y_acc)
    b_acc[...] = jnp.zeros_like(b_acc)
    st_ref[...] = jnp.zeros_like(st_ref)

    def stack2(x):
        return jnp.concatenate([jnp.where(lane_lo, x, 0.0), jnp.where(lane_lo, 0.0, x)], axis=0)

    def dup(x):
        return jnp.concatenate([x, x], axis=0)

    def unstack(x):
        return jnp.where(lane_lo, x[:c], x[c:])

    sh_t = lax.broadcasted_iota(jnp.int32, (c, c + 2 * HALO), 0) + HALO
    sh_s = lax.broadcasted_iota(jnp.int32, (c, c + 2 * HALO), 1)
    shift_base = jnp.where(jnp.abs(sh_s - sh_t) == 1, 0.5, 0.0)

    n_steps = n_lat + n_ctx
    fields = ("at", "bt", "kt", "rt", "v", "bonus")

    def step_rows(i, d):
        return pl.ds(pl.multiple_of(_scan_chunk_id(i, n_lat, n_ctx, d == 1) * c, c), c)

    def prepare(i, d, slot):
        tri = masks[d][3]
        c0 = pl.multiple_of(_scan_chunk_id(i, n_lat, n_ctx, d == 1) * c, c)
        p0 = pl.multiple_of(jnp.maximum(c0 - HALO, 0), HALO)
        n0 = pl.multiple_of(jnp.minimum(c0 + c, t_rows - HALO), HALO)
        no_prev = jnp.logical_or(c0 == 0, c0 == lat_rows)
        no_next = jnp.logical_or(c0 + c == lat_rows, c0 + c == t_rows)
        dead_prev = jnp.where(no_prev, HALO - 1, -1)
        dead_next = jnp.where(no_next, HALO + c, -1)
        dead = jnp.logical_or(sh_s == dead_prev, sh_s == dead_next)
        shift = jnp.where(dead, 0.0, shift_base).astype(BF16)

        def mixed(ref, mu):
            x = ref[pl.ds(c0, c), :]
            ext = jnp.concatenate([ref[pl.ds(p0, HALO), :], x, ref[pl.ds(n0, HALO), :]], axis=0)
            x = x.astype(F32)
            return x + mu * (jnp.dot(shift, ext, preferred_element_type=F32) - x)

        r = mixed(r_ref, mu_ref[0])
        k = mixed(k_ref, mu_ref[1])
        v = mixed(v_ref, mu_ref[2])
        lo = mixed(lo_ref, mulo_ref[...])
        kk = k * kkw_ref[...]
        yield
        kk_norm = jnp.sqrt(_head_sum_mxu(kk * kk, ones_bd))
        w_pre = _bdot(jnp.tanh(lo[:, :LANES]), w2_ref[d])
        a_pre = _bdot(lo[:, LANES:], a2_ref[d])
        yield
        kk = kk / jnp.maximum(kk_norm, 1e-12)
        logw = -math.exp(-0.5) * _sigmoid(w0_ref[d] + w_pre)
        a = _sigmoid(a0_ref[d] + a_pre)
        k_d = k * (1.0 + (a - 1.0) * ka_ref[...])
        lw_hi = logw.astype(BF16)
        lw_lo = (logw - lw_hi.astype(F32)).astype(BF16)
        cum = jnp.dot(tri, lw_hi, preferred_element_type=F32) + jnp.dot(tri, lw_lo, preferred_element_type=F32)
        bonus_sum = jnp.concatenate(
            [_bdot((r * k_d * rk_ref[...])[:, pr * LANES:(pr + 1) * LANES], ones_bd) for pr in range(PAIRS)], axis=1)
        yield
        p = jnp.exp(cum)
        inv_p = jnp.exp(-cum)
        staged = dict(at=-kk * jnp.exp(cum - logw), bt=kk * a * inv_p, kt=k_d * inv_p, rt=r * p, v=v,
                      bonus=bonus_sum * v)
        yield "stores next"
        for f, name in enumerate(fields):
            prep_ref[slot, d, f] = staged[name]
        plast_ref[slot, d] = p[c - 1:c, :] if d == 0 else p[0:1, :]

    everyone = [(d, pr) for d in range(2) for pr in range(PAIRS)]
    each = lambda fn: [fn(j, *everyone[j]) for j in range(len(everyone))]
    lanes = lambda pr: slice(pr * LANES, (pr + 1) * LANES)

    def independent_half(slot):
        op = lambda name, d, pr: prep_ref[slot, d, fields.index(name), :, lanes(pr)]
        rhs = each(lambda j, d, pr: jnp.concatenate([dup(op("bt", d, pr)), dup(op("kt", d, pr))], axis=0).astype(BF16))
        top = each(lambda j, d, pr: _bdot_nt(stack2(op("at", d, pr)), rhs[j]))
        yield
        n_all = each(lambda j, d, pr: jnp.where(masks[d][0], top[j][:, :c2], 0.0))
        aak_v = each(lambda j, d, pr: _bdot(jnp.where(masks[d][0], top[j][:, c2:], 0.0),
                                            dup(op("v", d, pr)).astype(BF16)))
        n_tr = each(lambda j, d, pr: n_all[j].T)
        nsub_tr = each(lambda j, d, pr: jnp.where(masks[1 - d][2], n_tr[j], 0.0))
        t_tr = each(lambda j, d, pr: eye + nsub_tr[j])
        p_tr = each(lambda j, d, pr: _bdot(nsub_tr[j], nsub_tr[j]))
        yield
        span = 2
        while 2 * span < RW_SUB:
            both = each(lambda j, d, pr: _bdot(p_tr[j], jnp.concatenate([t_tr[j], p_tr[j]], axis=1)))
            t_tr = each(lambda j, d, pr: t_tr[j] + both[j][:, :c2])
            p_tr = each(lambda j, d, pr: both[j][:, c2:])
            span *= 2
            yield
        t_tr = each(lambda j, d, pr: t_tr[j] + _bdot(p_tr[j], t_tr[j]))
        yield
        corr = each(lambda j, d, pr: _bdot(n_tr[j] - nsub_tr[j], t_tr[j]))
        bot = each(lambda j, d, pr: _bdot_nt(stack2(op("rt", d, pr)), rhs[j]))
        yield
        tinv_tr = each(lambda j, d, pr: t_tr[j] + _bdot(t_tr[j], corr[j]))
        for j, (d, pr) in enumerate(everyone):
            tinv_ref[slot, j] = tinv_tr[j].astype(BF16)
            aakv_ref[slot, j] = aak_v[j]
            arbk_ref[slot, j] = jnp.where(jnp.concatenate([masks[d][1], masks[d][1]], axis=1), bot[j], 0.0).astype(BF16)

    def dependent_half(slot, i):
        op = lambda name, d, pr: prep_ref[slot, d, fields.index(name), :, lanes(pr)]
        rows = [step_rows(i, d) for d in range(2)]
        s0 = each(lambda j, d, pr: st_ref[d, pr])
        proj = each(lambda j, d, pr: _bdot_nt(jnp.concatenate([op("at", d, pr), op("rt", d, pr)], axis=0), s0[j]))
        yield
        u_st = each(lambda j, d, pr: _bdot_tn(tinv_ref[slot, j], dup(proj[j][:c]) + aakv_ref[slot, j]))
        yield
        y_st = each(lambda j, d, pr: _bdot(arbk_ref[slot, j], jnp.concatenate(
            [u_st[j].astype(BF16), dup(op("v", d, pr)).astype(BF16)], axis=0)))
        upd = each(lambda j, d, pr: _bdot_tn(jnp.concatenate([unstack(u_st[j]), op("v", d, pr)], axis=0),
                                             jnp.concatenate([op("bt", d, pr), op("kt", d, pr)], axis=0)))
        yield
        for j, (d, pr) in enumerate(everyone):
            st_ref[d, pr] = (s0[j] + jnp.where(same_head, upd[j], 0.0)) * plast_ref[slot, d][:, lanes(pr)]
            y_acc[rows[d], lanes(pr)] += proj[j][c:] + unstack(y_st[j])
        for d in range(2):
            b_acc[rows[d], :] += prep_ref[slot, d, fields.index("bonus")]

    def interleave(main, staging):
        main = list(main)
        held = [[g, None] for g in staging]
        turn = 0
        while main:
            for entry in list(main):
                g, every = entry
                if turn % every == 0 and next(g, "done") == "done":
                    main.remove(entry)
            for s in held:
                if s[1] is None:
                    s[1] = next(s[0])
            turn += 1
        for g, _ in held:
            for _ in g:
                pass

    interleave([(prepare(0, d, 0), 1) for d in range(2)], [])
    interleave([(independent_half(0), 1)], [prepare(1, d, 1) for d in range(2)])

    def step(i, carry):
        slot = i % 2
        nxt = jnp.minimum(i + 1, n_steps - 1)
        interleave([(dependent_half(1 - slot, i - 1), 1), (independent_half(slot), 1)],
                   [prepare(nxt, d, 1 - slot) for d in range(2)])
        return carry

    lax.fori_loop(1, n_steps, step, 0)
    interleave([(dependent_half((n_steps - 1) % 2, n_steps - 1), 1)], [])

    def finish(i, carry):
        rows = pl.ds(pl.multiple_of(i * ROW_TILE, ROW_TILE), ROW_TILE)
        y = y_acc[rows, :]
        mean = _head_sum_mxu(y, ones_bd) * (1.0 / HD)
        yc = y - mean
        var = _head_sum_mxu(yc * yc, ones_bd) * (1.0 / HD)
        gn = yc * lax.rsqrt(var + RWKV_GN_EPS) * lnw_ref[...] + lnb_ref[...]
        o_ref[rows, :] = ((gn + b_acc[rows, :]) * _silu(z_ref[rows, :].astype(F32))).astype(o_ref.dtype)
        return carry

    lax.fori_loop(0, t_rows // ROW_TILE, finish, 0)


def _rwkv(u, n_lat_rows, col_shift, col_z, rw_mu, rw_w0, rw_w2, rw_a0, rw_a2, rw_kk, rw_ka, rw_rk, rw_ln_w,
          rw_ln_b):
    b, t, _ = u.shape
    c = RW_CHUNK
    assert c == HD
    col_lo = col_shift + 3 * W
    def pad_dirs(w2):
        z = jnp.zeros_like(w2[0])
        return jnp.stack([jnp.concatenate([w2[0], z], axis=0), jnp.concatenate([z, w2[1]], axis=0)]).astype(BF16)

    row = lambda a: a.reshape(1, W).astype(F32)
    blk = lambda col, width=W: pl.BlockSpec((None, t, width), lambda bi, col=col, width=width: (bi, 0, col // width))
    full = lambda shape: pl.BlockSpec(shape, lambda bi: (0,) * len(shape))
    return pl.pallas_call(
        functools.partial(_rwkv_kernel, n_lat=n_lat_rows // c, n_ctx=(t - n_lat_rows) // c),
        grid=(b,),
        in_specs=[
            blk(col_shift), blk(col_shift + W), blk(col_shift + 2 * W), blk(col_lo, 2 * LANES), blk(col_z),
            full((3, 1, W)), full((1, 2 * LANES)), full((1, W)), full((1, W)), full((2, 1, W)), full((2, 1, W)),
            full((2, 2 * LORA, W)), full((2, 2 * LORA, W)), full((1, W)), full((1, W)), full((1, W)),
        ],
        out_specs=pl.BlockSpec((None, t, W), lambda bi: (bi, 0, 0)),
        out_shape=jax.ShapeDtypeStruct((b, t, W), BF16),
        scratch_shapes=[pltpu.VMEM((t, W), F32), pltpu.VMEM((t, W), F32),
                        pltpu.VMEM((2, PAIRS, LANES, LANES), F32), pltpu.VMEM((2, 2, 6, c, W), F32),
                        pltpu.VMEM((2, 2, 1, W), F32), pltpu.VMEM((2, 2 * PAIRS, 2 * c, 2 * c), BF16),
                        pltpu.VMEM((2, 2 * PAIRS, 2 * c, LANES), F32), pltpu.VMEM((2, 2 * PAIRS, 2 * c, 4 * c), BF16)],
        compiler_params=_cparams("parallel"),
        name="rwkv7",
    )(u, u, u, u, u, rw_mu[:3 * W].reshape(3, 1, W), rw_mu[3 * W:].reshape(1, 2 * LANES), row(rw_kk), row(rw_ka),
      rw_w0.reshape(2, 1, W), rw_a0.reshape(2, 1, W), pad_dirs(rw_w2), pad_dirs(rw_a2), row(rw_rk), row(rw_ln_w),
      row(rw_ln_b))


def _out_proj_kernel(*refs, n_src, lat_tiles, tail, ctx_cols):
    a_ref, b_ref, w_ref, mod_ref = refs[n_src:n_src + 4]
    wa = a_ref.shape[-1]
    y = _bdot(a_ref[...], w_ref[:wa, :]) + _bdot(b_ref[...], w_ref[wa:, :])
    hn = _read_token_rows(refs[:n_src], lat_tiles) + mod_ref[2:3, :] * y
    if tail == "final_norm":
        fnw_ref, o_ref = refs[n_src + 4:]
        o_ref[...] = hn * lax.rsqrt(jnp.mean(hn * hn, axis=-1, keepdims=True) + NORM_EPS) * fnw_ref[...]
    else:
        nw_ref, mod2_ref, w2_ref, o_ref, u_ref = refs[n_src + 4:]
        o_ref[...] = hn
        yn = hn * lax.rsqrt(jnp.mean(hn * hn, axis=-1, keepdims=True) + NORM_EPS) * nw_ref[...]
        n = yn * (1.0 + mod2_ref[1:2, :]) + mod2_ref[0:1, :]
        is_ctx = pl.program_id(1) >= lat_tiles

        @pl.when(jnp.logical_not(is_ctx))
        def _():
            u_ref[...] = _bdot(n, w2_ref[...]).astype(u_ref.dtype)

        @pl.when(is_ctx)
        def _():
            lo, hi = ctx_cols
            u_ref[...] = jnp.zeros_like(u_ref)
            u_ref[:, lo:hi] = _bdot(n, w2_ref[:, lo:hi]).astype(u_ref.dtype)


def _out_proj(a, b, w, h, mod, n_lat, final_norm_w=None, next_proj=None):
    bsz, rows, wa = a.shape
    d = w.shape[1]
    tm = ROW_TILE
    lat_tiles = n_lat // tm
    src_specs, srcs = _token_rows(h, tm, lat_tiles)
    kind = lambda bi, i: (bi, (i >= lat_tiles).astype(jnp.int32), 0, 0)
    in_specs = src_specs + [
        pl.BlockSpec((None, tm, wa), lambda bi, i: (bi, i, 0)),
        pl.BlockSpec((None, tm, b.shape[-1]), lambda bi, i: (bi, i, 0)),
        pl.BlockSpec(w.shape, lambda bi, i: (0, 0)),
        pl.BlockSpec((None, None, 3, d), kind),
    ]
    args = srcs + [a, b, w, mod]
    out_specs = pl.BlockSpec((None, tm, d), lambda bi, i: (bi, i, 0))
    out_shape = jax.ShapeDtypeStruct((bsz, rows, d), F32)
    ctx_cols = None
    if final_norm_w is not None:
        tail = "final_norm"
        in_specs.append(pl.BlockSpec((1, d), lambda bi, i: (0, 0)))
        args.append(final_norm_w.reshape(1, d))
    else:
        tail = "next_proj"
        norm_w, mod2, w2, ctx_cols = next_proj
        n2 = w2.shape[1]
        in_specs += [pl.BlockSpec((1, d), lambda bi, i: (0, 0)), pl.BlockSpec((None, None, 3, d), kind),
                     pl.BlockSpec((d, n2), lambda bi, i: (0, 0))]
        args += [norm_w.reshape(1, d), mod2, w2]
        out_specs = [out_specs, pl.BlockSpec((None, tm, n2), lambda bi, i: (bi, i, 0))]
        out_shape = [out_shape, jax.ShapeDtypeStruct((bsz, rows, n2), BF16)]
    return pl.pallas_call(
        functools.partial(_out_proj_kernel, n_src=len(srcs), lat_tiles=lat_tiles, tail=tail, ctx_cols=ctx_cols),
        grid=(bsz, rows // tm),
        in_specs=in_specs,
        out_specs=out_specs,
        out_shape=out_shape,
        compiler_params=_cparams("parallel", "parallel"),
        name="out_proj_" + tail,
    )(*args)


HY_FAST = 64
HY_SUB = 8
HY_CH_TILE = 256
HY_UNROLL = 32


def _hy_dft_tables(l):
    n = 2 * l
    half = l // HY_FAST
    k1 = jnp.arange(half, dtype=jnp.int32)[None, :, None]
    n1 = jnp.arange(half, dtype=jnp.int32)[None, None, :]
    n2 = jnp.arange(HY_FAST, dtype=jnp.int32)[:, None, None]
    ang = (jnp.pi / n) * (((2 * k1 + 1) * (HY_FAST * n1 + n2)) % (2 * n)).astype(F32)
    m1 = jnp.concatenate([jnp.cos(ang), -jnp.sin(ang)], axis=1)
    groups = HY_FAST // HY_SUB
    m1g = m1.reshape(groups, HY_SUB, 2 * half, half)
    jj = np.arange(HY_SUB)[:, None, None]
    row_sel = (np.arange(2 * half * HY_SUB)[None, :, None] == np.arange(2 * half)[None, None, :] * HY_SUB + jj)
    col_sel = (np.arange(half * HY_SUB)[None, None, :] == np.arange(half)[None, :, None] * HY_SUB + jj)
    place = lambda out: jnp.einsum("jRr,gjrn,jnC->" + out, row_sel.astype(np.float32), m1g,
                                   col_sel.astype(np.float32), precision=lax.Precision.HIGHEST).astype(BF16)
    big, big_t = place("gRC"), place("gCR")
    kk = jnp.arange(HY_FAST, dtype=jnp.int32)
    ang2 = (2.0 * jnp.pi / HY_FAST) * ((kk[:, None] * kk[None, :]) % HY_FAST).astype(F32)
    wr, wi = jnp.cos(ang2), -jnp.sin(ang2)
    w2 = jnp.concatenate([jnp.concatenate([wr, -wi], axis=1), jnp.concatenate([wi, wr], axis=1)], axis=0)
    w2_inv = jnp.concatenate([jnp.concatenate([wr, wi], axis=1), jnp.concatenate([-wi, wr], axis=1)], axis=0)
    return big, big_t, w2.astype(BF16), w2_inv.astype(BF16)


def _hy_stage1(src_ref, m1_ref, bs_ref):
    half, _, c = src_ref.shape
    for g in range(HY_FAST // HY_SUB):
        cols = pl.ds(g * HY_SUB, HY_SUB)
        x = src_ref[:, cols, :].reshape(half * HY_SUB, c).astype(BF16)
        out = jnp.dot(m1_ref[g], x, preferred_element_type=F32)
        bs_ref[:, cols, :] = out.reshape(2 * half, HY_SUB, c)


def _hy_stage2(bs_ref, w2_ref, k1):
    half = bs_ref.shape[0] // 2
    return jnp.dot(w2_ref[...], jnp.concatenate([bs_ref[k1], bs_ref[half + k1]], axis=0).astype(BF16),
                   preferred_element_type=F32)


def _hy_filter_kernel(z_ref, w1_ref, b1_ref, f1_ref, w2_ref, b2_ref, f2_ref, w3_ref, dec_ref, o_ref, hid_ref):
    j = pl.program_id(0)

    @pl.when(j == 0)
    def _():
        hid = jnp.sin(f1_ref[...] * (_dot_hi(z_ref[...], w1_ref[...]) + b1_ref[...]))
        hid_ref[...] = jnp.sin(f2_ref[...] * (_dot_hi(hid, w2_ref[...]) + b2_ref[...]))

    h = _dot_hi(hid_ref[...], w3_ref[...]) * dec_ref[...]
    row = lax.broadcasted_iota(jnp.int32, h.shape, 0)
    o_ref[...] = jnp.where(jnp.logical_and(j % 2 == 1, row == 0), 0.0, h).astype(o_ref.dtype)


def _hy_filters(l, w1, b1, f1, w2, b2, f2, w3):
    t = jnp.linspace(0.0, 1.0, l)[:, None]
    bands = jnp.linspace(1e-4, HY_BANDS - 1, HY_BANDS)
    ang = (2.0 * math.pi / l) * jnp.arange(l, dtype=F32)[:, None] * bands[None]
    z = jnp.concatenate([t, jnp.cos(ang), -jnp.sin(ang)], axis=-1)
    emb = z.shape[1]
    emb_pad = 64
    z = jnp.pad(z, ((0, 0), (0, emb_pad - emb)))
    w1p = jnp.pad(w1, ((0, emb_pad - emb), (0, 0)))
    deltas = jnp.abs(jnp.linspace(math.log(HY_TARGET) / HY_SLOW_PCT, math.log(HY_TARGET) / HY_FAST_PCT, W))
    dec = jnp.exp(-t * deltas)
    ffn = w1.shape[1]
    nblk = w3.shape[1] // W
    full = lambda shape: pl.BlockSpec(shape, lambda j: (0,) * len(shape))
    return pl.pallas_call(
        _hy_filter_kernel,
        grid=(nblk,),
        in_specs=[full((l, emb_pad)), full((emb_pad, ffn)), full((1, ffn)), full((1, ffn)), full((ffn, ffn)),
                  full((1, ffn)), full((1, ffn)), pl.BlockSpec((ffn, W), lambda j: (0, j)), full((l, W))],
        out_specs=pl.BlockSpec((l, W), lambda j: (0, j)),
        out_shape=jax.ShapeDtypeStruct((l, nblk * W), F32),
        scratch_shapes=[pltpu.VMEM((l, ffn), F32)],
        compiler_params=_cparams("arbitrary"),
        name="hyena_filters",
    )(z, w1p, b1.reshape(1, -1), f1.reshape(1, -1), w2, b2.reshape(1, -1), f2.reshape(1, -1), w3, dec)


def _hy_spectrum_kernel(hf_ref, hb_ref, m1_ref, w2_ref, o_ref, bs_ref, xf_ref, *, half):
    _hy_stage1(hf_ref, m1_ref, bs_ref)

    def forward(k1, carry):
        xf_ref[k1] = _hy_stage2(bs_ref, w2_ref, k1)
        return carry

    lax.fori_loop(0, half, forward, 0, unroll=min(HY_UNROLL, half))
    _hy_stage1(hb_ref, m1_ref, bs_ref)

    def combine(k1, carry):
        xf = xf_ref[k1]
        xb = _hy_stage2(bs_ref, w2_ref, k1)
        o_ref[k1] = jnp.concatenate([xf[:HY_FAST] + xb[:HY_FAST], xf[HY_FAST:] - xb[HY_FAST:]],
                                    axis=0).astype(o_ref.dtype)
        return carry

    lax.fori_loop(0, half, combine, 0, unroll=min(HY_UNROLL, half))


def _hy_spectra(filt, m1, w2):
    l = filt.shape[0]
    half = l // HY_FAST
    ct = HY_CH_TILE
    nct = W // ct
    orders = filt.shape[1] // (2 * W)
    filt = filt.reshape(half, HY_FAST, filt.shape[1])
    return pl.pallas_call(
        functools.partial(_hy_spectrum_kernel, half=half),
        grid=(orders, nct),
        in_specs=[pl.BlockSpec((half, HY_FAST, ct), lambda o, c: (0, 0, 2 * o * nct + c)),
                  pl.BlockSpec((half, HY_FAST, ct), lambda o, c: (0, 0, (2 * o + 1) * nct + c)),
                  pl.BlockSpec(m1.shape, lambda o, c: (0, 0, 0)), pl.BlockSpec(w2.shape, lambda o, c: (0, 0))],
        out_specs=pl.BlockSpec((None, half, 2 * HY_FAST, ct), lambda o, c: (o, 0, 0, c)),
        out_shape=jax.ShapeDtypeStruct((orders, half, 2 * HY_FAST, W), BF16),
        scratch_shapes=[pltpu.VMEM((2 * half, HY_FAST, ct), F32), pltpu.VMEM((half, 2 * HY_FAST, ct), F32)],
        compiler_params=_cparams("parallel", "parallel"),
        name="hyena_spectra",
    )(filt, filt, m1, w2)


def _hyena_kernel(v_ref, x1_ref, x2_ref, g_ref, cw_ref, cb_ref, bias_ref, m1_ref, m1t_ref, w2_ref, w2i_ref, gs_ref,
                  o_ref, z_ref, bs_ref, cs_ref, acc_ref, *, l):
    o = pl.program_id(2)
    rc = ROW_TILE
    half = l // HY_FAST
    ct = z_ref.shape[-1]

    def conv_rows(ref, sec, c0):
        x, prev, nxt = _with_neighbours(ref, slice(None), c0, rc, l, l)
        w = cw_ref[sec]
        return prev * w[0:1, :] + x * w[1:2, :] + nxt * w[2:3, :] + cb_ref[sec]

    slabs = rc // HY_FAST

    @pl.when(o == 0)
    def _():
        def body(i, carry):
            c0 = pl.multiple_of(i * rc, rc)
            z_ref[pl.ds(i * slabs, slabs)] = conv_rows(v_ref, 0, c0).reshape(slabs, HY_FAST, ct)
            return carry

        lax.fori_loop(0, l // rc, body, 0)

    _hy_stage1(z_ref, m1_ref, bs_ref)

    def per_k1(k1, carry):
        x = _hy_stage2(bs_ref, w2_ref, k1)
        xr, xi = x[:HY_FAST], x[HY_FAST:]
        g = gs_ref[k1].astype(F32)
        gr, gi = g[:HY_FAST], g[HY_FAST:]
        y = jnp.concatenate([xr * gr - xi * gi, xr * gi + xi * gr], axis=0).astype(BF16)
        cs_ref[k1] = jnp.dot(w2i_ref[...], y, preferred_element_type=F32)
        return carry

    lax.fori_loop(0, half, per_k1, 0, unroll=min(HY_UNROLL, half))

    for g in range(HY_FAST // HY_SUB):
        c_re = cs_ref[:, pl.ds(g * HY_SUB, HY_SUB), :].reshape(half * HY_SUB, ct)
        c_im = cs_ref[:, pl.ds(HY_FAST + g * HY_SUB, HY_SUB), :].reshape(half * HY_SUB, ct)
        out = jnp.dot(m1t_ref[g], jnp.concatenate([c_re, c_im], axis=0).astype(BF16), preferred_element_type=F32)
        acc_ref[:, pl.ds(g * HY_SUB, HY_SUB), :] = out.reshape(half, HY_SUB, ct)

    def finish(i, carry):
        c0 = pl.multiple_of(i * rc, rc)
        rows = pl.ds(c0, rc)
        tile = pl.ds(i * slabs, slabs)
        conv = (acc_ref[tile] * (1.0 / l) + z_ref[tile] * bias_ref[o]).reshape(rc, ct)

        @pl.when(o == 0)
        def _():
            z_ref[tile] = (conv_rows(x1_ref, 1, c0) * conv).reshape(slabs, HY_FAST, ct)

        @pl.when(o == 1)
        def _():
            zn = conv_rows(x2_ref, 2, c0) * conv
            o_ref[rows, :] = (zn * _silu(g_ref[rows, :].astype(F32))).astype(o_ref.dtype)

        return carry

    lax.fori_loop(0, l // rc, finish, 0)


def _hyena(u, l, conv_w, conv_b, hy_bias, tables, spectra):
    b = u.shape[0]
    ct = HY_CH_TILE
    nct = W // ct
    half = l // HY_FAST
    sec = lambda s: pl.BlockSpec((None, l, ct), lambda bi, c, o, s=s: (bi, 0, s * nct + c))
    cw = jnp.transpose(conv_w.reshape(3, 3, W), (1, 0, 2))
    whole = lambda a: pl.BlockSpec(a.shape, lambda bi, c, o: (0,) * a.ndim)
    return pl.pallas_call(
        functools.partial(_hyena_kernel, l=l),
        grid=(b, nct, 2),
        in_specs=[
            sec(0), sec(1), sec(2), sec(3),
            pl.BlockSpec((3, 3, ct), lambda bi, c, o: (0, 0, c)),
            pl.BlockSpec((3, 1, ct), lambda bi, c, o: (0, 0, c)),
            pl.BlockSpec((2, 1, ct), lambda bi, c, o: (0, 0, c)),
            whole(tables[0]), whole(tables[1]), whole(tables[2]), whole(tables[3]),
            pl.BlockSpec((None, half, 2 * HY_FAST, ct), lambda bi, c, o: (o, 0, 0, c)),
        ],
        out_specs=pl.BlockSpec((None, l, ct), lambda bi, c, o: (bi, 0, c)),
        out_shape=jax.ShapeDtypeStruct((b, l, W), BF16),
        scratch_shapes=[pltpu.VMEM((half, HY_FAST, ct), F32), pltpu.VMEM((2 * half, HY_FAST, ct), F32),
                        pltpu.VMEM((half, 2 * HY_FAST, ct), F32), pltpu.VMEM((half, HY_FAST, ct), F32)],
        compiler_params=_cparams("parallel", "parallel", "arbitrary"),
        name="hyena",
    )(u, u, u, u, cw, conv_b.reshape(3, 1, W), hy_bias.reshape(2, 1, W), *tables, spectra)


NA_ROWS_PER_STEP = 32
NEG_BIG = -1e30


def _na_bias_tables(rpb, rows):
    heads = rpb.shape[0]
    n_dr, n_dc = 2 * NA_WIN_R - 1, 2 * NA_WIN_C - 1
    cq = np.arange(GRID_W)[:, None]
    kc = np.arange(GRID_W)[None, :]
    c0 = np.clip(cq - NA_WIN_C // 2, 0, GRID_W - NA_WIN_C)
    col_ok = (kc >= c0) & (kc < c0 + NA_WIN_C)
    col_sel = (np.arange(n_dc)[:, None, None] == (kc - cq + NA_WIN_C - 1)[None]).astype(np.float32)
    tiles = jnp.einsum("phrj,jcx->phrcx", rpb.astype(F32).reshape(heads // 2, 2, n_dr, n_dc), col_sel,
                       precision=lax.Precision.HIGHEST)
    tiles = jnp.where(col_ok, tiles, NEG_BIG)
    tiles = jnp.pad(tiles, ((0, 0), (0, 0), (0, 2), (0, 0), (0, 0)), constant_values=NEG_BIG)
    return jnp.concatenate([tiles[:, :, :n_dr + 1], tiles[:, :, 1:]], axis=-1)


def _na_kernel(q_ref, k_ref, v_ref, z_ref, bias_ref, o_ref, *, l):
    rows = l // GRID_W
    kn = NA_WIN_R * GRID_W
    n_ctx = k_ref.shape[0] - l
    lane_lo = _lane_ids((1, LANES)) < HD
    k_ctx = k_ref[pl.ds(l, n_ctx), :]
    v_ctx = v_ref[pl.ds(l, n_ctx), :]
    per_step = min(NA_ROWS_PER_STEP, rows)

    def body(it, carry):
        each = lambda fn: [fn(j, it * per_step + j) for j in range(per_step)]
        r0 = each(lambda j, rq: jnp.clip(rq - NA_WIN_R // 2, 0, rows - NA_WIN_R))
        qrows = each(lambda j, rq: pl.ds(pl.multiple_of(rq * GRID_W, GRID_W), GRID_W))
        krows = each(lambda j, rq: pl.ds(pl.multiple_of(r0[j] * GRID_W, GRID_W), kn))

        def stacked_q(j, rq):
            q = q_ref[qrows[j], :].astype(F32) * HD ** -0.5
            return jnp.concatenate([jnp.where(lane_lo, q, 0.0), jnp.where(lane_lo, 0.0, q)], axis=0).astype(BF16)

        def window_bias(j, rq):
            base = r0[j] - rq + NA_WIN_R - 1
            return jnp.concatenate(
                [jnp.concatenate([bias_ref[hh, base + w] for w in range(0, NA_WIN_R, 2)], axis=1) for hh in range(2)],
                axis=0)

        q2 = each(stacked_q)
        s_win = each(lambda j, rq: _bdot_nt(q2[j], k_ref[krows[j], :]) + window_bias(j, rq))
        s_ctx = each(lambda j, rq: _bdot_nt(q2[j], k_ctx))
        m = each(lambda j, rq: jnp.maximum(jnp.max(s_win[j], axis=-1, keepdims=True),
                                           jnp.max(s_ctx[j], axis=-1, keepdims=True)))
        p_win = each(lambda j, rq: jnp.exp(s_win[j] - m[j]))
        p_ctx = each(lambda j, rq: jnp.exp(s_ctx[j] - m[j]))
        den = each(lambda j, rq: jnp.sum(p_win[j], axis=-1, keepdims=True) + jnp.sum(p_ctx[j], axis=-1, keepdims=True))
        out = each(lambda j, rq: (_bdot(p_win[j], v_ref[krows[j], :]) + _bdot(p_ctx[j], v_ctx)) / den[j])
        for j in range(per_step):
            gate = _silu(z_ref[qrows[j], :].astype(F32))
            o_ref[qrows[j], :] = (jnp.where(lane_lo, out[j][:GRID_W], out[j][GRID_W:]) * gate).astype(o_ref.dtype)
        return carry

    lax.fori_loop(0, rows // per_step, body, 0)


def _natten(u, l, col_q, col_k, col_v, col_z, bias):
    b, t, _ = u.shape
    lat = lambda col: pl.BlockSpec((None, l, LANES), lambda bi, p, col=col: (bi, 0, col // LANES + p))
    full = lambda col: pl.BlockSpec((None, t, LANES), lambda bi, p, col=col: (bi, 0, col // LANES + p))
    return pl.pallas_call(
        functools.partial(_na_kernel, l=l),
        grid=(b, PAIRS),
        in_specs=[lat(col_q), full(col_k), full(col_v), lat(col_z),
                  pl.BlockSpec((None,) + bias.shape[1:], lambda bi, p: (p, 0, 0, 0, 0))],
        out_specs=pl.BlockSpec((None, l, LANES), lambda bi, p: (bi, 0, p)),
        out_shape=jax.ShapeDtypeStruct((b, l, W), BF16),
        compiler_params=_cparams("parallel", "parallel"),
        name="natten",
    )(u, u, u, u, bias)


def kernel(x, c, ctx, c_ctx, ada_w, ada_b, norm_w, final_norm_w, even_w_in, even_w_out, ret_decay, rw_mu, rw_w0, rw_w2, rw_a0, rw_a2, rw_kk, rw_ka, rw_rk, rw_ln_w, rw_ln_b, odd_w_in, odd_w_out, hy_conv_w, hy_conv_b, hy_w1, hy_b1, hy_f1, hy_w2, hy_b2, hy_f2, hy_w3, hy_bias, na_rpb):
    b, l, d = x.shape
    n_ctx = ctx.shape[1]
    t = l + n_ctx
    depth = ada_w.shape[0]
    assert depth == 2 and l % ROW_TILE == 0 and n_ctx % ROW_TILE == 0 and (l // GRID_W) >= NA_WIN_R
    assert (l // GRID_W) % min(NA_ROWS_PER_STEP, l // GRID_W) == 0

    cond_rows = -(-(b + 1) // 8) * 8
    cond = jnp.concatenate([c, c_ctx[None, :], jnp.zeros((cond_rows - b - 1, d), F32)], axis=0)
    mod_all = _ada_mod(cond, ada_w, ada_b)
    mods = [jnp.stack([mod_all[i, :b].reshape(b, 3, d),
                       jnp.broadcast_to(mod_all[i, b].reshape(1, 3, d), (b, 3, d))], axis=1) for i in range(depth)]

    h = (x, ctx)

    col_rz, col_wz, col_shift = 3 * W, 4 * W, 5 * W
    u0 = _norm_proj(h, norm_w[0], mods[0], even_w_in[0].astype(BF16), l, t)
    cos_t, sin_t = _rope_tables(l, t)
    ret_o = _retention(u0, cos_t, sin_t, ret_decay[0], l, 0, W, 2 * W, col_rz)
    rw_o = _rwkv(u0, l, col_shift, col_wz, rw_mu[0], rw_w0[0], rw_w2[0], rw_a0[0], rw_a2[0], rw_kk[0], rw_ka[0],
                 rw_rk[0], rw_ln_w[0], rw_ln_b[0])
    h, u1 = _out_proj(ret_o, rw_o, even_w_out[0].astype(BF16), h, mods[0], l,
                      next_proj=(norm_w[1], mods[1], odd_w_in[0].astype(BF16), (5 * W, 7 * W)))
    tables = _hy_dft_tables(l)
    filt = _hy_filters(l, hy_w1[0], hy_b1[0], hy_f1[0], hy_w2[0], hy_b2[0], hy_f2[0], hy_w3[0])
    spectra = _hy_spectra(filt, tables[0], tables[2])
    hy_o = _hyena(u1, l, hy_conv_w[0], hy_conv_b[0], hy_bias[0], tables, spectra)
    na_o = _natten(u1, l, 4 * W, 5 * W, 6 * W, 7 * W, _na_bias_tables(na_rpb[0], l // GRID_W))
    return _out_proj(hy_o, na_o, odd_w_out[0].astype(BF16), h, mods[1], l, final_norm_w)
```

```python
import functools
import math

import jax
import jax.numpy as jnp
import numpy as np
from jax import lax
from jax.experimental import pallas as pl
from jax.experimental.pallas import tpu as pltpu

F32 = jnp.float32
BF16 = jnp.bfloat16

GRID_W = 64
NORM_EPS = 1e-6
ROPE_BASE = 10000.0
HEADS = 8
HD = 64
W = HEADS * HD
LORA = 64
RWKV_GN_EPS = 64e-5
HY_BANDS = 16
HY_TARGET = 1e-2
HY_FAST_PCT = 0.3
HY_SLOW_PCT = 1.5
NA_WIN_R = 8
NA_WIN_C = 16
LANES = 128
PAIRS = W // LANES
RET_CHUNK = 128
RET_STEPS = 2
RW_CHUNK = 64
RW_SUB = 32
ROW_TILE = 256
HALO = 16
VMEM_LIMIT = 56 * 1024 * 1024


def _cparams(*sem):
    return pltpu.CompilerParams(dimension_semantics=sem, vmem_limit_bytes=VMEM_LIMIT)


def _bdot(a, b):
    return jnp.dot(a.astype(BF16), b.astype(BF16), preferred_element_type=F32)


def _bdot_nt(a, b):
    return lax.dot_general(a.astype(BF16), b.astype(BF16), (((1,), (1,)), ((), ())), preferred_element_type=F32)


def _bdot_tn(a, b):
    return lax.dot_general(a.astype(BF16), b.astype(BF16), (((0,), (0,)), ((), ())), preferred_element_type=F32)


def _dot_hi(a, b):
    ah = a.astype(BF16)
    al = (a - ah.astype(F32)).astype(BF16)
    bh = b.astype(BF16)
    bl = (b - bh.astype(F32)).astype(BF16)
    d = lambda p, q: jnp.dot(p, q, preferred_element_type=F32)
    return d(ah, bh) + d(ah, bl) + d(al, bh)


def _silu(x):
    return x * (1.0 / (1.0 + jnp.exp(-x)))


def _sigmoid(x):
    return 1.0 / (1.0 + jnp.exp(-x))


def _lane_ids(shape):
    return lax.broadcasted_iota(jnp.int32, shape, len(shape) - 1)


def _row_ids(shape):
    return lax.broadcasted_iota(jnp.int32, shape, len(shape) - 2)


def _head_sum_mxu(x, ones_bd):
    hi = x.astype(BF16)
    lo = (x - hi.astype(F32)).astype(BF16)
    tiles = []
    for p in range(x.shape[-1] // LANES):
        sl = slice(p * LANES, (p + 1) * LANES)
        tiles.append(jnp.dot(hi[:, sl], ones_bd, preferred_element_type=F32)
                     + jnp.dot(lo[:, sl], ones_bd, preferred_element_type=F32))
    return tiles[0] if len(tiles) == 1 else jnp.concatenate(tiles, axis=-1)


def _with_neighbours(ref, cols, c0, c, n_rows, lat_rows):
    row = lax.broadcasted_iota(jnp.int32, (c, 1), 0)
    x = ref[pl.ds(c0, c), cols].astype(F32)
    p0 = pl.multiple_of(jnp.maximum(c0 - HALO, 0), HALO)
    n0 = pl.multiple_of(jnp.minimum(c0 + c, n_rows - HALO), HALO)
    prev_row = ref[pl.ds(p0, HALO), cols].astype(F32)[HALO - 1:HALO, :]
    next_row = ref[pl.ds(n0, HALO), cols].astype(F32)[0:1, :]
    has_prev = jnp.logical_and(c0 != 0, c0 != lat_rows)
    has_next = jnp.logical_and(c0 + c != lat_rows, c0 + c != n_rows)
    prev_row = jnp.where(has_prev, prev_row, 0.0)
    next_row = jnp.where(has_next, next_row, 0.0)
    prev = jnp.where(row == 0, prev_row, pltpu.roll(x, 1, 0))
    nxt = jnp.where(row == c - 1, next_row, pltpu.roll(x, c - 1, 0))
    return x, prev, nxt


def _scan_chunk_id(i, n_lat, n_ctx, reverse):
    if reverse:
        return jnp.where(i < n_ctx, n_lat + n_ctx - 1 - i, n_lat - 1 - (i - n_ctx))
    return jnp.where(i < n_ctx, n_lat + i, i - n_ctx)


def _ada_kernel(c_ref, w_ref, b_ref, o_ref):
    o_ref[...] = _dot_hi(_silu(c_ref[...]), w_ref[...]) + b_ref[...]


def _ada_mod(cond, ada_w, ada_b):
    depth, d, d3 = ada_w.shape
    rows = cond.shape[0]
    return pl.pallas_call(
        _ada_kernel,
        grid=(depth, d3 // d),
        in_specs=[
            pl.BlockSpec((rows, d), lambda i, j: (0, 0)),
            pl.BlockSpec((None, d, d), lambda i, j: (i, 0, j)),
            pl.BlockSpec((None, 1, d), lambda i, j: (i, 0, j)),
        ],
        out_specs=pl.BlockSpec((None, rows, d), lambda i, j: (i, 0, j)),
        out_shape=jax.ShapeDtypeStruct((depth, rows, d3), F32),
        compiler_params=_cparams("parallel", "parallel"),
        name="ada_mod",
    )(cond, ada_w, ada_b.reshape(depth, 1, d3))


def _token_rows(h, tm, lat_tiles):
    if isinstance(h, tuple):
        d = h[0].shape[-1]
        return [pl.BlockSpec((None, tm, d), lambda bi, i: (bi, jnp.minimum(i, lat_tiles - 1), 0)),
                pl.BlockSpec((None, tm, d), lambda bi, i: (bi, jnp.maximum(i - lat_tiles, 0), 0))], list(h)
    return [pl.BlockSpec((None, tm, h.shape[-1]), lambda bi, i: (bi, i, 0))], [h]


def _read_token_rows(refs, lat_tiles):
    if len(refs) == 2:
        return jnp.where(pl.program_id(1) >= lat_tiles, refs[1][...], refs[0][...])
    return refs[0][...]


def _norm_proj_kernel(*refs, n_src, lat_tiles):
    nw_ref, mod_ref, w_ref, o_ref = refs[n_src:]
    x = _read_token_rows(refs[:n_src], lat_tiles)
    y = x * lax.rsqrt(jnp.mean(x * x, axis=-1, keepdims=True) + NORM_EPS) * nw_ref[...]
    n = y * (1.0 + mod_ref[1:2, :]) + mod_ref[0:1, :]
    o_ref[...] = _bdot(n, w_ref[...]).astype(o_ref.dtype)


def _norm_proj(h, norm_w, mod, w, n_lat, t):
    d, n = w.shape
    b = mod.shape[0]
    tm = ROW_TILE
    lat_tiles = n_lat // tm
    src_specs, srcs = _token_rows(h, tm, lat_tiles)
    return pl.pallas_call(
        functools.partial(_norm_proj_kernel, n_src=len(srcs), lat_tiles=lat_tiles),
        grid=(b, t // tm),
        in_specs=src_specs + [
            pl.BlockSpec((1, d), lambda bi, i: (0, 0)),
            pl.BlockSpec((None, None, 3, d), lambda bi, i: (bi, (i >= lat_tiles).astype(jnp.int32), 0, 0)),
            pl.BlockSpec((d, n), lambda bi, i: (0, 0)),
        ],
        out_specs=pl.BlockSpec((None, tm, n), lambda bi, i: (bi, i, 0)),
        out_shape=jax.ShapeDtypeStruct((b, t, n), BF16),
        compiler_params=_cparams("parallel", "parallel"),
        name="norm_proj",
    )(*srcs, norm_w.reshape(1, d), mod, w)


def _ret_kernel(q_ref, k_ref, v_ref, z_ref, cos_ref, sin_ref, rd_ref, o_ref, y_acc, st_ref, dm_ref, qd_ref, kd_ref,
                *, n_lat, n_ctx):
    c = RET_CHUNK
    lane = _lane_ids((1, LANES))
    lane_lo = lane < HD
    rope_lo = (lane % HD) < (HD // 2)
    pos = lax.broadcasted_iota(jnp.int32, (c, 1), 0).astype(F32)
    ti = lax.broadcasted_iota(jnp.int32, (c, c), 0)
    si = lax.broadcasted_iota(jnp.int32, (c, c), 1)
    same_head = (_row_ids((LANES, LANES)) < HD) == (_lane_ids((LANES, LANES)) < HD)
    ones_bd = same_head.astype(BF16)
    inst = [(d, pr) for d in range(2) for pr in range(PAIRS)]
    sl = lambda pr: slice(pr * LANES, (pr + 1) * LANES)

    g_chunk = []
    for d, pr in inst:
        lg_a = -jnp.exp(rd_ref[d, 2 * pr])
        lg_b = -jnp.exp(rd_ref[d, 2 * pr + 1])
        lgv = jnp.where(lane_lo, lg_a, lg_b)
        diff = ((ti - si) if d == 0 else (si - ti)).astype(F32)
        keep = diff >= 0
        dmat = lambda lg: jnp.where(keep, jnp.exp(jnp.where(keep, diff, 0.0) * lg), 0.0)
        dm_ref[d, pr] = jnp.concatenate([dmat(lg_a), dmat(lg_b)], axis=0)
        qd_ref[d, :, sl(pr)] = jnp.exp(((pos + 1.0) if d == 0 else (c - pos)) * lgv)
        kd_ref[d, :, sl(pr)] = jnp.exp(((c - 1.0 - pos) if d == 0 else pos) * lgv)
        g_chunk.append(jnp.exp(c * lgv))
    y_acc[...] = jnp.zeros_like(y_acc)
    st_ref[...] = jnp.zeros_like(st_ref)

    def rope(x, rows):
        partner = jnp.where(rope_lo, pltpu.roll(x, LANES - HD // 2, 1), pltpu.roll(x, HD // 2, 1))
        return x * cos_ref[rows, :] + partner * sin_ref[rows, :]

    def stack2(x):
        return jnp.concatenate([jnp.where(lane_lo, x, 0.0), jnp.where(lane_lo, 0.0, x)], axis=0)

    steps = max(s for s in range(1, RET_STEPS + 1) if (n_lat + n_ctx) % s == 0)

    def step(i, carry):
        work = [(sub,) + dp for sub in range(steps) for dp in inst]
        rows = [[pl.ds(pl.multiple_of(_scan_chunk_id(steps * i + sub, n_lat, n_ctx, d == 1) * c, c), c)
                 for d in range(2)] for sub in range(steps)]
        each = lambda fn: [fn(j, *work[j]) for j in range(len(work))]
        q = each(lambda j, sub, d, pr: rope(q_ref[rows[sub][d], sl(pr)].astype(F32), rows[sub][d]))
        k = each(lambda j, sub, d, pr: rope(k_ref[rows[sub][d], sl(pr)].astype(F32) * HD ** -0.5, rows[sub][d]))
        v = each(lambda j, sub, d, pr: v_ref[rows[sub][d], sl(pr)])
        scores = each(lambda j, sub, d, pr: _bdot_nt(stack2(q[j]), k[j]) * dm_ref[d, pr])
        upd = each(lambda j, sub, d, pr: _bdot_tn(k[j] * kd_ref[d, :, sl(pr)], v[j]))
        intra = each(lambda j, sub, d, pr: _bdot(scores[j], v[j]))
        state = [st_ref[d, pr] for d, pr in inst]
        for sub in range(steps):
            for n, (d, pr) in enumerate(inst):
                j = sub * len(inst) + n
                cross = _bdot(q[j] * qd_ref[d, :, sl(pr)], state[n])
                state[n] = state[n] * g_chunk[n] + jnp.where(same_head, upd[j], 0.0)
                y_acc[rows[sub][d], sl(pr)] += jnp.where(lane_lo, intra[j][:c], intra[j][c:]) + cross
        for n, (d, pr) in enumerate(inst):
            st_ref[d, pr] = state[n]
        return carry

    lax.fori_loop(0, (n_lat + n_ctx) // steps, step, 0)

    def finish(i, carry):
        rows = pl.ds(pl.multiple_of(i * ROW_TILE, ROW_TILE), ROW_TILE)
        y = y_acc[rows, :]
        ms = _head_sum_mxu(y * y, ones_bd) * (1.0 / HD)
        o_ref[rows, :] = (y * lax.rsqrt(ms + NORM_EPS) * _silu(z_ref[rows, :].astype(F32))).astype(o_ref.dtype)
        return carry

    lax.fori_loop(0, (n_lat + n_ctx) * c // ROW_TILE, finish, 0)


def _retention(u, cos_t, sin_t, ret_decay, n_lat_rows, col_q, col_k, col_v, col_z):
    b, t, _ = u.shape
    c = RET_CHUNK
    rd = jnp.broadcast_to(ret_decay.astype(F32)[:, :, None, None], (2, HEADS, 1, LANES))
    blk = lambda col: pl.BlockSpec((None, t, W), lambda bi, col=col: (bi, 0, col // W))
    return pl.pallas_call(
        functools.partial(_ret_kernel, n_lat=n_lat_rows // c, n_ctx=(t - n_lat_rows) // c),
        grid=(b,),
        in_specs=[
            blk(col_q), blk(col_k), blk(col_v), blk(col_z),
            pl.BlockSpec((t, LANES), lambda bi: (0, 0)),
            pl.BlockSpec((t, LANES), lambda bi: (0, 0)),
            pl.BlockSpec((2, HEADS, 1, LANES), lambda bi: (0, 0, 0, 0)),
        ],
        out_specs=pl.BlockSpec((None, t, W), lambda bi: (bi, 0, 0)),
        out_shape=jax.ShapeDtypeStruct((b, t, W), BF16),
        scratch_shapes=[pltpu.VMEM((t, W), F32), pltpu.VMEM((2, PAIRS, LANES, LANES), F32),
                        pltpu.VMEM((2, PAIRS, 2 * c, c), F32), pltpu.VMEM((2, c, W), F32),
                        pltpu.VMEM((2, c, W), F32)],
        compiler_params=_cparams("parallel"),
        name="retention",
    )(u, u, u, u, cos_t, sin_t, rd)


def _rope_tables(n_lat_rows, t):
    pos = jnp.arange(n_lat_rows)
    row = (pos // GRID_W).astype(F32)
    col = (pos % GRID_W).astype(F32)
    nf = HD // 4
    inv = ROPE_BASE ** (-jnp.arange(nf, dtype=F32) / nf)
    ang = jnp.concatenate([row[:, None] * inv, col[:, None] * inv], axis=-1)
    cos, sin = jnp.cos(ang), jnp.sin(ang)
    cos_h = jnp.concatenate([cos, cos], axis=-1)
    sin_h = jnp.concatenate([-sin, sin], axis=-1)
    reps = LANES // HD
    cos_t = jnp.concatenate([jnp.tile(cos_h, (1, reps)), jnp.ones((t - n_lat_rows, LANES), F32)], axis=0)
    sin_t = jnp.concatenate([jnp.tile(sin_h, (1, reps)), jnp.zeros((t - n_lat_rows, LANES), F32)], axis=0)
    return cos_t, sin_t


def _rwkv_kernel(r_ref, k_ref, v_ref, lo_ref, z_ref, mu_ref, mulo_ref, kkw_ref, ka_ref, w0_ref, a0_ref, w2_ref,
                 a2_ref, rk_ref, lnw_ref, lnb_ref, o_ref, y_acc, b_acc, st_ref, prep_ref, bonus_ref, plast_ref, tinv_ref,
                 aakv_ref, arbk_ref, *, n_lat, n_ctx):
    c = RW_CHUNK
    c2 = 2 * c
    t_rows = (n_lat + n_ctx) * c
    lat_rows = n_lat * c
    lane_lo = _lane_ids((1, LANES)) < HD
    ri = _row_ids((c2, c2))
    ci = _lane_ids((c2, c2))
    same_head = (ri < c) == (ci < c)
    rt_, ct_ = ri % c, ci % c
    same_sub = (rt_ // RW_SUB) == (ct_ // RW_SUB)
    eye = (ri == ci).astype(F32)
    ones_bd = same_head.astype(BF16)
    t64 = lax.broadcasted_iota(jnp.int32, (c, c), 0)
    s64 = lax.broadcasted_iota(jnp.int32, (c, c), 1)
    masks = []
    for d in range(2):
        before = (ct_ < rt_) if d == 0 else (ct_ > rt_)
        strict = jnp.logical_and(same_head, before)
        incl = jnp.logical_and(same_head, jnp.logical_or(before, ct_ == rt_))
        tri = ((s64 <= t64) if d == 0 else (s64 >= t64)).astype(BF16)
        masks.append((strict, incl, jnp.logical_and(strict, same_sub), tri))

    y_acc[...] = jnp.zeros_like(y_acc)
    b_acc[...] = jnp.zeros_like(b_acc)
    st_ref[...] = jnp.zeros_like(st_ref)

    def stack2(x):
        return jnp.concatenate([jnp.where(lane_lo, x, 0.0), jnp.where(lane_lo, 0.0, x)], axis=0)

    def dup(x):
        return jnp.concatenate([x, x], axis=0)

    def unstack(x):
        return jnp.where(lane_lo, x[:c], x[c:])

    sh_t = lax.broadcasted_iota(jnp.int32, (c, c + 2 * HALO), 0) + HALO
    sh_s = lax.broadcasted_iota(jnp.int32, (c, c + 2 * HALO), 1)
    shift_base = jnp.where(jnp.abs(sh_s - sh_t) == 1, 0.5, 0.0)

    n_steps = n_lat + n_ctx
    fields = ("at", "bt", "kt", "rt", "v")

    def step_rows(i, d):
        return pl.ds(pl.multiple_of(_scan_chunk_id(i, n_lat, n_ctx, d == 1) * c, c), c)

    def prepare(i, d, slot):
        tri = masks[d][3]
        c0 = pl.multiple_of(_scan_chunk_id(i, n_lat, n_ctx, d == 1) * c, c)
        p0 = pl.multiple_of(jnp.maximum(c0 - HALO, 0), HALO)
        n0 = pl.multiple_of(jnp.minimum(c0 + c, t_rows - HALO), HALO)
        no_prev = jnp.logical_or(c0 == 0, c0 == lat_rows)
        no_next = jnp.logical_or(c0 + c == lat_rows, c0 + c == t_rows)
        dead_prev = jnp.where(no_prev, HALO - 1, -1)
        dead_next = jnp.where(no_next, HALO + c, -1)
        dead = jnp.logical_or(sh_s == dead_prev, sh_s == dead_next)
        shift = jnp.where(dead, 0.0, shift_base).astype(BF16)

        def mixed(ref, mu):
            x = ref[pl.ds(c0, c), :]
            ext = jnp.concatenate([ref[pl.ds(p0, HALO), :], x, ref[pl.ds(n0, HALO), :]], axis=0)
            x = x.astype(F32)
            return x + mu * (jnp.dot(shift, ext, preferred_element_type=F32) - x)

        r = mixed(r_ref, mu_ref[0])
        k = mixed(k_ref, mu_ref[1])
        v = mixed(v_ref, mu_ref[2])
        lo = mixed(lo_ref, mulo_ref[...])
        kk = k * kkw_ref[...]
        yield
        kk_norm = jnp.sqrt(_head_sum_mxu(kk * kk, ones_bd))
        w_pre = _bdot(jnp.tanh(lo[:, :LANES]), w2_ref[d])
        a_pre = _bdot(lo[:, LANES:], a2_ref[d])
        yield
        kk = kk / jnp.maximum(kk_norm, 1e-12)
        logw = -math.exp(-0.5) * _sigmoid(w0_ref[d] + w_pre)
        a = _sigmoid(a0_ref[d] + a_pre)
        k_d = k * (1.0 + (a - 1.0) * ka_ref[...])
        lw_hi = logw.astype(BF16)
        lw_lo = (logw - lw_hi.astype(F32)).astype(BF16)
        cum = jnp.dot(tri, lw_hi, preferred_element_type=F32) + jnp.dot(tri, lw_lo, preferred_element_type=F32)
        bonus_sum = jnp.concatenate(
            [_bdot((r * k_d * rk_ref[...])[:, pr * LANES:(pr + 1) * LANES], ones_bd) for pr in range(PAIRS)], axis=1)
        yield
        p = jnp.exp(cum)
        inv_p = jnp.exp(-cum)
        staged = dict(at=-kk * jnp.exp(cum - logw), bt=kk * a * inv_p, kt=k_d * inv_p, rt=r * p, v=v,
                      bonus=bonus_sum * v)
        yield "stores next"
        for f, name in enumerate(fields):
            prep_ref[slot, d, f] = staged[name].astype(BF16)
        bonus_ref[slot, d] = staged["bonus"]
        plast_ref[slot, d] = p[c - 1:c, :] if d == 0 else p[0:1, :]

    everyone = [(d, pr) for d in range(2) for pr in range(PAIRS)]
    each = lambda fn: [fn(j, *everyone[j]) for j in range(len(everyone))]
    lanes = lambda pr: slice(pr * LANES, (pr + 1) * LANES)

    def independent_half(slot):
        op = lambda name, d, pr: prep_ref[slot, d, fields.index(name), :, lanes(pr)]
        rhs = each(lambda j, d, pr: jnp.concatenate([dup(op("bt", d, pr)), dup(op("kt", d, pr))], axis=0).astype(BF16))
        top = each(lambda j, d, pr: _bdot_nt(stack2(op("at", d, pr)), rhs[j]))
        yield
        n_all = each(lambda j, d, pr: jnp.where(masks[d][0], top[j][:, :c2], 0.0))
        aak_v = each(lambda j, d, pr: _bdot(jnp.where(masks[d][0], top[j][:, c2:], 0.0),
                                            dup(op("v", d, pr)).astype(BF16)))
        n_tr = each(lambda j, d, pr: n_all[j].T)
        nsub_tr = each(lambda j, d, pr: jnp.where(masks[1 - d][2], n_tr[j], 0.0))
        t_tr = each(lambda j, d, pr: (eye + nsub_tr[j]).astype(BF16))
        p_tr = each(lambda j, d, pr: _bdot(nsub_tr[j], nsub_tr[j]).astype(BF16))
        yield
        span = 2
        while 2 * span < RW_SUB:
            both = each(lambda j, d, pr: _bdot(p_tr[j], jnp.concatenate([t_tr[j], p_tr[j]], axis=1)).astype(BF16))
            t_tr = each(lambda j, d, pr: t_tr[j] + both[j][:, :c2])
            p_tr = each(lambda j, d, pr: both[j][:, c2:])
            span *= 2
            yield
        t_tr = each(lambda j, d, pr: t_tr[j] + _bdot(p_tr[j], t_tr[j]).astype(BF16))
        yield
        corr = each(lambda j, d, pr: _bdot(n_tr[j] - nsub_tr[j], t_tr[j]))
        bot = each(lambda j, d, pr: _bdot_nt(stack2(op("rt", d, pr)), rhs[j]))
        yield
        tinv_tr = each(lambda j, d, pr: t_tr[j] + _bdot(t_tr[j], corr[j]))
        for j, (d, pr) in enumerate(everyone):
            tinv_ref[slot, j] = tinv_tr[j].astype(BF16)
            aakv_ref[slot, j] = aak_v[j]
            arbk_ref[slot, j] = jnp.where(jnp.concatenate([masks[d][1], masks[d][1]], axis=1), bot[j], 0.0).astype(BF16)

    def dependent_half(slot, i):
        op = lambda name, d, pr: prep_ref[slot, d, fields.index(name), :, lanes(pr)]
        rows = [step_rows(i, d) for d in range(2)]
        s0 = each(lambda j, d, pr: st_ref[d, pr])
        proj = each(lambda j, d, pr: _bdot_nt(jnp.concatenate([op("at", d, pr), op("rt", d, pr)], axis=0), s0[j]))
        yield
        u_st = each(lambda j, d, pr: _bdot_tn(tinv_ref[slot, j], dup(proj[j][:c]) + aakv_ref[slot, j]))
        yield
        y_st = each(lambda j, d, pr: _bdot(arbk_ref[slot, j], jnp.concatenate(
            [u_st[j].astype(BF16), dup(op("v", d, pr)).astype(BF16)], axis=0)))
        upd = each(lambda j, d, pr: _bdot_tn(jnp.concatenate([unstack(u_st[j]).astype(BF16), op("v", d, pr)], axis=0),
                                             jnp.concatenate([op("bt", d, pr), op("kt", d, pr)], axis=0)))
        yield
        for j, (d, pr) in enumerate(everyone):
            st_ref[d, pr] = (s0[j] + jnp.where(same_head, upd[j], 0.0)) * plast_ref[slot, d][:, lanes(pr)]
            y_acc[rows[d], lanes(pr)] += proj[j][c:] + unstack(y_st[j])
        for d in range(2):
            b_acc[rows[d], :] += bonus_ref[slot, d]

    def interleave(main, staging):
        main = list(main)
        held = [[g, None] for g in staging]
        turn = 0
        while main:
            for entry in list(main):
                g, every = entry
                if turn % every == 0 and next(g, "done") == "done":
                    main.remove(entry)
            for s in held:
                if s[1] is None:
                    s[1] = next(s[0])
            turn += 1
        for g, _ in held:
            for _ in g:
                pass

    interleave([(prepare(0, d, 0), 1) for d in range(2)], [])
    interleave([(independent_half(0), 1)], [prepare(1, d, 1) for d in range(2)])

    def step(i, carry):
        slot = i % 2
        nxt = jnp.minimum(i + 1, n_steps - 1)
        interleave([(dependent_half(1 - slot, i - 1), 1), (independent_half(slot), 1)],
                   [prepare(nxt, d, 1 - slot) for d in range(2)])
        return carry

    lax.fori_loop(1, n_steps, step, 0)
    interleave([(dependent_half((n_steps - 1) % 2, n_steps - 1), 1)], [])

    def finish(i, carry):
        rows = pl.ds(pl.multiple_of(i * ROW_TILE, ROW_TILE), ROW_TILE)
        y = y_acc[rows, :]
        mean = _head_sum_mxu(y, ones_bd) * (1.0 / HD)
        yc = y - mean
        var = _head_sum_mxu(yc * yc, ones_bd) * (1.0 / HD)
        gn = yc * lax.rsqrt(var + RWKV_GN_EPS) * lnw_ref[...] + lnb_ref[...]
        o_ref[rows, :] = ((gn + b_acc[rows, :]) * _silu(z_ref[rows, :].astype(F32))).astype(o_ref.dtype)
        return carry

    lax.fori_loop(0, t_rows // ROW_TILE, finish, 0)


def _rwkv(u, n_lat_rows, col_shift, col_z, rw_mu, rw_w0, rw_w2, rw_a0, rw_a2, rw_kk, rw_ka, rw_rk, rw_ln_w,
          rw_ln_b):
    b, t, _ = u.shape
    c = RW_CHUNK
    assert c == HD
    col_lo = col_shift + 3 * W
    def pad_dirs(w2):
        z = jnp.zeros_like(w2[0])
        return jnp.stack([jnp.concatenate([w2[0], z], axis=0), jnp.concatenate([z, w2[1]], axis=0)]).astype(BF16)

    row = lambda a: a.reshape(1, W).astype(F32)
    blk = lambda col, width=W: pl.BlockSpec((None, t, width), lambda bi, col=col, width=width: (bi, 0, col // width))
    full = lambda shape: pl.BlockSpec(shape, lambda bi: (0,) * len(shape))
    return pl.pallas_call(
        functools.partial(_rwkv_kernel, n_lat=n_lat_rows // c, n_ctx=(t - n_lat_rows) // c),
        grid=(b,),
        in_specs=[
            blk(col_shift), blk(col_shift + W), blk(col_shift + 2 * W), blk(col_lo, 2 * LANES), blk(col_z),
            full((3, 1, W)), full((1, 2 * LANES)), full((1, W)), full((1, W)), full((2, 1, W)), full((2, 1, W)),
            full((2, 2 * LORA, W)), full((2, 2 * LORA, W)), full((1, W)), full((1, W)), full((1, W)),
        ],
        out_specs=pl.BlockSpec((None, t, W), lambda bi: (bi, 0, 0)),
        out_shape=jax.ShapeDtypeStruct((b, t, W), BF16),
        scratch_shapes=[pltpu.VMEM((t, W), F32), pltpu.VMEM((t, W), F32),
                        pltpu.VMEM((2, PAIRS, LANES, LANES), F32), pltpu.VMEM((2, 2, 5, c, W), BF16),
                        pltpu.VMEM((2, 2, c, W), F32), pltpu.VMEM((2, 2, 1, W), F32), pltpu.VMEM((2, 2 * PAIRS, 2 * c, 2 * c), BF16),
                        pltpu.VMEM((2, 2 * PAIRS, 2 * c, LANES), F32), pltpu.VMEM((2, 2 * PAIRS, 2 * c, 4 * c), BF16)],
        compiler_params=_cparams("parallel"),
        name="rwkv7",
    )(u, u, u, u, u, rw_mu[:3 * W].reshape(3, 1, W), rw_mu[3 * W:].reshape(1, 2 * LANES), row(rw_kk), row(rw_ka),
      rw_w0.reshape(2, 1, W), rw_a0.reshape(2, 1, W), pad_dirs(rw_w2), pad_dirs(rw_a2), row(rw_rk), row(rw_ln_w),
      row(rw_ln_b))


def _out_proj_kernel(*refs, n_src, lat_tiles, tail, ctx_cols):
    a_ref, b_ref, w_ref, mod_ref = refs[n_src:n_src + 4]
    wa = a_ref.shape[-1]
    y = _bdot(a_ref[...], w_ref[:wa, :]) + _bdot(b_ref[...], w_ref[wa:, :])
    hn = _read_token_rows(refs[:n_src], lat_tiles) + mod_ref[2:3, :] * y
    if tail == "final_norm":
        fnw_ref, o_ref = refs[n_src + 4:]
        o_ref[...] = hn * lax.rsqrt(jnp.mean(hn * hn, axis=-1, keepdims=True) + NORM_EPS) * fnw_ref[...]
    else:
        nw_ref, mod2_ref, w2_ref, o_ref, u_ref = refs[n_src + 4:]
        o_ref[...] = hn
        yn = hn * lax.rsqrt(jnp.mean(hn * hn, axis=-1, keepdims=True) + NORM_EPS) * nw_ref[...]
        n = yn * (1.0 + mod2_ref[1:2, :]) + mod2_ref[0:1, :]
        is_ctx = pl.program_id(1) >= lat_tiles

        @pl.when(jnp.logical_not(is_ctx))
        def _():
            u_ref[...] = _bdot(n, w2_ref[...]).astype(u_ref.dtype)

        @pl.when(is_ctx)
        def _():
            lo, hi = ctx_cols
            u_ref[...] = jnp.zeros_like(u_ref)
            u_ref[:, lo:hi] = _bdot(n, w2_ref[:, lo:hi]).astype(u_ref.dtype)


def _out_proj(a, b, w, h, mod, n_lat, final_norm_w=None, next_proj=None):
    bsz, rows, wa = a.shape
    d = w.shape[1]
    tm = ROW_TILE
    lat_tiles = n_lat // tm
    src_specs, srcs = _token_rows(h, tm, lat_tiles)
    kind = lambda bi, i: (bi, (i >= lat_tiles).astype(jnp.int32), 0, 0)
    in_specs = src_specs + [
        pl.BlockSpec((None, tm, wa), lambda bi, i: (bi, i, 0)),
        pl.BlockSpec((None, tm, b.shape[-1]), lambda bi, i: (bi, i, 0)),
        pl.BlockSpec(w.shape, lambda bi, i: (0, 0)),
        pl.BlockSpec((None, None, 3, d), kind),
    ]
    args = srcs + [a, b, w, mod]
    out_specs = pl.BlockSpec((None, tm, d), lambda bi, i: (bi, i, 0))
    out_shape = jax.ShapeDtypeStruct((bsz, rows, d), F32)
    ctx_cols = None
    if final_norm_w is not None:
        tail = "final_norm"
        in_specs.append(pl.BlockSpec((1, d), lambda bi, i: (0, 0)))
        args.append(final_norm_w.reshape(1, d))
    else:
        tail = "next_proj"
        norm_w, mod2, w2, ctx_cols = next_proj
        n2 = w2.shape[1]
        in_specs += [pl.BlockSpec((1, d), lambda bi, i: (0, 0)), pl.BlockSpec((None, None, 3, d), kind),
                     pl.BlockSpec((d, n2), lambda bi, i: (0, 0))]
        args += [norm_w.reshape(1, d), mod2, w2]
        out_specs = [out_specs, pl.BlockSpec((None, tm, n2), lambda bi, i: (bi, i, 0))]
        out_shape = [out_shape, jax.ShapeDtypeStruct((bsz, rows, n2), BF16)]
    return pl.pallas_call(
        functools.partial(_out_proj_kernel, n_src=len(srcs), lat_tiles=lat_tiles, tail=tail, ctx_cols=ctx_cols),
        grid=(bsz, rows // tm),
        in_specs=in_specs,
        out_specs=out_specs,
        out_shape=out_shape,
        compiler_params=_cparams("parallel", "parallel"),
        name="out_proj_" + tail,
    )(*args)


HY_FAST = 64
HY_SUB = 8
HY_CH_TILE = 256
HY_UNROLL = 32


def _hy_dft_tables(l):
    n = 2 * l
    half = l // HY_FAST
    k1 = jnp.arange(half, dtype=jnp.int32)[None, :, None]
    n1 = jnp.arange(half, dtype=jnp.int32)[None, None, :]
    n2 = jnp.arange(HY_FAST, dtype=jnp.int32)[:, None, None]
    ang = (jnp.pi / n) * (((2 * k1 + 1) * (HY_FAST * n1 + n2)) % (2 * n)).astype(F32)
    m1 = jnp.concatenate([jnp.cos(ang), -jnp.sin(ang)], axis=1)
    groups = HY_FAST // HY_SUB
    m1g = m1.reshape(groups, HY_SUB, 2 * half, half)
    jj = np.arange(HY_SUB)[:, None, None]
    row_sel = (np.arange(2 * half * HY_SUB)[None, :, None] == np.arange(2 * half)[None, None, :] * HY_SUB + jj)
    col_sel = (np.arange(half * HY_SUB)[None, None, :] == np.arange(half)[None, :, None] * HY_SUB + jj)
    place = lambda out: jnp.einsum("jRr,gjrn,jnC->" + out, row_sel.astype(np.float32), m1g,
                                   col_sel.astype(np.float32), precision=lax.Precision.HIGHEST).astype(BF16)
    big, big_t = place("gRC"), place("gCR")
    kk = jnp.arange(HY_FAST, dtype=jnp.int32)
    ang2 = (2.0 * jnp.pi / HY_FAST) * ((kk[:, None] * kk[None, :]) % HY_FAST).astype(F32)
    wr, wi = jnp.cos(ang2), -jnp.sin(ang2)
    w2 = jnp.concatenate([jnp.concatenate([wr, -wi], axis=1), jnp.concatenate([wi, wr], axis=1)], axis=0)
    w2_inv = jnp.concatenate([jnp.concatenate([wr, wi], axis=1), jnp.concatenate([-wi, wr], axis=1)], axis=0)
    return big, big_t, w2.astype(BF16), w2_inv.astype(BF16)


def _hy_stage1(src_ref, m1_ref, bs_ref):
    half, _, c = src_ref.shape
    for g in range(HY_FAST // HY_SUB):
        cols = pl.ds(g * HY_SUB, HY_SUB)
        x = src_ref[:, cols, :].reshape(half * HY_SUB, c).astype(BF16)
        out = jnp.dot(m1_ref[g], x, preferred_element_type=F32)
        bs_ref[:, cols, :] = out.reshape(2 * half, HY_SUB, c)


def _hy_stage2(bs_ref, w2_ref, k1):
    half = bs_ref.shape[0] // 2
    return jnp.dot(w2_ref[...], jnp.concatenate([bs_ref[k1], bs_ref[half + k1]], axis=0).astype(BF16),
                   preferred_element_type=F32)


def _hy_filter_kernel(z_ref, w1_ref, b1_ref, f1_ref, w2_ref, b2_ref, f2_ref, w3_ref, dec_ref, o_ref, hid_ref):
    j = pl.program_id(0)

    @pl.when(j == 0)
    def _():
        hid = jnp.sin(f1_ref[...] * (_dot_hi(z_ref[...], w1_ref[...]) + b1_ref[...]))
        hid_ref[...] = jnp.sin(f2_ref[...] * (_dot_hi(hid, w2_ref[...]) + b2_ref[...]))

    h = _dot_hi(hid_ref[...], w3_ref[...]) * dec_ref[...]
    row = lax.broadcasted_iota(jnp.int32, h.shape, 0)
    o_ref[...] = jnp.where(jnp.logical_and(j % 2 == 1, row == 0), 0.0, h).astype(o_ref.dtype)


def _hy_filters(l, w1, b1, f1, w2, b2, f2, w3):
    t = jnp.linspace(0.0, 1.0, l)[:, None]
    bands = jnp.linspace(1e-4, HY_BANDS - 1, HY_BANDS)
    ang = (2.0 * math.pi / l) * jnp.arange(l, dtype=F32)[:, None] * bands[None]
    z = jnp.concatenate([t, jnp.cos(ang), -jnp.sin(ang)], axis=-1)
    emb = z.shape[1]
    emb_pad = 64
    z = jnp.pad(z, ((0, 0), (0, emb_pad - emb)))
    w1p = jnp.pad(w1, ((0, emb_pad - emb), (0, 0)))
    deltas = jnp.abs(jnp.linspace(math.log(HY_TARGET) / HY_SLOW_PCT, math.log(HY_TARGET) / HY_FAST_PCT, W))
    dec = jnp.exp(-t * deltas)
    ffn = w1.shape[1]
    nblk = w3.shape[1] // W
    full = lambda shape: pl.BlockSpec(shape, lambda j: (0,) * len(shape))
    return pl.pallas_call(
        _hy_filter_kernel,
        grid=(nblk,),
        in_specs=[full((l, emb_pad)), full((emb_pad, ffn)), full((1, ffn)), full((1, ffn)), full((ffn, ffn)),
                  full((1, ffn)), full((1, ffn)), pl.BlockSpec((ffn, W), lambda j: (0, j)), full((l, W))],
        out_specs=pl.BlockSpec((l, W), lambda j: (0, j)),
        out_shape=jax.ShapeDtypeStruct((l, nblk * W), F32),
        scratch_shapes=[pltpu.VMEM((l, ffn), F32)],
        compiler_params=_cparams("arbitrary"),
        name="hyena_filters",
    )(z, w1p, b1.reshape(1, -1), f1.reshape(1, -1), w2, b2.reshape(1, -1), f2.reshape(1, -1), w3, dec)


def _hy_spectrum_kernel(hf_ref, hb_ref, m1_ref, w2_ref, o_ref, bs_ref, xf_ref, *, half):
    _hy_stage1(hf_ref, m1_ref, bs_ref)

    def forward(k1, carry):
        xf_ref[k1] = _hy_stage2(bs_ref, w2_ref, k1)
        return carry

    lax.fori_loop(0, half, forward, 0, unroll=min(HY_UNROLL, half))
    _hy_stage1(hb_ref, m1_ref, bs_ref)

    def combine(k1, carry):
        xf = xf_ref[k1]
        xb = _hy_stage2(bs_ref, w2_ref, k1)
        o_ref[k1] = jnp.concatenate([xf[:HY_FAST] + xb[:HY_FAST], xf[HY_FAST:] - xb[HY_FAST:]],
                                    axis=0).astype(o_ref.dtype)
        return carry

    lax.fori_loop(0, half, combine, 0, unroll=min(HY_UNROLL, half))


def _hy_spectra(filt, m1, w2):
    l = filt.shape[0]
    half = l // HY_FAST
    ct = HY_CH_TILE
    nct = W // ct
    orders = filt.shape[1] // (2 * W)
    filt = filt.reshape(half, HY_FAST, filt.shape[1])
    return pl.pallas_call(
        functools.partial(_hy_spectrum_kernel, half=half),
        grid=(orders, nct),
        in_specs=[pl.BlockSpec((half, HY_FAST, ct), lambda o, c: (0, 0, 2 * o * nct + c)),
                  pl.BlockSpec((half, HY_FAST, ct), lambda o, c: (0, 0, (2 * o + 1) * nct + c)),
                  pl.BlockSpec(m1.shape, lambda o, c: (0, 0, 0)), pl.BlockSpec(w2.shape, lambda o, c: (0, 0))],
        out_specs=pl.BlockSpec((None, half, 2 * HY_FAST, ct), lambda o, c: (o, 0, 0, c)),
        out_shape=jax.ShapeDtypeStruct((orders, half, 2 * HY_FAST, W), BF16),
        scratch_shapes=[pltpu.VMEM((2 * half, HY_FAST, ct), F32), pltpu.VMEM((half, 2 * HY_FAST, ct), F32)],
        compiler_params=_cparams("parallel", "parallel"),
        name="hyena_spectra",
    )(filt, filt, m1, w2)


def _hyena_kernel(v_ref, x1_ref, x2_ref, g_ref, cw_ref, cb_ref, bias_ref, m1_ref, m1t_ref, w2_ref, w2i_ref, gs_ref,
                  o_ref, z_ref, bs_ref, cs_ref, acc_ref, *, l):
    o = pl.program_id(2)
    rc = ROW_TILE
    half = l // HY_FAST
    ct = z_ref.shape[-1]

    def conv_rows(ref, sec, c0):
        x, prev, nxt = _with_neighbours(ref, slice(None), c0, rc, l, l)
        w = cw_ref[sec]
        return prev * w[0:1, :] + x * w[1:2, :] + nxt * w[2:3, :] + cb_ref[sec]

    slabs = rc // HY_FAST

    @pl.when(o == 0)
    def _():
        def body(i, carry):
            c0 = pl.multiple_of(i * rc, rc)
            z_ref[pl.ds(i * slabs, slabs)] = conv_rows(v_ref, 0, c0).reshape(slabs, HY_FAST, ct)
            return carry

        lax.fori_loop(0, l // rc, body, 0)

    _hy_stage1(z_ref, m1_ref, bs_ref)

    def per_k1(k1, carry):
        x = _hy_stage2(bs_ref, w2_ref, k1)
        xr, xi = x[:HY_FAST], x[HY_FAST:]
        g = gs_ref[k1].astype(F32)
        gr, gi = g[:HY_FAST], g[HY_FAST:]
        y = jnp.concatenate([xr * gr - xi * gi, xr * gi + xi * gr], axis=0).astype(BF16)
        cs_ref[k1] = jnp.dot(w2i_ref[...], y, preferred_element_type=F32)
        return carry

    lax.fori_loop(0, half, per_k1, 0, unroll=min(HY_UNROLL, half))

    for g in range(HY_FAST // HY_SUB):
        c_re = cs_ref[:, pl.ds(g * HY_SUB, HY_SUB), :].reshape(half * HY_SUB, ct)
        c_im = cs_ref[:, pl.ds(HY_FAST + g * HY_SUB, HY_SUB), :].reshape(half * HY_SUB, ct)
        out = jnp.dot(m1t_ref[g], jnp.concatenate([c_re, c_im], axis=0).astype(BF16), preferred_element_type=F32)
        acc_ref[:, pl.ds(g * HY_SUB, HY_SUB), :] = out.reshape(half, HY_SUB, ct)

    def finish(i, carry):
        c0 = pl.multiple_of(i * rc, rc)
        rows = pl.ds(c0, rc)
        tile = pl.ds(i * slabs, slabs)
        conv = (acc_ref[tile] * (1.0 / l) + z_ref[tile] * bias_ref[o]).reshape(rc, ct)

        @pl.when(o == 0)
        def _():
            z_ref[tile] = (conv_rows(x1_ref, 1, c0) * conv).reshape(slabs, HY_FAST, ct)

        @pl.when(o == 1)
        def _():
            zn = conv_rows(x2_ref, 2, c0) * conv
            o_ref[rows, :] = (zn * _silu(g_ref[rows, :].astype(F32))).astype(o_ref.dtype)

        return carry

    lax.fori_loop(0, l // rc, finish, 0)


def _hyena(u, l, conv_w, conv_b, hy_bias, tables, spectra):
    b = u.shape[0]
    ct = HY_CH_TILE
    nct = W // ct
    half = l // HY_FAST
    sec = lambda s: pl.BlockSpec((None, l, ct), lambda bi, c, o, s=s: (bi, 0, s * nct + c))
    cw = jnp.transpose(conv_w.reshape(3, 3, W), (1, 0, 2))
    whole = lambda a: pl.BlockSpec(a.shape, lambda bi, c, o: (0,) * a.ndim)
    return pl.pallas_call(
        functools.partial(_hyena_kernel, l=l),
        grid=(b, nct, 2),
        in_specs=[
            sec(0), sec(1), sec(2), sec(3),
            pl.BlockSpec((3, 3, ct), lambda bi, c, o: (0, 0, c)),
            pl.BlockSpec((3, 1, ct), lambda bi, c, o: (0, 0, c)),
            pl.BlockSpec((2, 1, ct), lambda bi, c, o: (0, 0, c)),
            whole(tables[0]), whole(tables[1]), whole(tables[2]), whole(tables[3]),
            pl.BlockSpec((None, half, 2 * HY_FAST, ct), lambda bi, c, o: (o, 0, 0, c)),
        ],
        out_specs=pl.BlockSpec((None, l, ct), lambda bi, c, o: (bi, 0, c)),
        out_shape=jax.ShapeDtypeStruct((b, l, W), BF16),
        scratch_shapes=[pltpu.VMEM((half, HY_FAST, ct), F32), pltpu.VMEM((2 * half, HY_FAST, ct), F32),
                        pltpu.VMEM((half, 2 * HY_FAST, ct), F32), pltpu.VMEM((half, HY_FAST, ct), F32)],
        compiler_params=_cparams("parallel", "parallel", "arbitrary"),
        name="hyena",
    )(u, u, u, u, cw, conv_b.reshape(3, 1, W), hy_bias.reshape(2, 1, W), *tables, spectra)


NA_ROWS_PER_STEP = 32
NEG_BIG = -1e30


def _na_bias_tables(rpb, rows):
    heads = rpb.shape[0]
    n_dr, n_dc = 2 * NA_WIN_R - 1, 2 * NA_WIN_C - 1
    cq = np.arange(GRID_W)[:, None]
    kc = np.arange(GRID_W)[None, :]
    c0 = np.clip(cq - NA_WIN_C // 2, 0, GRID_W - NA_WIN_C)
    col_ok = (kc >= c0) & (kc < c0 + NA_WIN_C)
    col_sel = (np.arange(n_dc)[:, None, None] == (kc - cq + NA_WIN_C - 1)[None]).astype(np.float32)
    tiles = jnp.einsum("phrj,jcx->phrcx", rpb.astype(F32).reshape(heads // 2, 2, n_dr, n_dc), col_sel,
                       precision=lax.Precision.HIGHEST)
    tiles = jnp.where(col_ok, tiles, NEG_BIG)
    tiles = jnp.pad(tiles, ((0, 0), (0, 0), (0, 2), (0, 0), (0, 0)), constant_values=NEG_BIG)
    return jnp.concatenate([tiles[:, :, :n_dr + 1], tiles[:, :, 1:]], axis=-1)


def _na_kernel(q_ref, k_ref, v_ref, z_ref, bias_ref, o_ref, *, l):
    rows = l // GRID_W
    kn = NA_WIN_R * GRID_W
    n_ctx = k_ref.shape[0] - l
    lane_lo = _lane_ids((1, LANES)) < HD
    k_ctx = k_ref[pl.ds(l, n_ctx), :]
    v_ctx = v_ref[pl.ds(l, n_ctx), :]
    per_step = min(NA_ROWS_PER_STEP, rows)

    def body(it, carry):
        each = lambda fn: [fn(j, it * per_step + j) for j in range(per_step)]
        r0 = each(lambda j, rq: jnp.clip(rq - NA_WIN_R // 2, 0, rows - NA_WIN_R))
        qrows = each(lambda j, rq: pl.ds(pl.multiple_of(rq * GRID_W, GRID_W), GRID_W))
        krows = each(lambda j, rq: pl.ds(pl.multiple_of(r0[j] * GRID_W, GRID_W), kn))

        def stacked_q(j, rq):
            q = q_ref[qrows[j], :].astype(F32) * HD ** -0.5
            return jnp.concatenate([jnp.where(lane_lo, q, 0.0), jnp.where(lane_lo, 0.0, q)], axis=0).astype(BF16)

        def window_bias(j, rq):
            base = r0[j] - rq + NA_WIN_R - 1
            return jnp.concatenate(
                [jnp.concatenate([bias_ref[hh, base + w] for w in range(0, NA_WIN_R, 2)], axis=1) for hh in range(2)],
                axis=0)

        q2 = each(stacked_q)
        s_win = each(lambda j, rq: _bdot_nt(q2[j], k_ref[krows[j], :]) + window_bias(j, rq))
        s_ctx = each(lambda j, rq: _bdot_nt(q2[j], k_ctx))
        m = each(lambda j, rq: jnp.maximum(jnp.max(s_win[j], axis=-1, keepdims=True),
                                           jnp.max(s_ctx[j], axis=-1, keepdims=True)))
        p_win = each(lambda j, rq: jnp.exp(s_win[j] - m[j]))
        p_ctx = each(lambda j, rq: jnp.exp(s_ctx[j] - m[j]))
        den = each(lambda j, rq: jnp.sum(p_win[j], axis=-1, keepdims=True) + jnp.sum(p_ctx[j], axis=-1, keepdims=True))
        out = each(lambda j, rq: (_bdot(p_win[j], v_ref[krows[j], :]) + _bdot(p_ctx[j], v_ctx)) / den[j])
        for j in range(per_step):
            gate = _silu(z_ref[qrows[j], :].astype(F32))
            o_ref[qrows[j], :] = (jnp.where(lane_lo, out[j][:GRID_W], out[j][GRID_W:]) * gate).astype(o_ref.dtype)
        return carry

    lax.fori_loop(0, rows // per_step, body, 0)


def _natten(u, l, col_q, col_k, col_v, col_z, bias):
    b, t, _ = u.shape
    lat = lambda col: pl.BlockSpec((None, l, LANES), lambda bi, p, col=col: (bi, 0, col // LANES + p))
    full = lambda col: pl.BlockSpec((None, t, LANES), lambda bi, p, col=col: (bi, 0, col // LANES + p))
    return pl.pallas_call(
        functools.partial(_na_kernel, l=l),
        grid=(b, PAIRS),
        in_specs=[lat(col_q), full(col_k), full(col_v), lat(col_z),
                  pl.BlockSpec((None,) + bias.shape[1:], lambda bi, p: (p, 0, 0, 0, 0))],
        out_specs=pl.BlockSpec((None, l, LANES), lambda bi, p: (bi, 0, p)),
        out_shape=jax.ShapeDtypeStruct((b, l, W), BF16),
        compiler_params=_cparams("parallel", "parallel"),
        name="natten",
    )(u, u, u, u, bias)


def kernel(x, c, ctx, c_ctx, ada_w, ada_b, norm_w, final_norm_w, even_w_in, even_w_out, ret_decay, rw_mu, rw_w0, rw_w2, rw_a0, rw_a2, rw_kk, rw_ka, rw_rk, rw_ln_w, rw_ln_b, odd_w_in, odd_w_out, hy_conv_w, hy_conv_b, hy_w1, hy_b1, hy_f1, hy_w2, hy_b2, hy_f2, hy_w3, hy_bias, na_rpb):
    b, l, d = x.shape
    n_ctx = ctx.shape[1]
    t = l + n_ctx
    depth = ada_w.shape[0]
    assert depth == 2 and l % ROW_TILE == 0 and n_ctx % ROW_TILE == 0 and (l // GRID_W) >= NA_WIN_R
    assert (l // GRID_W) % min(NA_ROWS_PER_STEP, l // GRID_W) == 0

    cond_rows = -(-(b + 1) // 8) * 8
    cond = jnp.concatenate([c, c_ctx[None, :], jnp.zeros((cond_rows - b - 1, d), F32)], axis=0)
    mod_all = _ada_mod(cond, ada_w, ada_b)
    mods = [jnp.stack([mod_all[i, :b].reshape(b, 3, d),
                       jnp.broadcast_to(mod_all[i, b].reshape(1, 3, d), (b, 3, d))], axis=1) for i in range(depth)]

    h = (x, ctx)

    col_rz, col_wz, col_shift = 3 * W, 4 * W, 5 * W
    u0 = _norm_proj(h, norm_w[0], mods[0], even_w_in[0].astype(BF16), l, t)
    cos_t, sin_t = _rope_tables(l, t)
    ret_o = _retention(u0, cos_t, sin_t, ret_decay[0], l, 0, W, 2 * W, col_rz)
    rw_o = _rwkv(u0, l, col_shift, col_wz, rw_mu[0], rw_w0[0], rw_w2[0], rw_a0[0], rw_a2[0], rw_kk[0], rw_ka[0],
                 rw_rk[0], rw_ln_w[0], rw_ln_b[0])
    h, u1 = _out_proj(ret_o, rw_o, even_w_out[0].astype(BF16), h, mods[0], l,
                      next_proj=(norm_w[1], mods[1], odd_w_in[0].astype(BF16), (5 * W, 7 * W)))
    tables = _hy_dft_tables(l)
    filt = _hy_filters(l, hy_w1[0], hy_b1[0], hy_f1[0], hy_w2[0], hy_b2[0], hy_f2[0], hy_w3[0])
    spectra = _hy_spectra(filt, tables[0], tables[2])
    hy_o = _hyena(u1, l, hy_conv_w[0], hy_conv_b[0], hy_bias[0], tables, spectra)
    na_o = _natten(u1, l, 4 * W, 5 * W, 6 * W, 7 * W, _na_bias_tables(na_rpb[0], l // GRID_W))
    return _out_proj(hy_o, na_o, odd_w_out[0].astype(BF16), h, mods[1], l, final_norm_w)
```

```python
import functools
import math

import jax
import jax.numpy as jnp
import numpy as np
from jax import lax
from jax.experimental import pallas as pl
from jax.experimental.pallas import tpu as pltpu

F32 = jnp.float32
BF16 = jnp.bfloat16

GRID_W = 64
NORM_EPS = 1e-6
ROPE_BASE = 10000.0
HEADS = 8
HD = 64
W = HEADS * HD
LORA = 64
RWKV_GN_EPS = 64e-5
HY_BANDS = 16
HY_TARGET = 1e-2
HY_FAST_PCT = 0.3
HY_SLOW_PCT = 1.5
NA_WIN_R = 8
NA_WIN_C = 16
LANES = 128
PAIRS = W // LANES
RET_CHUNK = 128
RET_STEPS = 2
RW_CHUNK = 64
RW_SUB = 32
RW_DEPTH = 2
ROW_TILE = 256
HALO = 16
VMEM_LIMIT = 60 * 1024 * 1024


def _cparams(*sem):
    return pltpu.CompilerParams(dimension_semantics=sem, vmem_limit_bytes=VMEM_LIMIT)


def _bdot(a, b):
    return jnp.dot(a.astype(BF16), b.astype(BF16), preferred_element_type=F32)


def _bdot_nt(a, b):
    return lax.dot_general(a.astype(BF16), b.astype(BF16), (((1,), (1,)), ((), ())), preferred_element_type=F32)


def _bdot_tn(a, b):
    return lax.dot_general(a.astype(BF16), b.astype(BF16), (((0,), (0,)), ((), ())), preferred_element_type=F32)


def _dot_hi(a, b):
    ah = a.astype(BF16)
    al = (a - ah.astype(F32)).astype(BF16)
    bh = b.astype(BF16)
    bl = (b - bh.astype(F32)).astype(BF16)
    d = lambda p, q: jnp.dot(p, q, preferred_element_type=F32)
    return d(ah, bh) + d(ah, bl) + d(al, bh)


def _silu(x):
    return x * (1.0 / (1.0 + jnp.exp(-x)))


def _sigmoid(x):
    return 1.0 / (1.0 + jnp.exp(-x))


def _lane_ids(shape):
    return lax.broadcasted_iota(jnp.int32, shape, len(shape) - 1)


def _row_ids(shape):
    return lax.broadcasted_iota(jnp.int32, shape, len(shape) - 2)


def _head_sum_mxu(x, ones_bd):
    hi = x.astype(BF16)
    lo = (x - hi.astype(F32)).astype(BF16)
    tiles = []
    for p in range(x.shape[-1] // LANES):
        sl = slice(p * LANES, (p + 1) * LANES)
        tiles.append(jnp.dot(hi[:, sl], ones_bd, preferred_element_type=F32)
                     + jnp.dot(lo[:, sl], ones_bd, preferred_element_type=F32))
    return tiles[0] if len(tiles) == 1 else jnp.concatenate(tiles, axis=-1)


def _with_neighbours(ref, cols, c0, c, n_rows, lat_rows):
    row = lax.broadcasted_iota(jnp.int32, (c, 1), 0)
    x = ref[pl.ds(c0, c), cols].astype(F32)
    p0 = pl.multiple_of(jnp.maximum(c0 - HALO, 0), HALO)
    n0 = pl.multiple_of(jnp.minimum(c0 + c, n_rows - HALO), HALO)
    prev_row = ref[pl.ds(p0, HALO), cols].astype(F32)[HALO - 1:HALO, :]
    next_row = ref[pl.ds(n0, HALO), cols].astype(F32)[0:1, :]
    has_prev = jnp.logical_and(c0 != 0, c0 != lat_rows)
    has_next = jnp.logical_and(c0 + c != lat_rows, c0 + c != n_rows)
    prev_row = jnp.where(has_prev, prev_row, 0.0)
    next_row = jnp.where(has_next, next_row, 0.0)
    prev = jnp.where(row == 0, prev_row, pltpu.roll(x, 1, 0))
    nxt = jnp.where(row == c - 1, next_row, pltpu.roll(x, c - 1, 0))
    return x, prev, nxt


def _scan_chunk_id(i, n_lat, n_ctx, reverse):
    if reverse:
        return jnp.where(i < n_ctx, n_lat + n_ctx - 1 - i, n_lat - 1 - (i - n_ctx))
    return jnp.where(i < n_ctx, n_lat + i, i - n_ctx)


def _ada_kernel(c_ref, w_ref, b_ref, o_ref):
    o_ref[...] = _dot_hi(_silu(c_ref[...]), w_ref[...]) + b_ref[...]


def _ada_mod(cond, ada_w, ada_b):
    depth, d, d3 = ada_w.shape
    rows = cond.shape[0]
    return pl.pallas_call(
        _ada_kernel,
        grid=(depth, d3 // d),
        in_specs=[
            pl.BlockSpec((rows, d), lambda i, j: (0, 0)),
            pl.BlockSpec((None, d, d), lambda i, j: (i, 0, j)),
            pl.BlockSpec((None, 1, d), lambda i, j: (i, 0, j)),
        ],
        out_specs=pl.BlockSpec((None, rows, d), lambda i, j: (i, 0, j)),
        out_shape=jax.ShapeDtypeStruct((depth, rows, d3), F32),
        compiler_params=_cparams("parallel", "parallel"),
        name="ada_mod",
    )(cond, ada_w, ada_b.reshape(depth, 1, d3))


def _token_rows(h, tm, lat_tiles):
    if isinstance(h, tuple):
        d = h[0].shape[-1]
        return [pl.BlockSpec((None, tm, d), lambda bi, i: (bi, jnp.minimum(i, lat_tiles - 1), 0)),
                pl.BlockSpec((None, tm, d), lambda bi, i: (bi, jnp.maximum(i - lat_tiles, 0), 0))], list(h)
    return [pl.BlockSpec((None, tm, h.shape[-1]), lambda bi, i: (bi, i, 0))], [h]


def _read_token_rows(refs, lat_tiles):
    if len(refs) == 2:
        return jnp.where(pl.program_id(1) >= lat_tiles, refs[1][...], refs[0][...])
    return refs[0][...]


def _norm_proj_kernel(*refs, n_src, lat_tiles):
    nw_ref, mod_ref, w_ref, o_ref = refs[n_src:]
    x = _read_token_rows(refs[:n_src], lat_tiles)
    y = x * lax.rsqrt(jnp.mean(x * x, axis=-1, keepdims=True) + NORM_EPS) * nw_ref[...]
    n = y * (1.0 + mod_ref[1:2, :]) + mod_ref[0:1, :]
    o_ref[...] = _bdot(n, w_ref[...]).astype(o_ref.dtype)


def _norm_proj(h, norm_w, mod, w, n_lat, t):
    d, n = w.shape
    b = mod.shape[0]
    tm = ROW_TILE
    lat_tiles = n_lat // tm
    src_specs, srcs = _token_rows(h, tm, lat_tiles)
    return pl.pallas_call(
        functools.partial(_norm_proj_kernel, n_src=len(srcs), lat_tiles=lat_tiles),
        grid=(b, t // tm),
        in_specs=src_specs + [
            pl.BlockSpec((1, d), lambda bi, i: (0, 0)),
            pl.BlockSpec((None, None, 3, d), lambda bi, i: (bi, (i >= lat_tiles).astype(jnp.int32), 0, 0)),
            pl.BlockSpec((d, n), lambda bi, i: (0, 0)),
        ],
        out_specs=pl.BlockSpec((None, tm, n), lambda bi, i: (bi, i, 0)),
        out_shape=jax.ShapeDtypeStruct((b, t, n), BF16),
        compiler_params=_cparams("parallel", "parallel"),
        name="norm_proj",
    )(*srcs, norm_w.reshape(1, d), mod, w)


def _ret_kernel(q_ref, k_ref, v_ref, z_ref, cos_ref, sin_ref, rd_ref, o_ref, y_acc, st_ref, dm_ref, qd_ref, kd_ref,
                *, n_lat, n_ctx):
    c = RET_CHUNK
    lane = _lane_ids((1, LANES))
    lane_lo = lane < HD
    rope_lo = (lane % HD) < (HD // 2)
    pos = lax.broadcasted_iota(jnp.int32, (c, 1), 0).astype(F32)
    ti = lax.broadcasted_iota(jnp.int32, (c, c), 0)
    si = lax.broadcasted_iota(jnp.int32, (c, c), 1)
    same_head = (_row_ids((LANES, LANES)) < HD) == (_lane_ids((LANES, LANES)) < HD)
    ones_bd = same_head.astype(BF16)
    inst = [(d, pr) for d in range(2) for pr in range(PAIRS)]
    sl = lambda pr: slice(pr * LANES, (pr + 1) * LANES)

    g_chunk = []
    for d, pr in inst:
        lg_a = -jnp.exp(rd_ref[d, 2 * pr])
        lg_b = -jnp.exp(rd_ref[d, 2 * pr + 1])
        lgv = jnp.where(lane_lo, lg_a, lg_b)
        diff = ((ti - si) if d == 0 else (si - ti)).astype(F32)
        keep = diff >= 0
        dmat = lambda lg: jnp.where(keep, jnp.exp(jnp.where(keep, diff, 0.0) * lg), 0.0)
        dm_ref[d, pr] = jnp.concatenate([dmat(lg_a), dmat(lg_b)], axis=0)
        qd_ref[d, :, sl(pr)] = jnp.exp(((pos + 1.0) if d == 0 else (c - pos)) * lgv)
        kd_ref[d, :, sl(pr)] = jnp.exp(((c - 1.0 - pos) if d == 0 else pos) * lgv)
        g_chunk.append(jnp.exp(c * lgv))
    y_acc[...] = jnp.zeros_like(y_acc)
    st_ref[...] = jnp.zeros_like(st_ref)

    def rope(x, rows):
        partner = jnp.where(rope_lo, pltpu.roll(x, LANES - HD // 2, 1), pltpu.roll(x, HD // 2, 1))
        return x * cos_ref[rows, :] + partner * sin_ref[rows, :]

    def stack2(x):
        return jnp.concatenate([jnp.where(lane_lo, x, 0.0), jnp.where(lane_lo, 0.0, x)], axis=0)

    steps = max(s for s in range(1, RET_STEPS + 1) if (n_lat + n_ctx) % s == 0)

    def step(i, carry):
        work = [(sub,) + dp for sub in range(steps) for dp in inst]
        rows = [[pl.ds(pl.multiple_of(_scan_chunk_id(steps * i + sub, n_lat, n_ctx, d == 1) * c, c), c)
                 for d in range(2)] for sub in range(steps)]
        each = lambda fn: [fn(j, *work[j]) for j in range(len(work))]
        q = each(lambda j, sub, d, pr: rope(q_ref[rows[sub][d], sl(pr)].astype(F32), rows[sub][d]))
        k = each(lambda j, sub, d, pr: rope(k_ref[rows[sub][d], sl(pr)].astype(F32) * HD ** -0.5, rows[sub][d]))
        v = each(lambda j, sub, d, pr: v_ref[rows[sub][d], sl(pr)])
        scores = each(lambda j, sub, d, pr: _bdot_nt(stack2(q[j]), k[j]) * dm_ref[d, pr])
        upd = each(lambda j, sub, d, pr: _bdot_tn(k[j] * kd_ref[d, :, sl(pr)], v[j]))
        intra = each(lambda j, sub, d, pr: _bdot(scores[j], v[j]))
        state = [st_ref[d, pr] for d, pr in inst]
        for sub in range(steps):
            for n, (d, pr) in enumerate(inst):
                j = sub * len(inst) + n
                cross = _bdot(q[j] * qd_ref[d, :, sl(pr)], state[n])
                state[n] = state[n] * g_chunk[n] + jnp.where(same_head, upd[j], 0.0)
                y_acc[rows[sub][d], sl(pr)] += jnp.where(lane_lo, intra[j][:c], intra[j][c:]) + cross
        for n, (d, pr) in enumerate(inst):
            st_ref[d, pr] = state[n]
        return carry

    lax.fori_loop(0, (n_lat + n_ctx) // steps, step, 0)

    def finish(i, carry):
        rows = pl.ds(pl.multiple_of(i * ROW_TILE, ROW_TILE), ROW_TILE)
        y = y_acc[rows, :]
        ms = _head_sum_mxu(y * y, ones_bd) * (1.0 / HD)
        o_ref[rows, :] = (y * lax.rsqrt(ms + NORM_EPS) * _silu(z_ref[rows, :].astype(F32))).astype(o_ref.dtype)
        return carry

    lax.fori_loop(0, (n_lat + n_ctx) * c // ROW_TILE, finish, 0)


def _retention(u, cos_t, sin_t, ret_decay, n_lat_rows, col_q, col_k, col_v, col_z):
    b, t, _ = u.shape
    c = RET_CHUNK
    rd = jnp.broadcast_to(ret_decay.astype(F32)[:, :, None, None], (2, HEADS, 1, LANES))
    blk = lambda col: pl.BlockSpec((None, t, W), lambda bi, col=col: (bi, 0, col // W))
    return pl.pallas_call(
        functools.partial(_ret_kernel, n_lat=n_lat_rows // c, n_ctx=(t - n_lat_rows) // c),
        grid=(b,),
        in_specs=[
            blk(col_q), blk(col_k), blk(col_v), blk(col_z),
            pl.BlockSpec((t, LANES), lambda bi: (0, 0)),
            pl.BlockSpec((t, LANES), lambda bi: (0, 0)),
            pl.BlockSpec((2, HEADS, 1, LANES), lambda bi: (0, 0, 0, 0)),
        ],
        out_specs=pl.BlockSpec((None, t, W), lambda bi: (bi, 0, 0)),
        out_shape=jax.ShapeDtypeStruct((b, t, W), BF16),
        scratch_shapes=[pltpu.VMEM((t, W), F32), pltpu.VMEM((2, PAIRS, LANES, LANES), F32),
                        pltpu.VMEM((2, PAIRS, 2 * c, c), F32), pltpu.VMEM((2, c, W), F32),
                        pltpu.VMEM((2, c, W), F32)],
        compiler_params=_cparams("parallel"),
        name="retention",
    )(u, u, u, u, cos_t, sin_t, rd)


def _rope_tables(n_lat_rows, t):
    pos = jnp.arange(n_lat_rows)
    row = (pos // GRID_W).astype(F32)
    col = (pos % GRID_W).astype(F32)
    nf = HD // 4
    inv = ROPE_BASE ** (-jnp.arange(nf, dtype=F32) / nf)
    ang = jnp.concatenate([row[:, None] * inv, col[:, None] * inv], axis=-1)
    cos, sin = jnp.cos(ang), jnp.sin(ang)
    cos_h = jnp.concatenate([cos, cos], axis=-1)
    sin_h = jnp.concatenate([-sin, sin], axis=-1)
    reps = LANES // HD
    cos_t = jnp.concatenate([jnp.tile(cos_h, (1, reps)), jnp.ones((t - n_lat_rows, LANES), F32)], axis=0)
    sin_t = jnp.concatenate([jnp.tile(sin_h, (1, reps)), jnp.zeros((t - n_lat_rows, LANES), F32)], axis=0)
    return cos_t, sin_t


def _rwkv_kernel(r_ref, k_ref, v_ref, lo_ref, z_ref, mu_ref, mulo_ref, kkw_ref, ka_ref, w0_ref, a0_ref, w2_ref,
                 a2_ref, rk_ref, lnw_ref, lnb_ref, o_ref, y_acc, b_acc, st_ref, prep_ref, bonus_ref, plast_ref, tinv_ref,
                 aakv_ref, arbk_ref, *, n_lat, n_ctx):
    c = RW_CHUNK
    c2 = 2 * c
    t_rows = (n_lat + n_ctx) * c
    lat_rows = n_lat * c
    lane_lo = _lane_ids((1, LANES)) < HD
    ri = _row_ids((c2, c2))
    ci = _lane_ids((c2, c2))
    same_head = (ri < c) == (ci < c)
    rt_, ct_ = ri % c, ci % c
    same_sub = (rt_ // RW_SUB) == (ct_ // RW_SUB)
    eye = (ri == ci).astype(F32)
    ones_bd = same_head.astype(BF16)
    t64 = lax.broadcasted_iota(jnp.int32, (c, c), 0)
    s64 = lax.broadcasted_iota(jnp.int32, (c, c), 1)
    masks = []
    for d in range(2):
        before = (ct_ < rt_) if d == 0 else (ct_ > rt_)
        strict = jnp.logical_and(same_head, before)
        incl = jnp.logical_and(same_head, jnp.logical_or(before, ct_ == rt_))
        tri = ((s64 <= t64) if d == 0 else (s64 >= t64)).astype(BF16)
        masks.append((strict, incl, jnp.logical_and(strict, same_sub), tri))

    y_acc[...] = jnp.zeros_like(y_acc)
    b_acc[...] = jnp.zeros_like(b_acc)
    st_ref[...] = jnp.zeros_like(st_ref)

    def stack2(x):
        return jnp.concatenate([jnp.where(lane_lo, x, 0.0), jnp.where(lane_lo, 0.0, x)], axis=0)

    def dup(x):
        return jnp.concatenate([x, x], axis=0)

    def unstack(x):
        return jnp.where(lane_lo, x[:c], x[c:])

    sh_t = lax.broadcasted_iota(jnp.int32, (c, c + 2 * HALO), 0) + HALO
    sh_s = lax.broadcasted_iota(jnp.int32, (c, c + 2 * HALO), 1)
    shift_base = jnp.where(jnp.abs(sh_s - sh_t) == 1, 0.5, 0.0)

    n_steps = n_lat + n_ctx
    fields = ("at", "bt", "kt", "rt", "v")

    def step_rows(i, d):
        return pl.ds(pl.multiple_of(_scan_chunk_id(i, n_lat, n_ctx, d == 1) * c, c), c)

    def prepare(i, d, slot):
        tri = masks[d][3]
        c0 = pl.multiple_of(_scan_chunk_id(i, n_lat, n_ctx, d == 1) * c, c)
        p0 = pl.multiple_of(jnp.maximum(c0 - HALO, 0), HALO)
        n0 = pl.multiple_of(jnp.minimum(c0 + c, t_rows - HALO), HALO)
        no_prev = jnp.logical_or(c0 == 0, c0 == lat_rows)
        no_next = jnp.logical_or(c0 + c == lat_rows, c0 + c == t_rows)
        dead_prev = jnp.where(no_prev, HALO - 1, -1)
        dead_next = jnp.where(no_next, HALO + c, -1)
        dead = jnp.logical_or(sh_s == dead_prev, sh_s == dead_next)
        shift = jnp.where(dead, 0.0, shift_base).astype(BF16)

        def mixed(ref, mu):
            x = ref[pl.ds(c0, c), :]
            ext = jnp.concatenate([ref[pl.ds(p0, HALO), :], x, ref[pl.ds(n0, HALO), :]], axis=0)
            x = x.astype(F32)
            return x + mu * (jnp.dot(shift, ext, preferred_element_type=F32) - x)

        r = mixed(r_ref, mu_ref[0])
        k = mixed(k_ref, mu_ref[1])
        v = mixed(v_ref, mu_ref[2])
        lo = mixed(lo_ref, mulo_ref[...])
        kk = k * kkw_ref[...]
        yield
        kk_norm = jnp.sqrt(_head_sum_mxu(kk * kk, ones_bd))
        w_pre = _bdot(jnp.tanh(lo[:, :LANES]), w2_ref[d])
        a_pre = _bdot(lo[:, LANES:], a2_ref[d])
        yield
        kk = kk / jnp.maximum(kk_norm, 1e-12)
        logw = -math.exp(-0.5) * _sigmoid(w0_ref[d] + w_pre)
        a = _sigmoid(a0_ref[d] + a_pre)
        k_d = k * (1.0 + (a - 1.0) * ka_ref[...])
        lw_hi = logw.astype(BF16)
        lw_lo = (logw - lw_hi.astype(F32)).astype(BF16)
        cum = jnp.dot(tri, lw_hi, preferred_element_type=F32) + jnp.dot(tri, lw_lo, preferred_element_type=F32)
        bonus_sum = jnp.concatenate(
            [_bdot((r * k_d * rk_ref[...])[:, pr * LANES:(pr + 1) * LANES], ones_bd) for pr in range(PAIRS)], axis=1)
        yield
        p = jnp.exp(cum)
        inv_p = jnp.exp(-cum)
        staged = dict(at=-kk * jnp.exp(cum - logw), bt=kk * a * inv_p, kt=k_d * inv_p, rt=r * p, v=v,
                      bonus=bonus_sum * v)
        yield "stores next"
        for f, name in enumerate(fields):
            prep_ref[slot, d, f] = staged[name].astype(BF16)
        bonus_ref[slot, d] = staged["bonus"]
        plast_ref[slot, d] = p[c - 1:c, :] if d == 0 else p[0:1, :]

    everyone = [(d, pr) for d in range(2) for pr in range(PAIRS)]
    each = lambda fn: [fn(j, *everyone[j]) for j in range(len(everyone))]
    lanes = lambda pr: slice(pr * LANES, (pr + 1) * LANES)

    def independent_half(slot):
        op = lambda name, d, pr: prep_ref[slot, d, fields.index(name), :, lanes(pr)]
        rhs = each(lambda j, d, pr: jnp.concatenate([dup(op("bt", d, pr)), dup(op("kt", d, pr))], axis=0).astype(BF16))
        top = each(lambda j, d, pr: _bdot_nt(stack2(op("at", d, pr)), rhs[j]))
        yield
        n_all = each(lambda j, d, pr: jnp.where(masks[d][0], top[j][:, :c2], 0.0))
        aak_v = each(lambda j, d, pr: _bdot(jnp.where(masks[d][0], top[j][:, c2:], 0.0),
                                            dup(op("v", d, pr)).astype(BF16)))
        n_tr = each(lambda j, d, pr: n_all[j].T)
        nsub_tr = each(lambda j, d, pr: jnp.where(masks[1 - d][2], n_tr[j], 0.0))
        t_tr = each(lambda j, d, pr: (eye + nsub_tr[j]).astype(BF16))
        p_tr = each(lambda j, d, pr: _bdot(nsub_tr[j], nsub_tr[j]).astype(BF16))
        yield
        span = 2
        while 2 * span < RW_SUB:
            both = each(lambda j, d, pr: _bdot(p_tr[j], jnp.concatenate([t_tr[j], p_tr[j]], axis=1)).astype(BF16))
            t_tr = each(lambda j, d, pr: t_tr[j] + both[j][:, :c2])
            p_tr = each(lambda j, d, pr: both[j][:, c2:])
            span *= 2
            yield
        t_tr = each(lambda j, d, pr: t_tr[j] + _bdot(p_tr[j], t_tr[j]).astype(BF16))
        yield
        corr = each(lambda j, d, pr: _bdot(n_tr[j] - nsub_tr[j], t_tr[j]))
        bot = each(lambda j, d, pr: _bdot_nt(stack2(op("rt", d, pr)), rhs[j]))
        yield
        tinv_tr = each(lambda j, d, pr: t_tr[j] + _bdot(t_tr[j], corr[j]))
        for j, (d, pr) in enumerate(everyone):
            tinv_ref[slot, j] = tinv_tr[j].astype(BF16)
            aakv_ref[slot, j] = aak_v[j]
            arbk_ref[slot, j] = jnp.where(jnp.concatenate([masks[d][1], masks[d][1]], axis=1), bot[j], 0.0).astype(BF16)

    def dependent_half(slot, i):
        op = lambda name, d, pr: prep_ref[slot, d, fields.index(name), :, lanes(pr)]
        rows = [step_rows(i, d) for d in range(2)]
        s0 = each(lambda j, d, pr: st_ref[d, pr])
        proj = each(lambda j, d, pr: _bdot_nt(jnp.concatenate([op("at", d, pr), op("rt", d, pr)], axis=0), s0[j]))
        yield
        u_st = each(lambda j, d, pr: _bdot_tn(tinv_ref[slot, j], dup(proj[j][:c]) + aakv_ref[slot, j]))
        yield
        y_st = each(lambda j, d, pr: _bdot(arbk_ref[slot, j], jnp.concatenate(
            [u_st[j].astype(BF16), dup(op("v", d, pr)).astype(BF16)], axis=0)))
        upd = each(lambda j, d, pr: _bdot_tn(jnp.concatenate([unstack(u_st[j]).astype(BF16), op("v", d, pr)], axis=0),
                                             jnp.concatenate([op("bt", d, pr), op("kt", d, pr)], axis=0)))
        yield
        for j, (d, pr) in enumerate(everyone):
            st_ref[d, pr] = (s0[j] + jnp.where(same_head, upd[j], 0.0)) * plast_ref[slot, d][:, lanes(pr)]
            y_acc[rows[d], lanes(pr)] += proj[j][c:] + unstack(y_st[j])
        for d in range(2):
            b_acc[rows[d], :] += bonus_ref[slot, d]

    def interleave(main, staging):
        main = list(main)
        held = [[g, None] for g in staging]
        turn = 0
        while main:
            for entry in list(main):
                g, every = entry
                if turn % every == 0 and next(g, "done") == "done":
                    main.remove(entry)
            for s in held:
                if s[1] is None:
                    s[1] = next(s[0])
            turn += 1
        for g, _ in held:
            for _ in g:
                pass

    slots = 2 * RW_DEPTH
    group = range(RW_DEPTH)

    def in_turn(gens):
        for g in gens:
            yield from g

    def dependent_halves(first):
        return (in_turn([dependent_half((first + s) % slots, first + s) for s in group]), 1)

    def stagers(first):
        return [prepare(jnp.minimum(first + s, n_steps - 1), d, (first + s) % slots) for s in group for d in range(2)]

    interleave([(g, 1) for g in stagers(0)], [])
    interleave([(independent_half(s), 1) for s in group], stagers(RW_DEPTH))

    def step(i, carry):
        base = i * RW_DEPTH
        interleave([dependent_halves(base - RW_DEPTH)] + [(independent_half((base + s) % slots), 1) for s in group],
                   stagers(base + RW_DEPTH))
        return carry

    lax.fori_loop(1, n_steps // RW_DEPTH, step, 0)
    interleave([dependent_halves(n_steps - RW_DEPTH)], [])

    def finish(i, carry):
        rows = pl.ds(pl.multiple_of(i * ROW_TILE, ROW_TILE), ROW_TILE)
        y = y_acc[rows, :]
        mean = _head_sum_mxu(y, ones_bd) * (1.0 / HD)
        yc = y - mean
        var = _head_sum_mxu(yc * yc, ones_bd) * (1.0 / HD)
        gn = yc * lax.rsqrt(var + RWKV_GN_EPS) * lnw_ref[...] + lnb_ref[...]
        o_ref[rows, :] = ((gn + b_acc[rows, :]) * _silu(z_ref[rows, :].astype(F32))).astype(o_ref.dtype)
        return carry

    lax.fori_loop(0, t_rows // ROW_TILE, finish, 0)


def _rwkv(u, n_lat_rows, col_shift, col_z, rw_mu, rw_w0, rw_w2, rw_a0, rw_a2, rw_kk, rw_ka, rw_rk, rw_ln_w,
          rw_ln_b):
    b, t, _ = u.shape
    c = RW_CHUNK
    assert c == HD and (t // c) % RW_DEPTH == 0
    slots = 2 * RW_DEPTH
    col_lo = col_shift + 3 * W
    def pad_dirs(w2):
        z = jnp.zeros_like(w2[0])
        return jnp.stack([jnp.concatenate([w2[0], z], axis=0), jnp.concatenate([z, w2[1]], axis=0)]).astype(BF16)

    row = lambda a: a.reshape(1, W).astype(F32)
    blk = lambda col, width=W: pl.BlockSpec((None, t, width), lambda bi, col=col, width=width: (bi, 0, col // width))
    full = lambda shape: pl.BlockSpec(shape, lambda bi: (0,) * len(shape))
    return pl.pallas_call(
        functools.partial(_rwkv_kernel, n_lat=n_lat_rows // c, n_ctx=(t - n_lat_rows) // c),
        grid=(b,),
        in_specs=[
            blk(col_shift), blk(col_shift + W), blk(col_shift + 2 * W), blk(col_lo, 2 * LANES), blk(col_z),
            full((3, 1, W)), full((1, 2 * LANES)), full((1, W)), full((1, W)), full((2, 1, W)), full((2, 1, W)),
            full((2, 2 * LORA, W)), full((2, 2 * LORA, W)), full((1, W)), full((1, W)), full((1, W)),
        ],
        out_specs=pl.BlockSpec((None, t, W), lambda bi: (bi, 0, 0)),
        out_shape=jax.ShapeDtypeStruct((b, t, W), BF16),
        scratch_shapes=[pltpu.VMEM((t, W), F32), pltpu.VMEM((t, W), F32),
                        pltpu.VMEM((2, PAIRS, LANES, LANES), F32), pltpu.VMEM((slots, 2, 5, c, W), BF16),
                        pltpu.VMEM((slots, 2, c, W), F32), pltpu.VMEM((slots, 2, 1, W), F32),
                        pltpu.VMEM((slots, 2 * PAIRS, 2 * c, 2 * c), BF16),
                        pltpu.VMEM((slots, 2 * PAIRS, 2 * c, LANES), F32),
                        pltpu.VMEM((slots, 2 * PAIRS, 2 * c, 4 * c), BF16)],
        compiler_params=_cparams("parallel"),
        name="rwkv7",
    )(u, u, u, u, u, rw_mu[:3 * W].reshape(3, 1, W), rw_mu[3 * W:].reshape(1, 2 * LANES), row(rw_kk), row(rw_ka),
      rw_w0.reshape(2, 1, W), rw_a0.reshape(2, 1, W), pad_dirs(rw_w2), pad_dirs(rw_a2), row(rw_rk), row(rw_ln_w),
      row(rw_ln_b))


def _out_proj_kernel(*refs, n_src, lat_tiles, tail, ctx_cols):
    a_ref, b_ref, w_ref, mod_ref = refs[n_src:n_src + 4]
    wa = a_ref.shape[-1]
    y = _bdot(a_ref[...], w_ref[:wa, :]) + _bdot(b_ref[...], w_ref[wa:, :])
    hn = _read_token_rows(refs[:n_src], lat_tiles) + mod_ref[2:3, :] * y
    if tail == "final_norm":
        fnw_ref, o_ref = refs[n_src + 4:]
        o_ref[...] = hn * lax.rsqrt(jnp.mean(hn * hn, axis=-1, keepdims=True) + NORM_EPS) * fnw_ref[...]
    else:
        nw_ref, mod2_ref, w2_ref, o_ref, u_ref = refs[n_src + 4:]
        o_ref[...] = hn
        yn = hn * lax.rsqrt(jnp.mean(hn * hn, axis=-1, keepdims=True) + NORM_EPS) * nw_ref[...]
        n = yn * (1.0 + mod2_ref[1:2, :]) + mod2_ref[0:1, :]
        is_ctx = pl.program_id(1) >= lat_tiles

        @pl.when(jnp.logical_not(is_ctx))
        def _():
            u_ref[...] = _bdot(n, w2_ref[...]).astype(u_ref.dtype)

        @pl.when(is_ctx)
        def _():
            lo, hi = ctx_cols
            u_ref[...] = jnp.zeros_like(u_ref)
            u_ref[:, lo:hi] = _bdot(n, w2_ref[:, lo:hi]).astype(u_ref.dtype)


def _out_proj(a, b, w, h, mod, n_lat, final_norm_w=None, next_proj=None):
    bsz, rows, wa = a.shape
    d = w.shape[1]
    tm = ROW_TILE
    lat_tiles = n_lat // tm
    src_specs, srcs = _token_rows(h, tm, lat_tiles)
    kind = lambda bi, i: (bi, (i >= lat_tiles).astype(jnp.int32), 0, 0)
    in_specs = src_specs + [
        pl.BlockSpec((None, tm, wa), lambda bi, i: (bi, i, 0)),
        pl.BlockSpec((None, tm, b.shape[-1]), lambda bi, i: (bi, i, 0)),
        pl.BlockSpec(w.shape, lambda bi, i: (0, 0)),
        pl.BlockSpec((None, None, 3, d), kind),
    ]
    args = srcs + [a, b, w, mod]
    out_specs = pl.BlockSpec((None, tm, d), lambda bi, i: (bi, i, 0))
    out_shape = jax.ShapeDtypeStruct((bsz, rows, d), F32)
    ctx_cols = None
    if final_norm_w is not None:
        tail = "final_norm"
        in_specs.append(pl.BlockSpec((1, d), lambda bi, i: (0, 0)))
        args.append(final_norm_w.reshape(1, d))
    else:
        tail = "next_proj"
        norm_w, mod2, w2, ctx_cols = next_proj
        n2 = w2.shape[1]
        in_specs += [pl.BlockSpec((1, d), lambda bi, i: (0, 0)), pl.BlockSpec((None, None, 3, d), kind),
                     pl.BlockSpec((d, n2), lambda bi, i: (0, 0))]
        args += [norm_w.reshape(1, d), mod2, w2]
        out_specs = [out_specs, pl.BlockSpec((None, tm, n2), lambda bi, i: (bi, i, 0))]
        out_shape = [out_shape, jax.ShapeDtypeStruct((bsz, rows, n2), BF16)]
    return pl.pallas_call(
        functools.partial(_out_proj_kernel, n_src=len(srcs), lat_tiles=lat_tiles, tail=tail, ctx_cols=ctx_cols),
        grid=(bsz, rows // tm),
        in_specs=in_specs,
        out_specs=out_specs,
        out_shape=out_shape,
        compiler_params=_cparams("parallel", "parallel"),
        name="out_proj_" + tail,
    )(*args)


HY_FAST = 64
HY_SUB = 8
HY_CH_TILE = 256
HY_UNROLL = 32


def _hy_dft_tables(l):
    n = 2 * l
    half = l // HY_FAST
    k1 = jnp.arange(half, dtype=jnp.int32)[None, :, None]
    n1 = jnp.arange(half, dtype=jnp.int32)[None, None, :]
    n2 = jnp.arange(HY_FAST, dtype=jnp.int32)[:, None, None]
    ang = (jnp.pi / n) * (((2 * k1 + 1) * (HY_FAST * n1 + n2)) % (2 * n)).astype(F32)
    m1 = jnp.concatenate([jnp.cos(ang), -jnp.sin(ang)], axis=1)
    groups = HY_FAST // HY_SUB
    m1g = m1.reshape(groups, HY_SUB, 2 * half, half)
    jj = np.arange(HY_SUB)[:, None, None]
    row_sel = (np.arange(2 * half * HY_SUB)[None, :, None] == np.arange(2 * half)[None, None, :] * HY_SUB + jj)
    col_sel = (np.arange(half * HY_SUB)[None, None, :] == np.arange(half)[None, :, None] * HY_SUB + jj)
    place = lambda out: jnp.einsum("jRr,gjrn,jnC->" + out, row_sel.astype(np.float32), m1g,
                                   col_sel.astype(np.float32), precision=lax.Precision.HIGHEST).astype(BF16)
    big, big_t = place("gRC"), place("gCR")
    kk = jnp.arange(HY_FAST, dtype=jnp.int32)
    ang2 = (2.0 * jnp.pi / HY_FAST) * ((kk[:, None] * kk[None, :]) % HY_FAST).astype(F32)
    wr, wi = jnp.cos(ang2), -jnp.sin(ang2)
    w2 = jnp.concatenate([jnp.concatenate([wr, -wi], axis=1), jnp.concatenate([wi, wr], axis=1)], axis=0)
    w2_inv = jnp.concatenate([jnp.concatenate([wr, wi], axis=1), jnp.concatenate([-wi, wr], axis=1)], axis=0)
    return big, big_t, w2.astype(BF16), w2_inv.astype(BF16)


def _hy_stage1(src_ref, m1_ref, bs_ref):
    half, _, c = src_ref.shape
    for g in range(HY_FAST // HY_SUB):
        cols = pl.ds(g * HY_SUB, HY_SUB)
        x = src_ref[:, cols, :].reshape(half * HY_SUB, c).astype(BF16)
        out = jnp.dot(m1_ref[g], x, preferred_element_type=F32)
        bs_ref[:, cols, :] = out.reshape(2 * half, HY_SUB, c)


def _hy_stage2(bs_ref, w2_ref, k1):
    half = bs_ref.shape[0] // 2
    return jnp.dot(w2_ref[...], jnp.concatenate([bs_ref[k1], bs_ref[half + k1]], axis=0).astype(BF16),
                   preferred_element_type=F32)


def _hy_filter_kernel(z_ref, w1_ref, b1_ref, f1_ref, w2_ref, b2_ref, f2_ref, w3_ref, dec_ref, o_ref, hid_ref):
    j = pl.program_id(0)

    @pl.when(j == 0)
    def _():
        hid = jnp.sin(f1_ref[...] * (_dot_hi(z_ref[...], w1_ref[...]) + b1_ref[...]))
        hid_ref[...] = jnp.sin(f2_ref[...] * (_dot_hi(hid, w2_ref[...]) + b2_ref[...]))

    h = _dot_hi(hid_ref[...], w3_ref[...]) * dec_ref[...]
    row = lax.broadcasted_iota(jnp.int32, h.shape, 0)
    o_ref[...] = jnp.where(jnp.logical_and(j % 2 == 1, row == 0), 0.0, h).astype(o_ref.dtype)


def _hy_filters(l, w1, b1, f1, w2, b2, f2, w3):
    t = jnp.linspace(0.0, 1.0, l)[:, None]
    bands = jnp.linspace(1e-4, HY_BANDS - 1, HY_BANDS)
    ang = (2.0 * math.pi / l) * jnp.arange(l, dtype=F32)[:, None] * bands[None]
    z = jnp.concatenate([t, jnp.cos(ang), -jnp.sin(ang)], axis=-1)
    emb = z.shape[1]
    emb_pad = 64
    z = jnp.pad(z, ((0, 0), (0, emb_pad - emb)))
    w1p = jnp.pad(w1, ((0, emb_pad - emb), (0, 0)))
    deltas = jnp.abs(jnp.linspace(math.log(HY_TARGET) / HY_SLOW_PCT, math.log(HY_TARGET) / HY_FAST_PCT, W))
    dec = jnp.exp(-t * deltas)
    ffn = w1.shape[1]
    nblk = w3.shape[1] // W
    full = lambda shape: pl.BlockSpec(shape, lambda j: (0,) * len(shape))
    return pl.pallas_call(
        _hy_filter_kernel,
        grid=(nblk,),
        in_specs=[full((l, emb_pad)), full((emb_pad, ffn)), full((1, ffn)), full((1, ffn)), full((ffn, ffn)),
                  full((1, ffn)), full((1, ffn)), pl.BlockSpec((ffn, W), lambda j: (0, j)), full((l, W))],
        out_specs=pl.BlockSpec((l, W), lambda j: (0, j)),
        out_shape=jax.ShapeDtypeStruct((l, nblk * W), F32),
        scratch_shapes=[pltpu.VMEM((l, ffn), F32)],
        compiler_params=_cparams("arbitrary"),
        name="hyena_filters",
    )(z, w1p, b1.reshape(1, -1), f1.reshape(1, -1), w2, b2.reshape(1, -1), f2.reshape(1, -1), w3, dec)


def _hy_spectrum_kernel(hf_ref, hb_ref, m1_ref, w2_ref, o_ref, bs_ref, xf_ref, *, half):
    _hy_stage1(hf_ref, m1_ref, bs_ref)

    def forward(k1, carry):
        xf_ref[k1] = _hy_stage2(bs_ref, w2_ref, k1)
        return carry

    lax.fori_loop(0, half, forward, 0, unroll=min(HY_UNROLL, half))
    _hy_stage1(hb_ref, m1_ref, bs_ref)

    def combine(k1, carry):
        xf = xf_ref[k1]
        xb = _hy_stage2(bs_ref, w2_ref, k1)
        o_ref[k1] = jnp.concatenate([xf[:HY_FAST] + xb[:HY_FAST], xf[HY_FAST:] - xb[HY_FAST:]],
                                    axis=0).astype(o_ref.dtype)
        return carry

    lax.fori_loop(0, half, combine, 0, unroll=min(HY_UNROLL, half))


def _hy_spectra(filt, m1, w2):
    l = filt.shape[0]
    half = l // HY_FAST
    ct = HY_CH_TILE
    nct = W // ct
    orders = filt.shape[1] // (2 * W)
    filt = filt.reshape(half, HY_FAST, filt.shape[1])
    return pl.pallas_call(
        functools.partial(_hy_spectrum_kernel, half=half),
        grid=(orders, nct),
        in_specs=[pl.BlockSpec((half, HY_FAST, ct), lambda o, c: (0, 0, 2 * o * nct + c)),
                  pl.BlockSpec((half, HY_FAST, ct), lambda o, c: (0, 0, (2 * o + 1) * nct + c)),
                  pl.BlockSpec(m1.shape, lambda o, c: (0, 0, 0)), pl.BlockSpec(w2.shape, lambda o, c: (0, 0))],
        out_specs=pl.BlockSpec((None, half, 2 * HY_FAST, ct), lambda o, c: (o, 0, 0, c)),
        out_shape=jax.ShapeDtypeStruct((orders, half, 2 * HY_FAST, W), BF16),
        scratch_shapes=[pltpu.VMEM((2 * half, HY_FAST, ct), F32), pltpu.VMEM((half, 2 * HY_FAST, ct), F32)],
        compiler_params=_cparams("parallel", "parallel"),
        name="hyena_spectra",
    )(filt, filt, m1, w2)


def _hyena_kernel(v_ref, x1_ref, x2_ref, g_ref, cw_ref, cb_ref, bias_ref, m1_ref, m1t_ref, w2_ref, w2i_ref, gs_ref,
                  o_ref, z_ref, bs_ref, cs_ref, acc_ref, *, l):
    o = pl.program_id(2)
    rc = ROW_TILE
    half = l // HY_FAST
    ct = z_ref.shape[-1]

    def conv_rows(ref, sec, c0):
        x, prev, nxt = _with_neighbours(ref, slice(None), c0, rc, l, l)
        w = cw_ref[sec]
        return prev * w[0:1, :] + x * w[1:2, :] + nxt * w[2:3, :] + cb_ref[sec]

    slabs = rc // HY_FAST

    @pl.when(o == 0)
    def _():
        def body(i, carry):
            c0 = pl.multiple_of(i * rc, rc)
            z_ref[pl.ds(i * slabs, slabs)] = conv_rows(v_ref, 0, c0).reshape(slabs, HY_FAST, ct)
            return carry

        lax.fori_loop(0, l // rc, body, 0)

    _hy_stage1(z_ref, m1_ref, bs_ref)

    def per_k1(k1, carry):
        x = _hy_stage2(bs_ref, w2_ref, k1)
        xr, xi = x[:HY_FAST], x[HY_FAST:]
        g = gs_ref[k1].astype(F32)
        gr, gi = g[:HY_FAST], g[HY_FAST:]
        y = jnp.concatenate([xr * gr - xi * gi, xr * gi + xi * gr], axis=0).astype(BF16)
        cs_ref[k1] = jnp.dot(w2i_ref[...], y, preferred_element_type=F32)
        return carry

    lax.fori_loop(0, half, per_k1, 0, unroll=min(HY_UNROLL, half))

    for g in range(HY_FAST // HY_SUB):
        c_re = cs_ref[:, pl.ds(g * HY_SUB, HY_SUB), :].reshape(half * HY_SUB, ct)
        c_im = cs_ref[:, pl.ds(HY_FAST + g * HY_SUB, HY_SUB), :].reshape(half * HY_SUB, ct)
        out = jnp.dot(m1t_ref[g], jnp.concatenate([c_re, c_im], axis=0).astype(BF16), preferred_element_type=F32)
        acc_ref[:, pl.ds(g * HY_SUB, HY_SUB), :] = out.reshape(half, HY_SUB, ct)

    def finish(i, carry):
        c0 = pl.multiple_of(i * rc, rc)
        rows = pl.ds(c0, rc)
        tile = pl.ds(i * slabs, slabs)
        conv = (acc_ref[tile] * (1.0 / l) + z_ref[tile] * bias_ref[o]).reshape(rc, ct)

        @pl.when(o == 0)
        def _():
            z_ref[tile] = (conv_rows(x1_ref, 1, c0) * conv).reshape(slabs, HY_FAST, ct)

        @pl.when(o == 1)
        def _():
            zn = conv_rows(x2_ref, 2, c0) * conv
            o_ref[rows, :] = (zn * _silu(g_ref[rows, :].astype(F32))).astype(o_ref.dtype)

        return carry

    lax.fori_loop(0, l // rc, finish, 0)


def _hyena(u, l, conv_w, conv_b, hy_bias, tables, spectra):
    b = u.shape[0]
    ct = HY_CH_TILE
    nct = W // ct
    half = l // HY_FAST
    sec = lambda s: pl.BlockSpec((None, l, ct), lambda bi, c, o, s=s: (bi, 0, s * nct + c))
    cw = jnp.transpose(conv_w.reshape(3, 3, W), (1, 0, 2))
    whole = lambda a: pl.BlockSpec(a.shape, lambda bi, c, o: (0,) * a.ndim)
    return pl.pallas_call(
        functools.partial(_hyena_kernel, l=l),
        grid=(b, nct, 2),
        in_specs=[
            sec(0), sec(1), sec(2), sec(3),
            pl.BlockSpec((3, 3, ct), lambda bi, c, o: (0, 0, c)),
            pl.BlockSpec((3, 1, ct), lambda bi, c, o: (0, 0, c)),
            pl.BlockSpec((2, 1, ct), lambda bi, c, o: (0, 0, c)),
            whole(tables[0]), whole(tables[1]), whole(tables[2]), whole(tables[3]),
            pl.BlockSpec((None, half, 2 * HY_FAST, ct), lambda bi, c, o: (o, 0, 0, c)),
        ],
        out_specs=pl.BlockSpec((None, l, ct), lambda bi, c, o: (bi, 0, c)),
        out_shape=jax.ShapeDtypeStruct((b, l, W), BF16),
        scratch_shapes=[pltpu.VMEM((half, HY_FAST, ct), F32), pltpu.VMEM((2 * half, HY_FAST, ct), F32),
                        pltpu.VMEM((half, 2 * HY_FAST, ct), F32), pltpu.VMEM((half, HY_FAST, ct), F32)],
        compiler_params=_cparams("parallel", "parallel", "arbitrary"),
        name="hyena",
    )(u, u, u, u, cw, conv_b.reshape(3, 1, W), hy_bias.reshape(2, 1, W), *tables, spectra)


NA_ROWS_PER_STEP = 32
NEG_BIG = -1e30


def _na_bias_tables(rpb, rows):
    heads = rpb.shape[0]
    n_dr, n_dc = 2 * NA_WIN_R - 1, 2 * NA_WIN_C - 1
    cq = np.arange(GRID_W)[:, None]
    kc = np.arange(GRID_W)[None, :]
    c0 = np.clip(cq - NA_WIN_C // 2, 0, GRID_W - NA_WIN_C)
    col_ok = (kc >= c0) & (kc < c0 + NA_WIN_C)
    col_sel = (np.arange(n_dc)[:, None, None] == (kc - cq + NA_WIN_C - 1)[None]).astype(np.float32)
    tiles = jnp.einsum("phrj,jcx->phrcx", rpb.astype(F32).reshape(heads // 2, 2, n_dr, n_dc), col_sel,
                       precision=lax.Precision.HIGHEST)
    tiles = jnp.where(col_ok, tiles, NEG_BIG)
    tiles = jnp.pad(tiles, ((0, 0), (0, 0), (0, 2), (0, 0), (0, 0)), constant_values=NEG_BIG)
    return jnp.concatenate([tiles[:, :, :n_dr + 1], tiles[:, :, 1:]], axis=-1)


def _na_kernel(q_ref, k_ref, v_ref, z_ref, bias_ref, o_ref, *, l):
    rows = l // GRID_W
    kn = NA_WIN_R * GRID_W
    n_ctx = k_ref.shape[0] - l
    lane_lo = _lane_ids((1, LANES)) < HD
    k_ctx = k_ref[pl.ds(l, n_ctx), :]
    v_ctx = v_ref[pl.ds(l, n_ctx), :]
    per_step = min(NA_ROWS_PER_STEP, rows)

    def body(it, carry):
        each = lambda fn: [fn(j, it * per_step + j) for j in range(per_step)]
        r0 = each(lambda j, rq: jnp.clip(rq - NA_WIN_R // 2, 0, rows - NA_WIN_R))
        qrows = each(lambda j, rq: pl.ds(pl.multiple_of(rq * GRID_W, GRID_W), GRID_W))
        krows = each(lambda j, rq: pl.ds(pl.multiple_of(r0[j] * GRID_W, GRID_W), kn))

        def stacked_q(j, rq):
            q = q_ref[qrows[j], :].astype(F32) * HD ** -0.5
            return jnp.concatenate([jnp.where(lane_lo, q, 0.0), jnp.where(lane_lo, 0.0, q)], axis=0).astype(BF16)

        def window_bias(j, rq):
            base = r0[j] - rq + NA_WIN_R - 1
            return jnp.concatenate(
                [jnp.concatenate([bias_ref[hh, base + w] for w in range(0, NA_WIN_R, 2)], axis=1) for hh in range(2)],
                axis=0)

        q2 = each(stacked_q)
        s_win = each(lambda j, rq: _bdot_nt(q2[j], k_ref[krows[j], :]) + window_bias(j, rq))
        s_ctx = each(lambda j, rq: _bdot_nt(q2[j], k_ctx))
        m = each(lambda j, rq: jnp.maximum(jnp.max(s_win[j], axis=-1, keepdims=True),
                                           jnp.max(s_ctx[j], axis=-1, keepdims=True)))
        p_win = each(lambda j, rq: jnp.exp(s_win[j] - m[j]))
        p_ctx = each(lambda j, rq: jnp.exp(s_ctx[j] - m[j]))
        den = each(lambda j, rq: jnp.sum(p_win[j], axis=-1, keepdims=True) + jnp.sum(p_ctx[j], axis=-1, keepdims=True))
        out = each(lambda j, rq: (_bdot(p_win[j], v_ref[krows[j], :]) + _bdot(p_ctx[j], v_ctx)) / den[j])
        for j in range(per_step):
            gate = _silu(z_ref[qrows[j], :].astype(F32))
            o_ref[qrows[j], :] = (jnp.where(lane_lo, out[j][:GRID_W], out[j][GRID_W:]) * gate).astype(o_ref.dtype)
        return carry

    lax.fori_loop(0, rows // per_step, body, 0)


def _natten(u, l, col_q, col_k, col_v, col_z, bias):
    b, t, _ = u.shape
    lat = lambda col: pl.BlockSpec((None, l, LANES), lambda bi, p, col=col: (bi, 0, col // LANES + p))
    full = lambda col: pl.BlockSpec((None, t, LANES), lambda bi, p, col=col: (bi, 0, col // LANES + p))
    return pl.pallas_call(
        functools.partial(_na_kernel, l=l),
        grid=(b, PAIRS),
        in_specs=[lat(col_q), full(col_k), full(col_v), lat(col_z),
                  pl.BlockSpec((None,) + bias.shape[1:], lambda bi, p: (p, 0, 0, 0, 0))],
        out_specs=pl.BlockSpec((None, l, LANES), lambda bi, p: (bi, 0, p)),
        out_shape=jax.ShapeDtypeStruct((b, l, W), BF16),
        compiler_params=_cparams("parallel", "parallel"),
        name="natten",
    )(u, u, u, u, bias)


def kernel(x, c, ctx, c_ctx, ada_w, ada_b, norm_w, final_norm_w, even_w_in, even_w_out, ret_decay, rw_mu, rw_w0, rw_w2, rw_a0, rw_a2, rw_kk, rw_ka, rw_rk, rw_ln_w, rw_ln_b, odd_w_in, odd_w_out, hy_conv_w, hy_conv_b, hy_w1, hy_b1, hy_f1, hy_w2, hy_b2, hy_f2, hy_w3, hy_bias, na_rpb):
    b, l, d = x.shape
    n_ctx = ctx.shape[1]
    t = l + n_ctx
    depth = ada_w.shape[0]
    assert depth == 2 and l % ROW_TILE == 0 and n_ctx % ROW_TILE == 0 and (l // GRID_W) >= NA_WIN_R
    assert (l // GRID_W) % min(NA_ROWS_PER_STEP, l // GRID_W) == 0

    cond_rows = -(-(b + 1) // 8) * 8
    cond = jnp.concatenate([c, c_ctx[None, :], jnp.zeros((cond_rows - b - 1, d), F32)], axis=0)
    mod_all = _ada_mod(cond, ada_w, ada_b)
    mods = [jnp.stack([mod_all[i, :b].reshape(b, 3, d),
                       jnp.broadcast_to(mod_all[i, b].reshape(1, 3, d), (b, 3, d))], axis=1) for i in range(depth)]

    h = (x, ctx)

    col_rz, col_wz, col_shift = 3 * W, 4 * W, 5 * W
    u0 = _norm_proj(h, norm_w[0], mods[0], even_w_in[0].astype(BF16), l, t)
    cos_t, sin_t = _rope_tables(l, t)
    ret_o = _retention(u0, cos_t, sin_t, ret_decay[0], l, 0, W, 2 * W, col_rz)
    rw_o = _rwkv(u0, l, col_shift, col_wz, rw_mu[0], rw_w0[0], rw_w2[0], rw_a0[0], rw_a2[0], rw_kk[0], rw_ka[0],
                 rw_rk[0], rw_ln_w[0], rw_ln_b[0])
    h, u1 = _out_proj(ret_o, rw_o, even_w_out[0].astype(BF16), h, mods[0], l,
                      next_proj=(norm_w[1], mods[1], odd_w_in[0].astype(BF16), (5 * W, 7 * W)))
    tables = _hy_dft_tables(l)
    filt = _hy_filters(l, hy_w1[0], hy_b1[0], hy_f1[0], hy_w2[0], hy_b2[0], hy_f2[0], hy_w3[0])
    spectra = _hy_spectra(filt, tables[0], tables[2])
    hy_o = _hyena(u1, l, hy_conv_w[0], hy_conv_b[0], hy_bias[0], tables, spectra)
    na_o = _natten(u1, l, 4 * W, 5 * W, 6 * W, 7 * W, _na_bias_tables(na_rpb[0], l // GRID_W))
    return _out_proj(hy_o, na_o, odd_w_out[0].astype(BF16), h, mods[1], l, final_norm_w)
```

```python
import functools
import math

import jax
import jax.numpy as jnp
import numpy as np
from jax import lax
from jax.experimental import pallas as pl
from jax.experimental.pallas import tpu as pltpu

F32 = jnp.float32
BF16 = jnp.bfloat16

GRID_W = 64
NORM_EPS = 1e-6
ROPE_BASE = 10000.0
HEADS = 8
HD = 64
W = HEADS * HD
LORA = 64
RWKV_GN_EPS = 64e-5
HY_BANDS = 16
HY_TARGET = 1e-2
HY_FAST_PCT = 0.3
HY_SLOW_PCT = 1.5
NA_WIN_R = 8
NA_WIN_C = 16
LANES = 128
PAIRS = W // LANES
RET_CHUNK = 128
RET_STEPS = 2
RW_CHUNK = 64
RW_SUB = 32
RW_DEPTH = 2
ROW_TILE = 256
HALO = 16
VMEM_LIMIT = 60 * 1024 * 1024


def _cparams(*sem):
    return pltpu.CompilerParams(dimension_semantics=sem, vmem_limit_bytes=VMEM_LIMIT)


def _bdot(a, b):
    return jnp.dot(a.astype(BF16), b.astype(BF16), preferred_element_type=F32)


def _bdot_nt(a, b):
    return lax.dot_general(a.astype(BF16), b.astype(BF16), (((1,), (1,)), ((), ())), preferred_element_type=F32)


def _bdot_tn(a, b):
    return lax.dot_general(a.astype(BF16), b.astype(BF16), (((0,), (0,)), ((), ())), preferred_element_type=F32)


def _dot_hi(a, b):
    ah = a.astype(BF16)
    al = (a - ah.astype(F32)).astype(BF16)
    bh = b.astype(BF16)
    bl = (b - bh.astype(F32)).astype(BF16)
    d = lambda p, q: jnp.dot(p, q, preferred_element_type=F32)
    return d(ah, bh) + d(ah, bl) + d(al, bh)


def _silu(x):
    return x * (1.0 / (1.0 + jnp.exp(-x)))


def _sigmoid(x):
    return 1.0 / (1.0 + jnp.exp(-x))


def _lane_ids(shape):
    return lax.broadcasted_iota(jnp.int32, shape, len(shape) - 1)


def _row_ids(shape):
    return lax.broadcasted_iota(jnp.int32, shape, len(shape) - 2)


def _head_sum_mxu(x, ones_bd):
    hi = x.astype(BF16)
    lo = (x - hi.astype(F32)).astype(BF16)
    tiles = []
    for p in range(x.shape[-1] // LANES):
        sl = slice(p * LANES, (p + 1) * LANES)
        tiles.append(jnp.dot(hi[:, sl], ones_bd, preferred_element_type=F32)
                     + jnp.dot(lo[:, sl], ones_bd, preferred_element_type=F32))
    return tiles[0] if len(tiles) == 1 else jnp.concatenate(tiles, axis=-1)


def _with_neighbours(ref, cols, c0, c, n_rows, lat_rows):
    row = lax.broadcasted_iota(jnp.int32, (c, 1), 0)
    x = ref[pl.ds(c0, c), cols].astype(F32)
    p0 = pl.multiple_of(jnp.maximum(c0 - HALO, 0), HALO)
    n0 = pl.multiple_of(jnp.minimum(c0 + c, n_rows - HALO), HALO)
    prev_row = ref[pl.ds(p0, HALO), cols].astype(F32)[HALO - 1:HALO, :]
    next_row = ref[pl.ds(n0, HALO), cols].astype(F32)[0:1, :]
    has_prev = jnp.logical_and(c0 != 0, c0 != lat_rows)
    has_next = jnp.logical_and(c0 + c != lat_rows, c0 + c != n_rows)
    prev_row = jnp.where(has_prev, prev_row, 0.0)
    next_row = jnp.where(has_next, next_row, 0.0)
    prev = jnp.where(row == 0, prev_row, pltpu.roll(x, 1, 0))
    nxt = jnp.where(row == c - 1, next_row, pltpu.roll(x, c - 1, 0))
    return x, prev, nxt


def _scan_chunk_id(i, n_lat, n_ctx, reverse):
    if reverse:
        return jnp.where(i < n_ctx, n_lat + n_ctx - 1 - i, n_lat - 1 - (i - n_ctx))
    return jnp.where(i < n_ctx, n_lat + i, i - n_ctx)


def _ada_kernel(c_ref, w_ref, b_ref, o_ref):
    o_ref[...] = _dot_hi(_silu(c_ref[...]), w_ref[...]) + b_ref[...]


def _ada_mod(cond, ada_w, ada_b):
    depth, d, d3 = ada_w.shape
    rows = cond.shape[0]
    return pl.pallas_call(
        _ada_kernel,
        grid=(depth, d3 // d),
        in_specs=[
            pl.BlockSpec((rows, d), lambda i, j: (0, 0)),
            pl.BlockSpec((None, d, d), lambda i, j: (i, 0, j)),
            pl.BlockSpec((None, 1, d), lambda i, j: (i, 0, j)),
        ],
        out_specs=pl.BlockSpec((None, rows, d), lambda i, j: (i, 0, j)),
        out_shape=jax.ShapeDtypeStruct((depth, rows, d3), F32),
        compiler_params=_cparams("parallel", "parallel"),
        name="ada_mod",
    )(cond, ada_w, ada_b.reshape(depth, 1, d3))


def _token_rows(h, tm, lat_tiles):
    if isinstance(h, tuple):
        d = h[0].shape[-1]
        return [pl.BlockSpec((None, tm, d), lambda bi, i: (bi, jnp.minimum(i, lat_tiles - 1), 0)),
                pl.BlockSpec((None, tm, d), lambda bi, i: (bi, jnp.maximum(i - lat_tiles, 0), 0))], list(h)
    return [pl.BlockSpec((None, tm, h.shape[-1]), lambda bi, i: (bi, i, 0))], [h]


def _read_token_rows(refs, lat_tiles):
    if len(refs) == 2:
        return jnp.where(pl.program_id(1) >= lat_tiles, refs[1][...], refs[0][...])
    return refs[0][...]


def _norm_proj_kernel(*refs, n_src, lat_tiles):
    nw_ref, mod_ref, w_ref, o_ref = refs[n_src:]
    x = _read_token_rows(refs[:n_src], lat_tiles)
    y = x * lax.rsqrt(jnp.mean(x * x, axis=-1, keepdims=True) + NORM_EPS) * nw_ref[...]
    n = y * (1.0 + mod_ref[1:2, :]) + mod_ref[0:1, :]
    o_ref[...] = _bdot(n, w_ref[...]).astype(o_ref.dtype)


def _norm_proj(h, norm_w, mod, w, n_lat, t):
    d, n = w.shape
    b = mod.shape[0]
    tm = ROW_TILE
    lat_tiles = n_lat // tm
    src_specs, srcs = _token_rows(h, tm, lat_tiles)
    return pl.pallas_call(
        functools.partial(_norm_proj_kernel, n_src=len(srcs), lat_tiles=lat_tiles),
        grid=(b, t // tm),
        in_specs=src_specs + [
            pl.BlockSpec((1, d), lambda bi, i: (0, 0)),
            pl.BlockSpec((None, None, 3, d), lambda bi, i: (bi, (i >= lat_tiles).astype(jnp.int32), 0, 0)),
            pl.BlockSpec((d, n), lambda bi, i: (0, 0)),
        ],
        out_specs=pl.BlockSpec((None, tm, n), lambda bi, i: (bi, i, 0)),
        out_shape=jax.ShapeDtypeStruct((b, t, n), BF16),
        compiler_params=_cparams("parallel", "parallel"),
        name="norm_proj",
    )(*srcs, norm_w.reshape(1, d), mod, w)


def _ret_kernel(q_ref, k_ref, v_ref, z_ref, cos_ref, sin_ref, rd_ref, o_ref, y_acc, st_ref, dm_ref, qd_ref, kd_ref,
                *, n_lat, n_ctx):
    c = RET_CHUNK
    lane = _lane_ids((1, LANES))
    lane_lo = lane < HD
    rope_lo = (lane % HD) < (HD // 2)
    pos = lax.broadcasted_iota(jnp.int32, (c, 1), 0).astype(F32)
    ti = lax.broadcasted_iota(jnp.int32, (c, c), 0)
    si = lax.broadcasted_iota(jnp.int32, (c, c), 1)
    same_head = (_row_ids((LANES, LANES)) < HD) == (_lane_ids((LANES, LANES)) < HD)
    ones_bd = same_head.astype(BF16)
    inst = [(d, pr) for d in range(2) for pr in range(PAIRS)]
    sl = lambda pr: slice(pr * LANES, (pr + 1) * LANES)

    g_chunk = []
    for d, pr in inst:
        lg_a = -jnp.exp(rd_ref[d, 2 * pr])
        lg_b = -jnp.exp(rd_ref[d, 2 * pr + 1])
        lgv = jnp.where(lane_lo, lg_a, lg_b)
        diff = ((ti - si) if d == 0 else (si - ti)).astype(F32)
        keep = diff >= 0
        dmat = lambda lg: jnp.where(keep, jnp.exp(jnp.where(keep, diff, 0.0) * lg), 0.0)
        dm_ref[d, pr] = jnp.concatenate([dmat(lg_a), dmat(lg_b)], axis=0)
        qd_ref[d, :, sl(pr)] = jnp.exp(((pos + 1.0) if d == 0 else (c - pos)) * lgv)
        kd_ref[d, :, sl(pr)] = jnp.exp(((c - 1.0 - pos) if d == 0 else pos) * lgv)
        g_chunk.append(jnp.exp(c * lgv))
    y_acc[...] = jnp.zeros_like(y_acc)
    st_ref[...] = jnp.zeros_like(st_ref)

    def rope(x, rows):
        partner = jnp.where(rope_lo, pltpu.roll(x, LANES - HD // 2, 1), pltpu.roll(x, HD // 2, 1))
        return x * cos_ref[rows, :] + partner * sin_ref[rows, :]

    def stack2(x):
        return jnp.concatenate([jnp.where(lane_lo, x, 0.0), jnp.where(lane_lo, 0.0, x)], axis=0)

    steps = max(s for s in range(1, RET_STEPS + 1) if (n_lat + n_ctx) % s == 0)

    def step(i, carry):
        work = [(sub,) + dp for sub in range(steps) for dp in inst]
        rows = [[pl.ds(pl.multiple_of(_scan_chunk_id(steps * i + sub, n_lat, n_ctx, d == 1) * c, c), c)
                 for d in range(2)] for sub in range(steps)]
        each = lambda fn: [fn(j, *work[j]) for j in range(len(work))]
        q = each(lambda j, sub, d, pr: rope(q_ref[rows[sub][d], sl(pr)].astype(F32), rows[sub][d]))
        k = each(lambda j, sub, d, pr: rope(k_ref[rows[sub][d], sl(pr)].astype(F32) * HD ** -0.5, rows[sub][d]))
        v = each(lambda j, sub, d, pr: v_ref[rows[sub][d], sl(pr)])
        scores = each(lambda j, sub, d, pr: _bdot_nt(stack2(q[j]), k[j]) * dm_ref[d, pr])
        upd = each(lambda j, sub, d, pr: _bdot_tn(k[j] * kd_ref[d, :, sl(pr)], v[j]))
        intra = each(lambda j, sub, d, pr: _bdot(scores[j], v[j]))
        state = [st_ref[d, pr] for d, pr in inst]
        for sub in range(steps):
            for n, (d, pr) in enumerate(inst):
                j = sub * len(inst) + n
                cross = _bdot(q[j] * qd_ref[d, :, sl(pr)], state[n])
                state[n] = state[n] * g_chunk[n] + jnp.where(same_head, upd[j], 0.0)
                y_acc[rows[sub][d], sl(pr)] += jnp.where(lane_lo, intra[j][:c], intra[j][c:]) + cross
        for n, (d, pr) in enumerate(inst):
            st_ref[d, pr] = state[n]
        return carry

    lax.fori_loop(0, (n_lat + n_ctx) // steps, step, 0)

    def finish(i, carry):
        rows = pl.ds(pl.multiple_of(i * ROW_TILE, ROW_TILE), ROW_TILE)
        y = y_acc[rows, :]
        ms = _head_sum_mxu(y * y, ones_bd) * (1.0 / HD)
        o_ref[rows, :] = (y * lax.rsqrt(ms + NORM_EPS) * _silu(z_ref[rows, :].astype(F32))).astype(o_ref.dtype)
        return carry

    lax.fori_loop(0, (n_lat + n_ctx) * c // ROW_TILE, finish, 0)


def _retention(u, cos_t, sin_t, ret_decay, n_lat_rows, col_q, col_k, col_v, col_z):
    b, t, _ = u.shape
    c = RET_CHUNK
    rd = jnp.broadcast_to(ret_decay.astype(F32)[:, :, None, None], (2, HEADS, 1, LANES))
    blk = lambda col: pl.BlockSpec((None, t, W), lambda bi, col=col: (bi, 0, col // W))
    return pl.pallas_call(
        functools.partial(_ret_kernel, n_lat=n_lat_rows // c, n_ctx=(t - n_lat_rows) // c),
        grid=(b,),
        in_specs=[
            blk(col_q), blk(col_k), blk(col_v), blk(col_z),
            pl.BlockSpec((t, LANES), lambda bi: (0, 0)),
            pl.BlockSpec((t, LANES), lambda bi: (0, 0)),
            pl.BlockSpec((2, HEADS, 1, LANES), lambda bi: (0, 0, 0, 0)),
        ],
        out_specs=pl.BlockSpec((None, t, W), lambda bi: (bi, 0, 0)),
        out_shape=jax.ShapeDtypeStruct((b, t, W), BF16),
        scratch_shapes=[pltpu.VMEM((t, W), F32), pltpu.VMEM((2, PAIRS, LANES, LANES), F32),
                        pltpu.VMEM((2, PAIRS, 2 * c, c), F32), pltpu.VMEM((2, c, W), F32),
                        pltpu.VMEM((2, c, W), F32)],
        compiler_params=_cparams("parallel"),
        name="retention",
    )(u, u, u, u, cos_t, sin_t, rd)


def _rope_tables(n_lat_rows, t):
    pos = jnp.arange(n_lat_rows)
    row = (pos // GRID_W).astype(F32)
    col = (pos % GRID_W).astype(F32)
    nf = HD // 4
    inv = ROPE_BASE ** (-jnp.arange(nf, dtype=F32) / nf)
    ang = jnp.concatenate([row[:, None] * inv, col[:, None] * inv], axis=-1)
    cos, sin = jnp.cos(ang), jnp.sin(ang)
    cos_h = jnp.concatenate([cos, cos], axis=-1)
    sin_h = jnp.concatenate([-sin, sin], axis=-1)
    reps = LANES // HD
    cos_t = jnp.concatenate([jnp.tile(cos_h, (1, reps)), jnp.ones((t - n_lat_rows, LANES), F32)], axis=0)
    sin_t = jnp.concatenate([jnp.tile(sin_h, (1, reps)), jnp.zeros((t - n_lat_rows, LANES), F32)], axis=0)
    return cos_t, sin_t


def _rwkv_kernel(r_ref, k_ref, v_ref, lo_ref, z_ref, mu_ref, mulo_ref, kkw_ref, ka_ref, w0_ref, a0_ref, w2_ref,
                 a2_ref, rk_ref, lnw_ref, lnb_ref, o_ref, y_acc, b_acc, st_ref, prep_ref, bonus_ref, plast_ref, tinv_ref,
                 aakv_ref, arbk_ref, *, n_lat, n_ctx):
    c = RW_CHUNK
    c2 = 2 * c
    t_rows = (n_lat + n_ctx) * c
    lat_rows = n_lat * c
    lane_lo = _lane_ids((1, LANES)) < HD
    ri = _row_ids((c2, c2))
    ci = _lane_ids((c2, c2))
    same_head = (ri < c) == (ci < c)
    rt_, ct_ = ri % c, ci % c
    same_sub = (rt_ // RW_SUB) == (ct_ // RW_SUB)
    eye = (ri == ci).astype(F32)
    ones_bd = same_head.astype(BF16)
    t64 = lax.broadcasted_iota(jnp.int32, (c, c), 0)
    s64 = lax.broadcasted_iota(jnp.int32, (c, c), 1)
    masks = []
    for d in range(2):
        before = (ct_ < rt_) if d == 0 else (ct_ > rt_)
        strict = jnp.logical_and(same_head, before)
        incl = jnp.logical_and(same_head, jnp.logical_or(before, ct_ == rt_))
        tri = ((s64 <= t64) if d == 0 else (s64 >= t64)).astype(BF16)
        masks.append((strict, incl, jnp.logical_and(strict, same_sub), tri))

    y_acc[...] = jnp.zeros_like(y_acc)
    b_acc[...] = jnp.zeros_like(b_acc)
    st_ref[...] = jnp.zeros_like(st_ref)

    def stack2(x):
        return jnp.concatenate([jnp.where(lane_lo, x, 0.0), jnp.where(lane_lo, 0.0, x)], axis=0)

    def dup(x):
        return jnp.concatenate([x, x], axis=0)

    def unstack(x):
        return jnp.where(lane_lo, x[:c], x[c:])

    sh_t = lax.broadcasted_iota(jnp.int32, (c, c + 2 * HALO), 0) + HALO
    sh_s = lax.broadcasted_iota(jnp.int32, (c, c + 2 * HALO), 1)
    shift_base = jnp.where(jnp.abs(sh_s - sh_t) == 1, 0.5, 0.0)

    n_steps = n_lat + n_ctx
    fields = ("at", "bt", "kt", "rt", "v")

    def step_rows(i, d):
        return pl.ds(pl.multiple_of(_scan_chunk_id(i, n_lat, n_ctx, d == 1) * c, c), c)

    def prepare(i, d, slot):
        tri = masks[d][3]
        c0 = pl.multiple_of(_scan_chunk_id(i, n_lat, n_ctx, d == 1) * c, c)
        p0 = pl.multiple_of(jnp.maximum(c0 - HALO, 0), HALO)
        n0 = pl.multiple_of(jnp.minimum(c0 + c, t_rows - HALO), HALO)
        no_prev = jnp.logical_or(c0 == 0, c0 == lat_rows)
        no_next = jnp.logical_or(c0 + c == lat_rows, c0 + c == t_rows)
        dead_prev = jnp.where(no_prev, HALO - 1, -1)
        dead_next = jnp.where(no_next, HALO + c, -1)
        dead = jnp.logical_or(sh_s == dead_prev, sh_s == dead_next)
        shift = jnp.where(dead, 0.0, shift_base).astype(BF16)

        def mixed(ref, mu):
            x = ref[pl.ds(c0, c), :]
            ext = jnp.concatenate([ref[pl.ds(p0, HALO), :], x, ref[pl.ds(n0, HALO), :]], axis=0)
            x = x.astype(F32)
            return x + mu * (jnp.dot(shift, ext, preferred_element_type=F32) - x)

        r = mixed(r_ref, mu_ref[0])
        k = mixed(k_ref, mu_ref[1])
        v = mixed(v_ref, mu_ref[2])
        lo = mixed(lo_ref, mulo_ref[...])
        kk = k * kkw_ref[...]
        yield
        kk_norm = jnp.sqrt(_head_sum_mxu(kk * kk, ones_bd))
        w_pre = _bdot(jnp.tanh(lo[:, :LANES]), w2_ref[d])
        a_pre = _bdot(lo[:, LANES:], a2_ref[d])
        yield
        kk = kk / jnp.maximum(kk_norm, 1e-12)
        logw = -math.exp(-0.5) * _sigmoid(w0_ref[d] + w_pre)
        a = _sigmoid(a0_ref[d] + a_pre)
        k_d = k * (1.0 + (a - 1.0) * ka_ref[...])
        lw_hi = logw.astype(BF16)
        lw_lo = (logw - lw_hi.astype(F32)).astype(BF16)
        cum = jnp.dot(tri, lw_hi, preferred_element_type=F32) + jnp.dot(tri, lw_lo, preferred_element_type=F32)
        bonus_sum = jnp.concatenate(
            [_bdot((r * k_d * rk_ref[...])[:, pr * LANES:(pr + 1) * LANES], ones_bd) for pr in range(PAIRS)], axis=1)
        yield
        p = jnp.exp(cum)
        inv_p = jnp.exp(-cum)
        staged = dict(at=-kk * jnp.exp(cum - logw), bt=kk * a * inv_p, kt=k_d * inv_p, rt=r * p, v=v,
                      bonus=bonus_sum * v)
        yield "stores next"
        for f, name in enumerate(fields):
            prep_ref[slot, d, f] = staged[name].astype(BF16)
        bonus_ref[slot, d] = staged["bonus"]
        plast_ref[slot, d] = p[c - 1:c, :] if d == 0 else p[0:1, :]

    everyone = [(d, pr) for d in range(2) for pr in range(PAIRS)]
    each = lambda fn: [fn(j, *everyone[j]) for j in range(len(everyone))]
    lanes = lambda pr: slice(pr * LANES, (pr + 1) * LANES)

    def independent_half(slot):
        op = lambda name, d, pr: prep_ref[slot, d, fields.index(name), :, lanes(pr)]
        rhs = each(lambda j, d, pr: jnp.concatenate([dup(op("bt", d, pr)), dup(op("kt", d, pr))], axis=0).astype(BF16))
        top = each(lambda j, d, pr: _bdot_nt(stack2(op("at", d, pr)), rhs[j]))
        yield
        n_all = each(lambda j, d, pr: jnp.where(masks[d][0], top[j][:, :c2], 0.0))
        aak_v = each(lambda j, d, pr: _bdot(jnp.where(masks[d][0], top[j][:, c2:], 0.0),
                                            dup(op("v", d, pr)).astype(BF16)))
        n_tr = each(lambda j, d, pr: n_all[j].T)
        nsub_tr = each(lambda j, d, pr: jnp.where(masks[1 - d][2], n_tr[j], 0.0))
        t_tr = each(lambda j, d, pr: (eye + nsub_tr[j]).astype(BF16))
        p_tr = each(lambda j, d, pr: _bdot(nsub_tr[j], nsub_tr[j]).astype(BF16))
        yield
        span = 2
        while 2 * span < RW_SUB:
            both = each(lambda j, d, pr: _bdot(p_tr[j], jnp.concatenate([t_tr[j], p_tr[j]], axis=1)).astype(BF16))
            t_tr = each(lambda j, d, pr: t_tr[j] + both[j][:, :c2])
            p_tr = each(lambda j, d, pr: both[j][:, c2:])
            span *= 2
            yield
        t_tr = each(lambda j, d, pr: t_tr[j] + _bdot(p_tr[j], t_tr[j]).astype(BF16))
        yield
        corr = each(lambda j, d, pr: _bdot(n_tr[j] - nsub_tr[j], t_tr[j]))
        bot = each(lambda j, d, pr: _bdot_nt(stack2(op("rt", d, pr)), rhs[j]))
        yield
        tinv_tr = each(lambda j, d, pr: t_tr[j] + _bdot(t_tr[j], corr[j]))
        for j, (d, pr) in enumerate(everyone):
            tinv_ref[slot, j] = tinv_tr[j].astype(BF16)
            aakv_ref[slot, j] = aak_v[j]
            arbk_ref[slot, j] = jnp.where(jnp.concatenate([masks[d][1], masks[d][1]], axis=1), bot[j], 0.0).astype(BF16)

    def dependent_half(slot, i):
        op = lambda name, d, pr: prep_ref[slot, d, fields.index(name), :, lanes(pr)]
        rows = [step_rows(i, d) for d in range(2)]
        s0 = each(lambda j, d, pr: st_ref[d, pr])
        proj = each(lambda j, d, pr: _bdot_nt(jnp.concatenate([op("at", d, pr), op("rt", d, pr)], axis=0), s0[j]))
        yield
        u_st = each(lambda j, d, pr: _bdot_tn(tinv_ref[slot, j], dup(proj[j][:c]) + aakv_ref[slot, j]))
        yield
        y_st = each(lambda j, d, pr: _bdot(arbk_ref[slot, j], jnp.concatenate(
            [u_st[j].astype(BF16), dup(op("v", d, pr)).astype(BF16)], axis=0)))
        upd = each(lambda j, d, pr: _bdot_tn(jnp.concatenate([unstack(u_st[j]).astype(BF16), op("v", d, pr)], axis=0),
                                             jnp.concatenate([op("bt", d, pr), op("kt", d, pr)], axis=0)))
        yield
        for j, (d, pr) in enumerate(everyone):
            st_ref[d, pr] = (s0[j] + jnp.where(same_head, upd[j], 0.0)) * plast_ref[slot, d][:, lanes(pr)]
            y_acc[rows[d], lanes(pr)] += proj[j][c:] + unstack(y_st[j])
        for d in range(2):
            b_acc[rows[d], :] += bonus_ref[slot, d]

    def interleave(main, staging):
        main = list(main)
        held = [[g, None] for g in staging]
        turn = 0
        while main:
            for entry in list(main):
                g, every = entry
                if turn % every == 0 and next(g, "done") == "done":
                    main.remove(entry)
            for s in held:
                if s[1] is None:
                    s[1] = next(s[0])
            turn += 1
        for g, _ in held:
            for _ in g:
                pass

    slots = 2 * RW_DEPTH
    group = range(RW_DEPTH)

    def in_turn(gens):
        for g in gens:
            yield from g

    def dependent_halves(first):
        return (in_turn([dependent_half((first + s) % slots, first + s) for s in group]), 1)

    def stagers(first):
        return [prepare(jnp.minimum(first + s, n_steps - 1), d, (first + s) % slots) for s in group for d in range(2)]

    interleave([(g, 1) for g in stagers(0)], [])
    interleave([(independent_half(s), 1) for s in group], stagers(RW_DEPTH))

    def step(i, carry):
        base = i * RW_DEPTH
        interleave([(independent_half((base + s) % slots), 1) for s in group] + [dependent_halves(base - RW_DEPTH)],
                   stagers(base + RW_DEPTH))
        return carry

    lax.fori_loop(1, n_steps // RW_DEPTH, step, 0)
    interleave([dependent_halves(n_steps - RW_DEPTH)], [])

    def finish(i, carry):
        rows = pl.ds(pl.multiple_of(i * ROW_TILE, ROW_TILE), ROW_TILE)
        y = y_acc[rows, :]
        mean = _head_sum_mxu(y, ones_bd) * (1.0 / HD)
        yc = y - mean
        var = _head_sum_mxu(yc * yc, ones_bd) * (1.0 / HD)
        gn = yc * lax.rsqrt(var + RWKV_GN_EPS) * lnw_ref[...] + lnb_ref[...]
        o_ref[rows, :] = ((gn + b_acc[rows, :]) * _silu(z_ref[rows, :].astype(F32))).astype(o_ref.dtype)
        return carry

    lax.fori_loop(0, t_rows // ROW_TILE, finish, 0)


def _rwkv(u, n_lat_rows, col_shift, col_z, rw_mu, rw_w0, rw_w2, rw_a0, rw_a2, rw_kk, rw_ka, rw_rk, rw_ln_w,
          rw_ln_b):
    b, t, _ = u.shape
    c = RW_CHUNK
    assert c == HD and (t // c) % RW_DEPTH == 0
    slots = 2 * RW_DEPTH
    col_lo = col_shift + 3 * W
    def pad_dirs(w2):
        z = jnp.zeros_like(w2[0])
        return jnp.stack([jnp.concatenate([w2[0], z], axis=0), jnp.concatenate([z, w2[1]], axis=0)]).astype(BF16)

    row = lambda a: a.reshape(1, W).astype(F32)
    blk = lambda col, width=W: pl.BlockSpec((None, t, width), lambda bi, col=col, width=width: (bi, 0, col // width))
    full = lambda shape: pl.BlockSpec(shape, lambda bi: (0,) * len(shape))
    return pl.pallas_call(
        functools.partial(_rwkv_kernel, n_lat=n_lat_rows // c, n_ctx=(t - n_lat_rows) // c),
        grid=(b,),
        in_specs=[
            blk(col_shift), blk(col_shift + W), blk(col_shift + 2 * W), blk(col_lo, 2 * LANES), blk(col_z),
            full((3, 1, W)), full((1, 2 * LANES)), full((1, W)), full((1, W)), full((2, 1, W)), full((2, 1, W)),
            full((2, 2 * LORA, W)), full((2, 2 * LORA, W)), full((1, W)), full((1, W)), full((1, W)),
        ],
        out_specs=pl.BlockSpec((None, t, W), lambda bi: (bi, 0, 0)),
        out_shape=jax.ShapeDtypeStruct((b, t, W), BF16),
        scratch_shapes=[pltpu.VMEM((t, W), F32), pltpu.VMEM((t, W), F32),
                        pltpu.VMEM((2, PAIRS, LANES, LANES), F32), pltpu.VMEM((slots, 2, 5, c, W), BF16),
                        pltpu.VMEM((slots, 2, c, W), F32), pltpu.VMEM((slots, 2, 1, W), F32),
                        pltpu.VMEM((slots, 2 * PAIRS, 2 * c, 2 * c), BF16),
                        pltpu.VMEM((slots, 2 * PAIRS, 2 * c, LANES), F32),
                        pltpu.VMEM((slots, 2 * PAIRS, 2 * c, 4 * c), BF16)],
        compiler_params=_cparams("parallel"),
        name="rwkv7",
    )(u, u, u, u, u, rw_mu[:3 * W].reshape(3, 1, W), rw_mu[3 * W:].reshape(1, 2 * LANES), row(rw_kk), row(rw_ka),
      rw_w0.reshape(2, 1, W), rw_a0.reshape(2, 1, W), pad_dirs(rw_w2), pad_dirs(rw_a2), row(rw_rk), row(rw_ln_w),
      row(rw_ln_b))


def _out_proj_kernel(*refs, n_src, lat_tiles, tail, ctx_cols):
    a_ref, b_ref, w_ref, mod_ref = refs[n_src:n_src + 4]
    wa = a_ref.shape[-1]
    y = _bdot(a_ref[...], w_ref[:wa, :]) + _bdot(b_ref[...], w_ref[wa:, :])
    hn = _read_token_rows(refs[:n_src], lat_tiles) + mod_ref[2:3, :] * y
    if tail == "final_norm":
        fnw_ref, o_ref = refs[n_src + 4:]
        o_ref[...] = hn * lax.rsqrt(jnp.mean(hn * hn, axis=-1, keepdims=True) + NORM_EPS) * fnw_ref[...]
    else:
        nw_ref, mod2_ref, w2_ref, o_ref, u_ref = refs[n_src + 4:]
        o_ref[...] = hn
        yn = hn * lax.rsqrt(jnp.mean(hn * hn, axis=-1, keepdims=True) + NORM_EPS) * nw_ref[...]
        n = yn * (1.0 + mod2_ref[1:2, :]) + mod2_ref[0:1, :]
        is_ctx = pl.program_id(1) >= lat_tiles

        @pl.when(jnp.logical_not(is_ctx))
        def _():
            u_ref[...] = _bdot(n, w2_ref[...]).astype(u_ref.dtype)

        @pl.when(is_ctx)
        def _():
            lo, hi = ctx_cols
            u_ref[...] = jnp.zeros_like(u_ref)
            u_ref[:, lo:hi] = _bdot(n, w2_ref[:, lo:hi]).astype(u_ref.dtype)


def _out_proj(a, b, w, h, mod, n_lat, final_norm_w=None, next_proj=None):
    bsz, rows, wa = a.shape
    d = w.shape[1]
    tm = ROW_TILE
    lat_tiles = n_lat // tm
    src_specs, srcs = _token_rows(h, tm, lat_tiles)
    kind = lambda bi, i: (bi, (i >= lat_tiles).astype(jnp.int32), 0, 0)
    in_specs = src_specs + [
        pl.BlockSpec((None, tm, wa), lambda bi, i: (bi, i, 0)),
        pl.BlockSpec((None, tm, b.shape[-1]), lambda bi, i: (bi, i, 0)),
        pl.BlockSpec(w.shape, lambda bi, i: (0, 0)),
        pl.BlockSpec((None, None, 3, d), kind),
    ]
    args = srcs + [a, b, w, mod]
    out_specs = pl.BlockSpec((None, tm, d), lambda bi, i: (bi, i, 0))
    out_shape = jax.ShapeDtypeStruct((bsz, rows, d), F32)
    ctx_cols = None
    if final_norm_w is not None:
        tail = "final_norm"
        in_specs.append(pl.BlockSpec((1, d), lambda bi, i: (0, 0)))
        args.append(final_norm_w.reshape(1, d))
    else:
        tail = "next_proj"
        norm_w, mod2, w2, ctx_cols = next_proj
        n2 = w2.shape[1]
        in_specs += [pl.BlockSpec((1, d), lambda bi, i: (0, 0)), pl.BlockSpec((None, None, 3, d), kind),
                     pl.BlockSpec((d, n2), lambda bi, i: (0, 0))]
        args += [norm_w.reshape(1, d), mod2, w2]
        out_specs = [out_specs, pl.BlockSpec((None, tm, n2), lambda bi, i: (bi, i, 0))]
        out_shape = [out_shape, jax.ShapeDtypeStruct((bsz, rows, n2), BF16)]
    return pl.pallas_call(
        functools.partial(_out_proj_kernel, n_src=len(srcs), lat_tiles=lat_tiles, tail=tail, ctx_cols=ctx_cols),
        grid=(bsz, rows // tm),
        in_specs=in_specs,
        out_specs=out_specs,
        out_shape=out_shape,
        compiler_params=_cparams("parallel", "parallel"),
        name="out_proj_" + tail,
    )(*args)


HY_FAST = 64
HY_SUB = 8
HY_CH_TILE = 256
HY_UNROLL = 32


def _hy_dft_tables(l):
    n = 2 * l
    half = l // HY_FAST
    k1 = jnp.arange(half, dtype=jnp.int32)[None, :, None]
    n1 = jnp.arange(half, dtype=jnp.int32)[None, None, :]
    n2 = jnp.arange(HY_FAST, dtype=jnp.int32)[:, None, None]
    ang = (jnp.pi / n) * (((2 * k1 + 1) * (HY_FAST * n1 + n2)) % (2 * n)).astype(F32)
    m1 = jnp.concatenate([jnp.cos(ang), -jnp.sin(ang)], axis=1)
    groups = HY_FAST // HY_SUB
    m1g = m1.reshape(groups, HY_SUB, 2 * half, half)
    jj = np.arange(HY_SUB)[:, None, None]
    row_sel = (np.arange(2 * half * HY_SUB)[None, :, None] == np.arange(2 * half)[None, None, :] * HY_SUB + jj)
    col_sel = (np.arange(half * HY_SUB)[None, None, :] == np.arange(half)[None, :, None] * HY_SUB + jj)
    place = lambda out: jnp.einsum("jRr,gjrn,jnC->" + out, row_sel.astype(np.float32), m1g,
                                   col_sel.astype(np.float32), precision=lax.Precision.HIGHEST).astype(BF16)
    big, big_t = place("gRC"), place("gCR")
    kk = jnp.arange(HY_FAST, dtype=jnp.int32)
    ang2 = (2.0 * jnp.pi / HY_FAST) * ((kk[:, None] * kk[None, :]) % HY_FAST).astype(F32)
    wr, wi = jnp.cos(ang2), -jnp.sin(ang2)
    w2 = jnp.concatenate([jnp.concatenate([wr, -wi], axis=1), jnp.concatenate([wi, wr], axis=1)], axis=0)
    w2_inv = jnp.concatenate([jnp.concatenate([wr, wi], axis=1), jnp.concatenate([-wi, wr], axis=1)], axis=0)
    return big, big_t, w2.astype(BF16), w2_inv.astype(BF16)


def _hy_stage1(src_ref, m1_ref, bs_ref):
    half, _, c = src_ref.shape
    for g in range(HY_FAST // HY_SUB):
        cols = pl.ds(g * HY_SUB, HY_SUB)
        x = src_ref[:, cols, :].reshape(half * HY_SUB, c).astype(BF16)
        out = jnp.dot(m1_ref[g], x, preferred_element_type=F32)
        bs_ref[:, cols, :] = out.reshape(2 * half, HY_SUB, c)


def _hy_stage2(bs_ref, w2_ref, k1):
    half = bs_ref.shape[0] // 2
    return jnp.dot(w2_ref[...], jnp.concatenate([bs_ref[k1], bs_ref[half + k1]], axis=0).astype(BF16),
                   preferred_element_type=F32)


def _hy_filter_kernel(z_ref, w1_ref, b1_ref, f1_ref, w2_ref, b2_ref, f2_ref, w3_ref, dec_ref, o_ref, hid_ref):
    j = pl.program_id(0)

    @pl.when(j == 0)
    def _():
        hid = jnp.sin(f1_ref[...] * (_dot_hi(z_ref[...], w1_ref[...]) + b1_ref[...]))
        hid_ref[...] = jnp.sin(f2_ref[...] * (_dot_hi(hid, w2_ref[...]) + b2_ref[...]))

    h = _dot_hi(hid_ref[...], w3_ref[...]) * dec_ref[...]
    row = lax.broadcasted_iota(jnp.int32, h.shape, 0)
    o_ref[...] = jnp.where(jnp.logical_and(j % 2 == 1, row == 0), 0.0, h).astype(o_ref.dtype)


def _hy_filters(l, w1, b1, f1, w2, b2, f2, w3):
    t = jnp.linspace(0.0, 1.0, l)[:, None]
    bands = jnp.linspace(1e-4, HY_BANDS - 1, HY_BANDS)
    ang = (2.0 * math.pi / l) * jnp.arange(l, dtype=F32)[:, None] * bands[None]
    z = jnp.concatenate([t, jnp.cos(ang), -jnp.sin(ang)], axis=-1)
    emb = z.shape[1]
    emb_pad = 64
    z = jnp.pad(z, ((0, 0), (0, emb_pad - emb)))
    w1p = jnp.pad(w1, ((0, emb_pad - emb), (0, 0)))
    deltas = jnp.abs(jnp.linspace(math.log(HY_TARGET) / HY_SLOW_PCT, math.log(HY_TARGET) / HY_FAST_PCT, W))
    dec = jnp.exp(-t * deltas)
    ffn = w1.shape[1]
    nblk = w3.shape[1] // W
    full = lambda shape: pl.BlockSpec(shape, lambda j: (0,) * len(shape))
    return pl.pallas_call(
        _hy_filter_kernel,
        grid=(nblk,),
        in_specs=[full((l, emb_pad)), full((emb_pad, ffn)), full((1, ffn)), full((1, ffn)), full((ffn, ffn)),
                  full((1, ffn)), full((1, ffn)), pl.BlockSpec((ffn, W), lambda j: (0, j)), full((l, W))],
        out_specs=pl.BlockSpec((l, W), lambda j: (0, j)),
        out_shape=jax.ShapeDtypeStruct((l, nblk * W), F32),
        scratch_shapes=[pltpu.VMEM((l, ffn), F32)],
        compiler_params=_cparams("arbitrary"),
        name="hyena_filters",
    )(z, w1p, b1.reshape(1, -1), f1.reshape(1, -1), w2, b2.reshape(1, -1), f2.reshape(1, -1), w3, dec)


def _hy_spectrum_kernel(hf_ref, hb_ref, m1_ref, w2_ref, o_ref, bs_ref, xf_ref, *, half):
    _hy_stage1(hf_ref, m1_ref, bs_ref)

    def forward(k1, carry):
        xf_ref[k1] = _hy_stage2(bs_ref, w2_ref, k1)
        return carry

    lax.fori_loop(0, half, forward, 0, unroll=min(HY_UNROLL, half))
    _hy_stage1(hb_ref, m1_ref, bs_ref)

    def combine(k1, carry):
        xf = xf_ref[k1]
        xb = _hy_stage2(bs_ref, w2_ref, k1)
        o_ref[k1] = jnp.concatenate([xf[:HY_FAST] + xb[:HY_FAST], xf[HY_FAST:] - xb[HY_FAST:]],
                                    axis=0).astype(o_ref.dtype)
        return carry

    lax.fori_loop(0, half, combine, 0, unroll=min(HY_UNROLL, half))


def _hy_spectra(filt, m1, w2):
    l = filt.shape[0]
    half = l // HY_FAST
    ct = HY_CH_TILE
    nct = W // ct
    orders = filt.shape[1] // (2 * W)
    filt = filt.reshape(half, HY_FAST, filt.shape[1])
    return pl.pallas_call(
        functools.partial(_hy_spectrum_kernel, half=half),
        grid=(orders, nct),
        in_specs=[pl.BlockSpec((half, HY_FAST, ct), lambda o, c: (0, 0, 2 * o * nct + c)),
                  pl.BlockSpec((half, HY_FAST, ct), lambda o, c: (0, 0, (2 * o + 1) * nct + c)),
                  pl.BlockSpec(m1.shape, lambda o, c: (0, 0, 0)), pl.BlockSpec(w2.shape, lambda o, c: (0, 0))],
        out_specs=pl.BlockSpec((None, half, 2 * HY_FAST, ct), lambda o, c: (o, 0, 0, c)),
        out_shape=jax.ShapeDtypeStruct((orders, half, 2 * HY_FAST, W), BF16),
        scratch_shapes=[pltpu.VMEM((2 * half, HY_FAST, ct), F32), pltpu.VMEM((half, 2 * HY_FAST, ct), F32)],
        compiler_params=_cparams("parallel", "parallel"),
        name="hyena_spectra",
    )(filt, filt, m1, w2)


def _hyena_kernel(v_ref, x1_ref, x2_ref, g_ref, cw_ref, cb_ref, bias_ref, m1_ref, m1t_ref, w2_ref, w2i_ref, gs_ref,
                  o_ref, z_ref, bs_ref, cs_ref, acc_ref, *, l):
    o = pl.program_id(2)
    rc = ROW_TILE
    half = l // HY_FAST
    ct = z_ref.shape[-1]

    def conv_rows(ref, sec, c0):
        x, prev, nxt = _with_neighbours(ref, slice(None), c0, rc, l, l)
        w = cw_ref[sec]
        return prev * w[0:1, :] + x * w[1:2, :] + nxt * w[2:3, :] + cb_ref[sec]

    slabs = rc // HY_FAST

    @pl.when(o == 0)
    def _():
        def body(i, carry):
            c0 = pl.multiple_of(i * rc, rc)
            z_ref[pl.ds(i * slabs, slabs)] = conv_rows(v_ref, 0, c0).reshape(slabs, HY_FAST, ct)
            return carry

        lax.fori_loop(0, l // rc, body, 0)

    _hy_stage1(z_ref, m1_ref, bs_ref)

    def per_k1(k1, carry):
        x = _hy_stage2(bs_ref, w2_ref, k1)
        xr, xi = x[:HY_FAST], x[HY_FAST:]
        g = gs_ref[k1].astype(F32)
        gr, gi = g[:HY_FAST], g[HY_FAST:]
        y = jnp.concatenate([xr * gr - xi * gi, xr * gi + xi * gr], axis=0).astype(BF16)
        cs_ref[k1] = jnp.dot(w2i_ref[...], y, preferred_element_type=F32)
        return carry

    lax.fori_loop(0, half, per_k1, 0, unroll=min(HY_UNROLL, half))

    for g in range(HY_FAST // HY_SUB):
        c_re = cs_ref[:, pl.ds(g * HY_SUB, HY_SUB), :].reshape(half * HY_SUB, ct)
        c_im = cs_ref[:, pl.ds(HY_FAST + g * HY_SUB, HY_SUB), :].reshape(half * HY_SUB, ct)
        out = jnp.dot(m1t_ref[g], jnp.concatenate([c_re, c_im], axis=0).astype(BF16), preferred_element_type=F32)
        acc_ref[:, pl.ds(g * HY_SUB, HY_SUB), :] = out.reshape(half, HY_SUB, ct)

    def finish(i, carry):
        c0 = pl.multiple_of(i * rc, rc)
        rows = pl.ds(c0, rc)
        tile = pl.ds(i * slabs, slabs)
        conv = (acc_ref[tile] * (1.0 / l) + z_ref[tile] * bias_ref[o]).reshape(rc, ct)

        @pl.when(o == 0)
        def _():
            z_ref[tile] = (conv_rows(x1_ref, 1, c0) * conv).reshape(slabs, HY_FAST, ct)

        @pl.when(o == 1)
        def _():
            zn = conv_rows(x2_ref, 2, c0) * conv
            o_ref[rows, :] = (zn * _silu(g_ref[rows, :].astype(F32))).astype(o_ref.dtype)

        return carry

    lax.fori_loop(0, l // rc, finish, 0)


def _hyena(u, l, conv_w, conv_b, hy_bias, tables, spectra):
    b = u.shape[0]
    ct = HY_CH_TILE
    nct = W // ct
    half = l // HY_FAST
    sec = lambda s: pl.BlockSpec((None, l, ct), lambda bi, c, o, s=s: (bi, 0, s * nct + c))
    cw = jnp.transpose(conv_w.reshape(3, 3, W), (1, 0, 2))
    whole = lambda a: pl.BlockSpec(a.shape, lambda bi, c, o: (0,) * a.ndim)
    return pl.pallas_call(
        functools.partial(_hyena_kernel, l=l),
        grid=(b, nct, 2),
        in_specs=[
            sec(0), sec(1), sec(2), sec(3),
            pl.BlockSpec((3, 3, ct), lambda bi, c, o: (0, 0, c)),
            pl.BlockSpec((3, 1, ct), lambda bi, c, o: (0, 0, c)),
            pl.BlockSpec((2, 1, ct), lambda bi, c, o: (0, 0, c)),
            whole(tables[0]), whole(tables[1]), whole(tables[2]), whole(tables[3]),
            pl.BlockSpec((None, half, 2 * HY_FAST, ct), lambda bi, c, o: (o, 0, 0, c)),
        ],
        out_specs=pl.BlockSpec((None, l, ct), lambda bi, c, o: (bi, 0, c)),
        out_shape=jax.ShapeDtypeStruct((b, l, W), BF16),
        scratch_shapes=[pltpu.VMEM((half, HY_FAST, ct), F32), pltpu.VMEM((2 * half, HY_FAST, ct), F32),
                        pltpu.VMEM((half, 2 * HY_FAST, ct), F32), pltpu.VMEM((half, HY_FAST, ct), F32)],
        compiler_params=_cparams("parallel", "parallel", "arbitrary"),
        name="hyena",
    )(u, u, u, u, cw, conv_b.reshape(3, 1, W), hy_bias.reshape(2, 1, W), *tables, spectra)


NA_ROWS_PER_STEP = 32
NEG_BIG = -1e30


def _na_bias_tables(rpb, rows):
    heads = rpb.shape[0]
    n_dr, n_dc = 2 * NA_WIN_R - 1, 2 * NA_WIN_C - 1
    cq = np.arange(GRID_W)[:, None]
    kc = np.arange(GRID_W)[None, :]
    c0 = np.clip(cq - NA_WIN_C // 2, 0, GRID_W - NA_WIN_C)
    col_ok = (kc >= c0) & (kc < c0 + NA_WIN_C)
    col_sel = (np.arange(n_dc)[:, None, None] == (kc - cq + NA_WIN_C - 1)[None]).astype(np.float32)
    tiles = jnp.einsum("phrj,jcx->phrcx", rpb.astype(F32).reshape(heads // 2, 2, n_dr, n_dc), col_sel,
                       precision=lax.Precision.HIGHEST)
    tiles = jnp.where(col_ok, tiles, NEG_BIG)
    tiles = jnp.pad(tiles, ((0, 0), (0, 0), (0, 2), (0, 0), (0, 0)), constant_values=NEG_BIG)
    return jnp.concatenate([tiles[:, :, :n_dr + 1], tiles[:, :, 1:]], axis=-1)


def _na_kernel(q_ref, k_ref, v_ref, z_ref, bias_ref, o_ref, *, l):
    rows = l // GRID_W
    kn = NA_WIN_R * GRID_W
    n_ctx = k_ref.shape[0] - l
    lane_lo = _lane_ids((1, LANES)) < HD
    k_ctx = k_ref[pl.ds(l, n_ctx), :]
    v_ctx = v_ref[pl.ds(l, n_ctx), :]
    per_step = min(NA_ROWS_PER_STEP, rows)

    def body(it, carry):
        each = lambda fn: [fn(j, it * per_step + j) for j in range(per_step)]
        r0 = each(lambda j, rq: jnp.clip(rq - NA_WIN_R // 2, 0, rows - NA_WIN_R))
        qrows = each(lambda j, rq: pl.ds(pl.multiple_of(rq * GRID_W, GRID_W), GRID_W))
        krows = each(lambda j, rq: pl.ds(pl.multiple_of(r0[j] * GRID_W, GRID_W), kn))

        def stacked_q(j, rq):
            q = q_ref[qrows[j], :].astype(F32) * HD ** -0.5
            return jnp.concatenate([jnp.where(lane_lo, q, 0.0), jnp.where(lane_lo, 0.0, q)], axis=0).astype(BF16)

        def window_bias(j, rq):
            base = r0[j] - rq + NA_WIN_R - 1
            return jnp.concatenate(
                [jnp.concatenate([bias_ref[hh, base + w] for w in range(0, NA_WIN_R, 2)], axis=1) for hh in range(2)],
                axis=0)

        q2 = each(stacked_q)
        s_win = each(lambda j, rq: _bdot_nt(q2[j], k_ref[krows[j], :]) + window_bias(j, rq))
        s_ctx = each(lambda j, rq: _bdot_nt(q2[j], k_ctx))
        m = each(lambda j, rq: jnp.maximum(jnp.max(s_win[j], axis=-1, keepdims=True),
                                           jnp.max(s_ctx[j], axis=-1, keepdims=True)))
        p_win = each(lambda j, rq: jnp.exp(s_win[j] - m[j]))
        p_ctx = each(lambda j, rq: jnp.exp(s_ctx[j] - m[j]))
        den = each(lambda j, rq: jnp.sum(p_win[j], axis=-1, keepdims=True) + jnp.sum(p_ctx[j], axis=-1, keepdims=True))
        out = each(lambda j, rq: (_bdot(p_win[j], v_ref[krows[j], :]) + _bdot(p_ctx[j], v_ctx)) / den[j])
        for j in range(per_step):
            gate = _silu(z_ref[qrows[j], :].astype(F32))
            o_ref[qrows[j], :] = (jnp.where(lane_lo, out[j][:GRID_W], out[j][GRID_W:]) * gate).astype(o_ref.dtype)
        return carry

    lax.fori_loop(0, rows // per_step, body, 0)


def _natten(u, l, col_q, col_k, col_v, col_z, bias):
    b, t, _ = u.shape
    lat = lambda col: pl.BlockSpec((None, l, LANES), lambda bi, p, col=col: (bi, 0, col // LANES + p))
    full = lambda col: pl.BlockSpec((None, t, LANES), lambda bi, p, col=col: (bi, 0, col // LANES + p))
    return pl.pallas_call(
        functools.partial(_na_kernel, l=l),
        grid=(b, PAIRS),
        in_specs=[lat(col_q), full(col_k), full(col_v), lat(col_z),
                  pl.BlockSpec((None,) + bias.shape[1:], lambda bi, p: (p, 0, 0, 0, 0))],
        out_specs=pl.BlockSpec((None, l, LANES), lambda bi, p: (bi, 0, p)),
        out_shape=jax.ShapeDtypeStruct((b, l, W), BF16),
        compiler_params=_cparams("parallel", "parallel"),
        name="natten",
    )(u, u, u, u, bias)


def kernel(x, c, ctx, c_ctx, ada_w, ada_b, norm_w, final_norm_w, even_w_in, even_w_out, ret_decay, rw_mu, rw_w0, rw_w2, rw_a0, rw_a2, rw_kk, rw_ka, rw_rk, rw_ln_w, rw_ln_b, odd_w_in, odd_w_out, hy_conv_w, hy_conv_b, hy_w1, hy_b1, hy_f1, hy_w2, hy_b2, hy_f2, hy_w3, hy_bias, na_rpb):
    b, l, d = x.shape
    n_ctx = ctx.shape[1]
    t = l + n_ctx
    depth = ada_w.shape[0]
    assert depth == 2 and l % ROW_TILE == 0 and n_ctx % ROW_TILE == 0 and (l // GRID_W) >= NA_WIN_R
    assert (l // GRID_W) % min(NA_ROWS_PER_STEP, l // GRID_W) == 0

    cond_rows = -(-(b + 1) // 8) * 8
    cond = jnp.concatenate([c, c_ctx[None, :], jnp.zeros((cond_rows - b - 1, d), F32)], axis=0)
    mod_all = _ada_mod(cond, ada_w, ada_b)
    mods = [jnp.stack([mod_all[i, :b].reshape(b, 3, d),
                       jnp.broadcast_to(mod_all[i, b].reshape(1, 3, d), (b, 3, d))], axis=1) for i in range(depth)]

    h = (x, ctx)

    col_rz, col_wz, col_shift = 3 * W, 4 * W, 5 * W
    u0 = _norm_proj(h, norm_w[0], mods[0], even_w_in[0].astype(BF16), l, t)
    cos_t, sin_t = _rope_tables(l, t)
    ret_o = _retention(u0, cos_t, sin_t, ret_decay[0], l, 0, W, 2 * W, col_rz)
    rw_o = _rwkv(u0, l, col_shift, col_wz, rw_mu[0], rw_w0[0], rw_w2[0], rw_a0[0], rw_a2[0], rw_kk[0], rw_ka[0],
                 rw_rk[0], rw_ln_w[0], rw_ln_b[0])
    h, u1 = _out_proj(ret_o, rw_o, even_w_out[0].astype(BF16), h, mods[0], l,
                      next_proj=(norm_w[1], mods[1], odd_w_in[0].astype(BF16), (5 * W, 7 * W)))
    tables = _hy_dft_tables(l)
    filt = _hy_filters(l, hy_w1[0], hy_b1[0], hy_f1[0], hy_w2[0], hy_b2[0], hy_f2[0], hy_w3[0])
    spectra = _hy_spectra(filt, tables[0], tables[2])
    hy_o = _hyena(u1, l, hy_conv_w[0], hy_conv_b[0], hy_bias[0], tables, spectra)
    na_o = _natten(u1, l, 4 * W, 5 * W, 6 * W, 7 * W, _na_bias_tables(na_rpb[0], l // GRID_W))
    return _out_proj(hy_o, na_o, odd_w_out[0].astype(BF16), h, mods[1], l, final_norm_w)
```
